```python
import math
import jax, jax.numpy as jnp
from jax import lax
import numpy as np

D_MODEL = 1024
BATCH = 32
SEQ = 256
DEPTH = 2
DEC_BATCH = 8
DEC_SEQ = 4096
PAST_LEN = 256

GRID_W = 64
D_A = 512
SSM_P = 16
SSM_G = D_A // SSM_P
SSM_N = 64
D_B = 512
SGU_HEADS = 8
SGU_HD = D_B // SGU_HEADS
CHUNK = 128
N_HEADS = 8
N_KV = 2
GQ = N_HEADS // N_KV
HEAD_DIM = 64
D_C = N_HEADS * HEAD_DIM
KV_W = N_KV * HEAD_DIM
WINDOW = 128
BLOCK = 128
ROPE_BASE = 10000.0
D_FF = 2816
CONV_W = 3
N_BRANCH = 3
EPS = 1e-6
OFF_B = D_A
OFF_Q = OFF_B + 2 * D_B
OFF_K = OFF_Q + D_C
OFF_V = OFF_K + KV_W
OFF_G = OFF_V + KV_W
D_IN = OFF_G + N_BRANCH * D_MODEL

kernel_name = 'hybrid_s5_sgu_swa_prefix_diffusion_step'

F32 = jnp.float32


def rmsnorm(x, g):
    xf = x.astype(F32)
    y = xf * lax.rsqrt(jnp.mean(xf * xf, axis=-1, keepdims=True) + EPS)
    return (y * g.astype(F32)).astype(x.dtype)


def rope_2d(x):
    T = x.shape[1]
    rows = T // GRID_W
    row = jnp.repeat(jnp.arange(rows), GRID_W)
    col = jnp.tile(jnp.arange(GRID_W), rows)
    half = HEAD_DIM // 2
    inv = ROPE_BASE ** (-jnp.arange(0, half, 2, dtype=F32) / half)

    def rot(xa, pos):
        ang = pos.astype(F32)[:, None] * inv[None, :]
        cos = jnp.cos(ang)[None, :, None, :]
        sin = jnp.sin(ang)[None, :, None, :]
        x1, x2 = jnp.split(xa.astype(F32), 2, axis=-1)
        return jnp.concatenate([x1 * cos - x2 * sin, x2 * cos + x1 * sin], axis=-1)

    out = jnp.concatenate([rot(x[..., :half], row), rot(x[..., half:], col)], axis=-1)
    return out.astype(x.dtype)


def _cplx_combine(e1, e2):
    a1r, a1i, b1r, b1i = e1
    a2r, a2i, b2r, b2i = e2
    return (a2r * a1r - a2i * a1i,
            a2r * a1i + a2i * a1r,
            a2r * b1r - a2i * b1i + b2r,
            a2r * b1i + a2i * b1r + b2i)


def _s5_scan(uf, lam_re, lam_im, log_step, b_re, b_im, reverse, h0_re=None, h0_im=None):
    T = uf.shape[1]
    lr = lam_re.astype(F32)
    li = lam_im.astype(F32)
    dt = jnp.exp(log_step.astype(F32))[:, None]
    ar, ai = lr * dt, li * dt
    mag = jnp.exp(ar)
    lb_re, lb_im = mag * jnp.cos(ai), mag * jnp.sin(ai)
    den = lr * lr + li * li
    f_re = ((lb_re - 1.0) * lr + lb_im * li) / den
    f_im = (lb_im * lr - (lb_re - 1.0) * li) / den
    br, bi = b_re.astype(F32), b_im.astype(F32)
    bb_re = f_re[..., None] * br - f_im[..., None] * bi
    bb_im = f_re[..., None] * bi + f_im[..., None] * br
    x_re = jnp.einsum('btgp,gnp->btgn', uf, bb_re)
    x_im = jnp.einsum('btgp,gnp->btgn', uf, bb_im)
    a_re = jnp.broadcast_to(lb_re, (1, T) + lb_re.shape)
    a_im = jnp.broadcast_to(lb_im, (1, T) + lb_im.shape)
    _, _, s_re, s_im = lax.associative_scan(_cplx_combine, (a_re, a_im, x_re, x_im),
                                            reverse=reverse, axis=1)
    if h0_re is not None:
        k = (T - jnp.arange(T)) if reverse else (jnp.arange(T) + 1)
        kf = k.astype(F32)[:, None, None]
        pm = jnp.exp(kf * ar)
        p_re, p_im = pm * jnp.cos(kf * ai), pm * jnp.sin(kf * ai)
        h_re = h0_re.astype(F32)[:, None]
        h_im = h0_im.astype(F32)[:, None]
        s_re = s_re + p_re * h_re - p_im * h_im
        s_im = s_im + p_re * h_im + p_im * h_re
    return s_re, s_im


def s5_branch(u, lp, h0_re=None, h0_im=None):
    B, T, _ = u.shape
    uf = u.astype(F32).reshape(B, T, SSM_G, SSM_P)
    y = lp['d_skip'].astype(F32).reshape(SSM_G, SSM_P) * uf
    fin_re, fin_im = [], []
    for d, rev in enumerate((False, True)):
        s_re, s_im = _s5_scan(uf, lp['lam_re'][d], lp['lam_im'][d], lp['log_step'][d],
                              lp['b_re'][d], lp['b_im'][d], rev,
                              None if h0_re is None else h0_re[:, d],
                              None if h0_im is None else h0_im[:, d])
        y = y + jnp.einsum('btgn,gpn->btgp', s_re, lp['c_re'][d].astype(F32)) \
              - jnp.einsum('btgn,gpn->btgp', s_im, lp['c_im'][d].astype(F32))
        if h0_re is None:
            idx = 0 if rev else T - 1
            fin_re.append(s_re[:, idx])
            fin_im.append(s_im[:, idx])
    y = jax.nn.gelu(y.reshape(B, T, D_A)).astype(u.dtype)
    z1, z2 = jnp.split(y @ lp['w_glu'], 2, axis=-1)
    out = z1 * jax.nn.sigmoid(z2)
    if h0_re is None:
        return out, jnp.stack(fin_re, axis=1), jnp.stack(fin_im, axis=1)
    return out


def chunk_sgu(uv, lp):
    B, T, _ = uv.shape
    u, v = jnp.split(jax.nn.gelu(uv), 2, axis=-1)
    vf = v.astype(F32)
    mu = jnp.mean(vf, axis=-1, keepdims=True)
    var = jnp.mean((vf - mu) ** 2, axis=-1, keepdims=True)
    vn = ((vf - mu) * lax.rsqrt(var + EPS) * lp['g_sgu'].astype(F32)).astype(v.dtype)
    vc = vn.reshape(B, T // CHUNK, CHUNK, SGU_HEADS, SGU_HD)
    mixed = jnp.einsum('hqs,bnshd->bnqhd', lp['w_spatial'], vc) + lp['b_spatial'].T[:, :, None]
    return (u * mixed.reshape(B, T, D_B)) @ lp['w_b_out']


def context_attention(q, k, v, sink):
    B, L = q.shape[0], q.shape[1]
    nq = L // BLOCK
    scale = HEAD_DIM ** -0.5
    qb = q.reshape(B, nq, BLOCK, N_KV, GQ, HEAD_DIM).transpose(1, 0, 2, 3, 4, 5)
    sk = sink.astype(F32).reshape(N_KV, GQ)[None, :, :, None, None]

    def one(qblk):
        s = jnp.einsum('bqkgd,bskd->bkgqs', qblk, k, preferred_element_type=F32) * scale
        m = jnp.maximum(jnp.max(s, axis=-1, keepdims=True), sk)
        p = jnp.exp(s - m)
        inv = 1.0 / (jnp.sum(p, axis=-1, keepdims=True) + jnp.exp(sk - m))
        return jnp.einsum('bkgqs,bskd->bqkgd', (p * inv).astype(v.dtype), v)

    out = lax.map(one, qb)
    return out.transpose(1, 0, 2, 3, 4, 5).reshape(B, L, D_C)


def window_attention(q, k, v, ck, cv, sink):
    B, T = q.shape[0], q.shape[1]
    nb = T // BLOCK
    scale = HEAD_DIM ** -0.5
    qb = q.reshape(B, nb, BLOCK, N_KV, GQ, HEAD_DIM)

    def band(x):
        xp = jnp.pad(x, ((0, 0), (BLOCK, BLOCK), (0, 0), (0, 0)))
        xp = xp.reshape(B, nb + 2, BLOCK, N_KV, HEAD_DIM)
        return jnp.concatenate([xp[:, :-2], xp[:, 1:-1], xp[:, 2:]], axis=2)

    kb, vb = band(k), band(v)
    qi = jnp.arange(BLOCK)[:, None]
    kj = jnp.arange(3 * BLOCK)[None, :]
    key_pos = jnp.arange(nb)[:, None, None] * BLOCK - BLOCK + kj[None]
    mask = (jnp.abs(kj - BLOCK - qi) <= WINDOW)[None] & (key_pos >= 0) & (key_pos < T)
    s_loc = jnp.einsum('bnqkgd,bnskd->bnkgqs', qb, kb, preferred_element_type=F32) * scale
    s_loc = jnp.where(mask[None, :, None, None], s_loc, -jnp.inf)
    s_ctx = jnp.einsum('bnqkgd,bskd->bnkgqs', qb, ck, preferred_element_type=F32) * scale
    sk = sink.astype(F32).reshape(N_KV, GQ)[None, None, :, :, None, None]
    m = jnp.maximum(jnp.maximum(jnp.max(s_loc, axis=-1, keepdims=True),
                                jnp.max(s_ctx, axis=-1, keepdims=True)), sk)
    p_loc = jnp.exp(s_loc - m)
    p_ctx = jnp.exp(s_ctx - m)
    inv = 1.0 / (jnp.sum(p_loc, axis=-1, keepdims=True) + jnp.sum(p_ctx, axis=-1, keepdims=True)
                 + jnp.exp(sk - m))
    o = jnp.einsum('bnkgqs,bnskd->bnqkgd', (p_loc * inv).astype(vb.dtype), vb) \
        + jnp.einsum('bnkgqs,bskd->bnqkgd', (p_ctx * inv).astype(cv.dtype), cv)
    return o.reshape(B, T, D_C)


def conv_ffn(h, lp):
    z = h @ lp['w_up']
    z = lax.conv_general_dilated(z, lp['conv_w'].astype(z.dtype)[:, None, :], window_strides=(1,),
                                 padding=((CONV_W // 2, CONV_W // 2),),
                                 dimension_numbers=('NWC', 'WIO', 'NWC'),
                                 feature_group_count=2 * D_FF) + lp['conv_b']
    g, val = jnp.split(z, 2, axis=-1)
    return (jax.nn.silu(g) * val) @ lp['w_down']


def token_mixer(h, lp, ctx):
    B, T, _ = h.shape
    a_in, uv, q, k, v, gates = jnp.split(h @ lp['w_in'], [OFF_B, OFF_Q, OFF_K, OFF_V, OFF_G], axis=-1)
    q = q.reshape(B, T, N_HEADS, HEAD_DIM)
    k = k.reshape(B, T, N_KV, HEAD_DIM)
    v = v.reshape(B, T, N_KV, HEAD_DIM)
    if ctx is None:
        y_a, fin_re, fin_im = s5_branch(a_in, lp)
        y_c = context_attention(q, k, v, lp['sink'])
        new = (k, v, fin_re, fin_im)
    else:
        ck, cv, h0_re, h0_im = ctx
        y_a = s5_branch(a_in, lp, h0_re, h0_im)
        y_c = window_attention(rope_2d(q), rope_2d(k), v, ck, cv, lp['sink'])
        new = None
    y_b = chunk_sgu(uv, lp)
    y_c = y_c @ lp['w_c_out']
    g_a, g_b, g_c = jnp.split(jax.nn.sigmoid(gates), N_BRANCH, axis=-1)
    merged = g_a * y_a + g_b * y_b + g_c * y_c
    return merged @ lp['w_o'], new


def layer(x, mod, lp, ctx):
    sh1, sc1, g1, sh2, sc2, g2 = jnp.split(mod, 6, axis=-1)
    h = rmsnorm(x, lp['g_pre_mix']) * (1.0 + sc1) + sh1
    m, new = token_mixer(h, lp, ctx)
    x = x + g1 * rmsnorm(m, lp['g_post_mix'])
    h = rmsnorm(x, lp['g_pre_ffn']) * (1.0 + sc2) + sh2
    x = x + g2 * rmsnorm(conv_ffn(h, lp), lp['g_post_ffn'])
    return x, new


def setup_inputs(seed: int = 0) -> dict:
    key = jax.random.key(seed)
    ks = iter(jax.random.split(key, 48))

    def nrm(shape, scale):
        return jax.random.normal(next(ks), shape, F32) * scale

    L = DEPTH
    lam_im_base = math.pi * jnp.arange(SSM_N, dtype=F32)
    return {
        'x_prompt': nrm((BATCH, SEQ, D_MODEL), 1.0),
        'x_sample': nrm((DEC_BATCH, DEC_SEQ, D_MODEL), 1.0),
        'cache_k': nrm((DEC_BATCH, DEPTH, PAST_LEN, N_KV, HEAD_DIM), 1.0),
        'cache_v': nrm((DEC_BATCH, DEPTH, PAST_LEN, N_KV, HEAD_DIM), 1.0),
        'state_ssm_re': nrm((DEC_BATCH, DEPTH, 2, SSM_G, SSM_N), 0.5),
        'state_ssm_im': nrm((DEC_BATCH, DEPTH, 2, SSM_G, SSM_N), 0.5),
        'c': nrm((DEC_BATCH, D_MODEL), 1.0),
        'c_ctx': nrm((D_MODEL,), 1.0),
        'w_mod': nrm((L, D_MODEL, 6 * D_MODEL), 0.5 * D_MODEL ** -0.5),
        'b_mod': nrm((L, 6 * D_MODEL), 0.1),
        'g_pre_mix': 1.0 + nrm((L, D_MODEL), 0.02),
        'g_post_mix': 1.0 + nrm((L, D_MODEL), 0.02),
        'g_pre_ffn': 1.0 + nrm((L, D_MODEL), 0.02),
        'g_post_ffn': 1.0 + nrm((L, D_MODEL), 0.02),
        'w_in': nrm((L, D_MODEL, D_IN), D_MODEL ** -0.5),
        'lam_re': -0.5 + nrm((L, 2, SSM_G, SSM_N), 0.01),
        'lam_im': lam_im_base + nrm((L, 2, SSM_G, SSM_N), 0.01),
        'log_step': jax.random.uniform(next(ks), (L, 2, SSM_G), F32, math.log(1e-3), math.log(1e-1)),
        'b_re': nrm((L, 2, SSM_G, SSM_N, SSM_P), (2 * SSM_P) ** -0.5),
        'b_im': nrm((L, 2, SSM_G, SSM_N, SSM_P), (2 * SSM_P) ** -0.5),
        'c_re': nrm((L, 2, SSM_G, SSM_P, SSM_N), (2 * SSM_N) ** -0.5),
        'c_im': nrm((L, 2, SSM_G, SSM_P, SSM_N), (2 * SSM_N) ** -0.5),
        'd_skip': nrm((L, D_A), 0.5),
        'w_glu': nrm((L, D_A, 2 * D_MODEL), D_A ** -0.5),
        'g_sgu': 1.0 + nrm((L, D_B), 0.02),
        'w_spatial': nrm((L, SGU_HEADS, CHUNK, CHUNK), CHUNK ** -0.5),
        'b_spatial': 1.0 + nrm((L, SGU_HEADS, CHUNK), 0.01),
        'w_b_out': nrm((L, D_B, D_MODEL), D_B ** -0.5),
        'sink': nrm((L, N_HEADS), 0.5),
        'w_c_out': nrm((L, D_C, D_MODEL), D_C ** -0.5),
        'w_o': nrm((L, D_MODEL, D_MODEL), D_MODEL ** -0.5),
        'w_up': nrm((L, D_MODEL, 2 * D_FF), D_MODEL ** -0.5),
        'conv_w': nrm((L, CONV_W, 2 * D_FF), CONV_W ** -0.5),
        'conv_b': nrm((L, 2 * D_FF), 0.01),
        'w_down': nrm((L, D_FF, D_MODEL), D_FF ** -0.5),
    }


def reference(x_prompt, x_sample, cache_k, cache_v, state_ssm_re, state_ssm_im, c, c_ctx,
              w_mod, b_mod, g_pre_mix, g_post_mix, g_pre_ffn, g_post_ffn, w_in,
              lam_re, lam_im, log_step, b_re, b_im, c_re, c_im, d_skip, w_glu,
              g_sgu, w_spatial, b_spatial, w_b_out, sink, w_c_out, w_o,
              w_up, conv_w, conv_b, w_down):
    xp, xs = x_prompt, x_sample
    new_k, new_v, new_re, new_im = [], [], [], []
    for l in range(DEPTH):
        lp = {
            'g_pre_mix': g_pre_mix[l], 'g_post_mix': g_post_mix[l],
            'g_pre_ffn': g_pre_ffn[l], 'g_post_ffn': g_post_ffn[l],
            'w_in': w_in[l], 'lam_re': lam_re[l], 'lam_im': lam_im[l], 'log_step': log_step[l],
            'b_re': b_re[l], 'b_im': b_im[l], 'c_re': c_re[l], 'c_im': c_im[l],
            'd_skip': d_skip[l], 'w_glu': w_glu[l], 'g_sgu': g_sgu[l],
            'w_spatial': w_spatial[l], 'b_spatial': b_spatial[l], 'w_b_out': w_b_out[l],
            'sink': sink[l], 'w_c_out': w_c_out[l], 'w_o': w_o[l],
            'w_up': w_up[l], 'conv_w': conv_w[l], 'conv_b': conv_b[l], 'w_down': w_down[l],
        }
        mod_p = (jax.nn.silu(c_ctx) @ w_mod[l] + b_mod[l])[None, None, :]
        xp, (k_l, v_l, fr_l, fi_l) = layer(xp, mod_p, lp, None)
        new_k.append(k_l)
        new_v.append(v_l)
        new_re.append(fr_l)
        new_im.append(fi_l)
        mod_s = (jax.nn.silu(c) @ w_mod[l] + b_mod[l])[:, None, :]
        xs, _ = layer(xs, mod_s, lp, (cache_k[:, l], cache_v[:, l],
                                      state_ssm_re[:, l], state_ssm_im[:, l]))
    new_cache_k = jnp.stack(new_k, axis=1)
    new_cache_v = jnp.stack(new_v, axis=1)
    new_state_ssm_re = jnp.stack(new_re, axis=1)
    new_state_ssm_im = jnp.stack(new_im, axis=1)
    return (xp, xs, new_cache_k, new_cache_v, new_state_ssm_re, new_state_ssm_im)
```

```python
import functools
import math

import jax
import jax.numpy as jnp
from jax import lax
from jax.experimental import pallas as pl
from jax.experimental.pallas import tpu as pltpu

F32 = jnp.float32
BF16 = jnp.bfloat16

GRID_W = 64
SSM_P = 16
SSM_N = 64
CHUNK = 128
N_HEADS = 8
N_KV = 2
HEAD_DIM = 64
BLOCK = 128
ROPE_BASE = 10000.0
EPS = 1e-6

LANES = 128
SUBLANES = 8
HALO = 16
VMEM_LIMIT = 56 * 1024 * 1024


def _params(*sem):
    return pltpu.CompilerParams(dimension_semantics=sem, vmem_limit_bytes=VMEM_LIMIT)


def _dot(a, b):
    return jnp.dot(a, b, preferred_element_type=F32)


def _rms(x, g):
    return x * lax.rsqrt(jnp.mean(x * x, axis=-1, keepdims=True) + EPS) * g


def _mod_kernel(c_ref, w_ref, b_ref, o_ref):
    a = jax.nn.silu(c_ref[...]).astype(BF16)
    o_ref[...] = _dot(a, w_ref[...].astype(BF16)) + b_ref[...]


def _modulation(cvecs, w_mod, b_mod):
    L, D, D6 = w_mod.shape
    NC = cvecs.shape[0]
    tn = D6 // 4
    return pl.pallas_call(
        _mod_kernel,
        grid=(L, D6 // tn),
        in_specs=[pl.BlockSpec((NC, D), lambda l, j: (0, 0)),
                  pl.BlockSpec((None, D, tn), lambda l, j: (l, 0, j)),
                  pl.BlockSpec((None, 1, tn), lambda l, j: (l, 0, j))],
        out_specs=pl.BlockSpec((None, NC, tn), lambda l, j: (l, 0, j)),
        out_shape=jax.ShapeDtypeStruct((L, NC, D6), F32),
        compiler_params=_params("parallel", "parallel"),
        name="modulation",
    )(cvecs, w_mod, b_mod.reshape(L, 1, D6))


def _rope(x, cos, sin):
    lane = lax.broadcasted_iota(jnp.int32, cos.shape, 1)
    first = (lane % 32) < 16
    outs = []
    for j in range(x.shape[1] // LANES):
        xb = x[:, j * LANES:(j + 1) * LANES]
        partner = jnp.where(first, pltpu.roll(xb, LANES - 16, 1), pltpu.roll(xb, 16, 1))
        outs.append(xb * cos + partner * sin)
    return outs[0] if len(outs) == 1 else jnp.concatenate(outs, axis=1)


def _inproj_kernel(*refs, rope, offs):
    if rope:
        x_ref, mod_ref, g_ref, w_ref, cos_ref, sin_ref, a_ref, uv_ref, q_ref, k_ref, v_ref, gt_ref = refs
    else:
        x_ref, mod_ref, g_ref, w_ref, a_ref, uv_ref, q_ref, k_ref, v_ref, gt_ref = refs
    ob, oq, ok, ov, og, dn = offs
    m = mod_ref[...]
    h = _rms(x_ref[...], g_ref[...]) * (1.0 + m[1:2]) + m[0:1]
    hb = h.astype(BF16)
    a_ref[...] = _dot(hb, w_ref[:, 0:ob])
    uv_ref[...] = _dot(hb, w_ref[:, ob:oq]).astype(BF16)
    q = _dot(hb, w_ref[:, oq:ok])
    k = _dot(hb, w_ref[:, ok:ov])
    if rope:
        q = _rope(q, cos_ref[...], sin_ref[...])
        k = _rope(k, cos_ref[...], sin_ref[...])
    q_ref[...] = q
    k_ref[...] = k
    v_ref[...] = _dot(hb, w_ref[:, ov:og])
    gt_ref[...] = _dot(hb, w_ref[:, og:dn]).astype(BF16)


def _in_proj(x2, mod, g_pre, w_in, B, T, dims, rope_tabs=None):
    R, D = x2.shape
    d_a, d_b2, d_c, kv_w, d_g = dims
    ob = d_a
    oq = ob + d_b2
    ok = oq + d_c
    ov = ok + kv_w
    og = ov + kv_w
    dn = og + d_g
    tm = min(256, T)
    nt = T // tm
    bm = mod.shape[0]
    mod_map = (lambda i: (i // nt, 0, 0)) if bm > 1 else (lambda i: (0, 0, 0))
    in_specs = [pl.BlockSpec((tm, D), lambda i: (i, 0)),
                pl.BlockSpec((None, 8, D), mod_map),
                pl.BlockSpec((1, D), lambda i: (0, 0)),
                pl.BlockSpec((D, dn), lambda i: (0, 0))]
    args = [x2, mod, g_pre, w_in]
    if rope_tabs is not None:
        in_specs += [pl.BlockSpec((tm, LANES), lambda i: (i % nt, 0))] * 2
        args += list(rope_tabs)
    row = lambda w: pl.BlockSpec((tm, w), lambda i: (i, 0))
    out_specs = [pl.BlockSpec((tm, d_a), lambda i: (i % nt, i // nt)),
                 row(d_b2), row(d_c), row(kv_w), row(kv_w), row(d_g)]
    out_shape = [jax.ShapeDtypeStruct((T, B * d_a), F32),
                 jax.ShapeDtypeStruct((R, d_b2), BF16),
                 jax.ShapeDtypeStruct((R, d_c), F32),
                 jax.ShapeDtypeStruct((R, kv_w), F32),
                 jax.ShapeDtypeStruct((R, kv_w), F32),
                 jax.ShapeDtypeStruct((R, d_g), BF16)]
    return pl.pallas_call(
        functools.partial(_inproj_kernel, rope=rope_tabs is not None, offs=(ob, oq, ok, ov, og, dn)),
        grid=(R // tm,), in_specs=in_specs, out_specs=out_specs, out_shape=out_shape,
        compiler_params=_params("parallel"), name="in_proj",
    )(*args)


def _s5_disc_kernel(lr_ref, li_ref, ls_ref, br_ref, bi_ref, lbr_ref, lbi_ref, bbr_ref, bbi_ref):
    lr = lr_ref[...]
    li = li_ref[...]
    dt = jnp.exp(ls_ref[...])
    ar = lr * dt
    ai = li * dt
    mag = jnp.exp(ar)
    lb_re = mag * jnp.cos(ai)
    lb_im = mag * jnp.sin(ai)
    den = lr * lr + li * li
    f_re = ((lb_re - 1.0) * lr + lb_im * li) / den
    f_im = (lb_im * lr - (lb_re - 1.0) * li) / den
    lbr_ref[...] = lb_re
    lbi_ref[...] = lb_im
    br = br_ref[...]
    bi = bi_ref[...]
    bbr_ref[...] = f_re[:, None, :] * br - f_im[:, None, :] * bi
    bbi_ref[...] = f_re[:, None, :] * bi + f_im[:, None, :] * br


def _s5_discretise(lam_re, lam_im, log_step, b_re, b_im):
    two, G, N = lam_re.shape
    P = b_re.shape[-1]
    rows = two * G
    tr = lambda b: jnp.swapaxes(b, -1, -2).reshape(rows, P, N)
    vec = jax.ShapeDtypeStruct((rows, N), F32)
    mat = jax.ShapeDtypeStruct((rows, P, N), F32)
    return pl.pallas_call(_s5_disc_kernel, out_shape=[vec, vec, mat, mat], name="s5_discretise")(
        lam_re.reshape(rows, N), lam_im.reshape(rows, N), log_step.reshape(rows, 1), tr(b_re), tr(b_im))


def _s5_tables(lb_re, lb_im, bb_re, bb_im, c_re, c_im, G, N, P):
    GB = LANES // P
    NB = G // GB
    eye = jnp.eye(GB, dtype=F32)
    def wb(bb):
        bbk = bb.reshape(NB, GB, P, N)
        return jnp.einsum('kgpn,gh->kgphn', bbk, eye).reshape(NB, GB * P, GB * N)
    w_b = jnp.concatenate([wb(bb_re), wb(bb_im)], axis=-1).astype(BF16)
    def wc(c):
        ck = c.reshape(NB, GB, P, N)
        return jnp.einsum('kgpn,gh->kgnhp', ck, eye).reshape(NB, GB * N, GB * P)
    w_c = jnp.concatenate([wc(c_re), -wc(c_im)], axis=1).astype(BF16)
    lam = jnp.concatenate([lb_re.reshape(NB, 1, GB * N), lb_im.reshape(NB, 1, GB * N)], axis=-1)
    return w_b, w_c, lam


def _s5_kernel(*refs, reverse, has_h0, fuse, nb, sw):
    it = iter(refs)
    u_ref, wb_ref, wc_ref, lam_ref = next(it), next(it), next(it), next(it)
    if has_h0:
        h0r_ref, h0i_ref = next(it), next(it)
    if fuse:
        yb_ref, dsk_ref, wglu_ref = next(it), next(it), next(it)
    y_ref, fr_ref, fi_ref = next(it), next(it), next(it)
    xs_ref, st_ref = next(it), next(it)
    if fuse:
        yacc_ref = next(it)

    tc, bs, _ = u_ref.shape
    rows = tc * bs
    c = pl.program_id(1)

    @pl.when(c == 0)
    def _():
        for j in range(nb):
            if has_h0:
                st_ref[j, :, 0:sw] = h0r_ref[:, j * sw:(j + 1) * sw]
                st_ref[j, :, sw:2 * sw] = h0i_ref[:, j * sw:(j + 1) * sw]
            else:
                st_ref[j] = jnp.zeros(st_ref.shape[1:], F32)

    for j in range(nb):
        ub = u_ref[:, :, j * LANES:(j + 1) * LANES].reshape(rows, LANES).astype(BF16)
        xs_ref[...] = _dot(ub, wb_ref[j]).reshape(tc, bs, 2 * sw)
        lr = jnp.broadcast_to(lam_ref[j, :, 0:sw], (bs, sw))
        li = jnp.broadcast_to(lam_ref[j, :, sw:2 * sw], (bs, sw))

        def step(i, carry, lr=lr, li=li):
            s_re, s_im = carry
            t = (tc - 1 - i) if reverse else i
            n_re = lr * s_re - li * s_im + xs_ref[t, :, 0:sw]
            n_im = lr * s_im + li * s_re + xs_ref[t, :, sw:2 * sw]
            xs_ref[t, :, 0:sw] = n_re
            xs_ref[t, :, sw:2 * sw] = n_im
            return n_re, n_im

        s_re, s_im = lax.fori_loop(0, tc, step, (st_ref[j, :, 0:sw], st_ref[j, :, sw:2 * sw]), unroll=8)
        st_ref[j, :, 0:sw] = s_re
        st_ref[j, :, sw:2 * sw] = s_im
        fr_ref[:, j * sw:(j + 1) * sw] = s_re
        fi_ref[:, j * sw:(j + 1) * sw] = s_im
        yj = _dot(xs_ref[...].reshape(rows, 2 * sw).astype(BF16), wc_ref[j])
        if fuse:
            yacc_ref[:, j * LANES:(j + 1) * LANES] = yj
        else:
            y_ref[:, :, j * LANES:(j + 1) * LANES] = yj.reshape(tc, bs, LANES)

    if fuse:
        da = nb * LANES
        y = yacc_ref[...] + yb_ref[...].reshape(rows, da) + dsk_ref[...] * u_ref[...].reshape(rows, da)
        z = _dot(jax.nn.gelu(y).astype(BF16), wglu_ref[...])
        dm = z.shape[1] // 2
        y_ref[...] = (z[:, :dm] * jax.nn.sigmoid(z[:, dm:])).reshape(tc, bs, dm)


def _s5_pass(u_t, tabs, B, T, reverse, h0=None, fuse=None):
    w_b, w_c, lam = tabs
    nb, _, sw2 = w_b.shape
    sw = sw2 // 2
    da = nb * LANES
    bs = SUBLANES
    tc = 128
    nct = T // tc
    u3 = u_t.reshape(T, B, da)
    tmap = (lambda g, c: (nct - 1 - c, g, 0)) if reverse else (lambda g, c: (c, g, 0))
    const3 = lambda g, c: (0, 0, 0)
    in_specs = [pl.BlockSpec((tc, bs, da), tmap),
                pl.BlockSpec(w_b.shape, const3), pl.BlockSpec(w_c.shape, const3), pl.BlockSpec(lam.shape, const3)]
    args = [u3, w_b, w_c, lam]
    if h0 is not None:
        in_specs += [pl.BlockSpec((bs, nb * sw), lambda g, c: (g, 0))] * 2
        args += list(h0)
    scratch = [pltpu.VMEM((tc, bs, sw2), F32), pltpu.VMEM((nb, bs, sw2), F32)]
    dout = da
    if fuse is not None:
        y_other, d_skip, w_glu = fuse
        dout = w_glu.shape[1] // 2
        in_specs += [pl.BlockSpec((tc, bs, da), tmap),
                     pl.BlockSpec((1, da), lambda g, c: (0, 0)),
                     pl.BlockSpec(w_glu.shape, lambda g, c: (0, 0))]
        args += [y_other.reshape(T, B, da), d_skip, w_glu]
        scratch.append(pltpu.VMEM((tc * bs, da), F32))
    fin = jax.ShapeDtypeStruct((B, nb * sw), F32)
    y, f_re, f_im = pl.pallas_call(
        functools.partial(_s5_kernel, reverse=reverse, has_h0=h0 is not None, fuse=fuse is not None, nb=nb, sw=sw),
        grid=(B // bs, nct), in_specs=in_specs,
        out_specs=[pl.BlockSpec((tc, bs, dout), tmap),
                   pl.BlockSpec((bs, nb * sw), lambda g, c: (g, 0)),
                   pl.BlockSpec((bs, nb * sw), lambda g, c: (g, 0))],
        out_shape=[jax.ShapeDtypeStruct((T, B, dout), F32), fin, fin],
        scratch_shapes=scratch,
        compiler_params=_params("parallel", "arbitrary"),
        name="s5_scan_glu" if fuse is not None else "s5_scan",
    )(*args)
    return y.reshape(T, B * dout), f_re, f_im


def _sgu_kernel(uv_ref, g_ref, ws_ref, bias_ref, wo_ref, o_ref, *, nch):
    db = g_ref.shape[1]
    x = jax.nn.gelu(uv_ref[...].astype(F32))
    u = x[:, :db]
    v = x[:, db:]
    mu = jnp.mean(v, axis=-1, keepdims=True)
    vc = v - mu
    var = jnp.mean(vc * vc, axis=-1, keepdims=True)
    vn = (vc * lax.rsqrt(var + EPS) * g_ref[...]).astype(BF16)
    lane = lax.broadcasted_iota(jnp.int32, (CHUNK, LANES), 1)
    lo = lane < (LANES // 2)
    bias = bias_ref[...]
    for c in range(nch):
        r0 = c * CHUNK
        parts = []
        for hp in range(db // LANES):
            vb = vn[r0:r0 + CHUNK, hp * LANES:(hp + 1) * LANES]
            parts.append(jnp.where(lo, _dot(ws_ref[2 * hp], vb), _dot(ws_ref[2 * hp + 1], vb)))
        mixed = jnp.concatenate(parts, axis=1) + bias
        prod = (u[r0:r0 + CHUNK] * mixed).astype(BF16)
        o_ref[r0:r0 + CHUNK, :] = _dot(prod, wo_ref[...])


def _sgu(uv, g_sgu, w_spatial, bias_full, w_b_out):
    R, db2 = uv.shape
    db = db2 // 2
    dm = w_b_out.shape[1]
    nch = 2
    tm = nch * CHUNK
    return pl.pallas_call(
        functools.partial(_sgu_kernel, nch=nch),
        grid=(R // tm,),
        in_specs=[pl.BlockSpec((tm, db2), lambda i: (i, 0)),
                  pl.BlockSpec((1, db), lambda i: (0, 0)),
                  pl.BlockSpec(w_spatial.shape, lambda i: (0, 0, 0)),
                  pl.BlockSpec(bias_full.shape, lambda i: (0, 0)),
                  pl.BlockSpec(w_b_out.shape, lambda i: (0, 0))],
        out_specs=pl.BlockSpec((tm, dm), lambda i: (i, 0)),
        out_shape=jax.ShapeDtypeStruct((R, dm), F32),
        compiler_params=_params("parallel"), name="chunk_sgu",
    )(uv, g_sgu, w_spatial, bias_full, w_b_out)


def _pair(x):
    lane = lax.broadcasted_iota(jnp.int32, x.shape, 1)
    lo = lane < HEAD_DIM
    sw = pltpu.roll(x, HEAD_DIM, 1)
    return jnp.where(lo, x, sw), jnp.where(lo, sw, x)


def _attend(q, kcat, vcat, sink_ref, mask, o_ref):
    M = q.shape[0]
    scale = HEAD_DIM ** -0.5
    lane = lax.broadcasted_iota(jnp.int32, (M, LANES), 1)
    lo = lane < HEAD_DIM
    kp = _pair(kcat)
    vp = _pair(vcat)
    heads_per_kv = N_HEADS // N_KV
    blocks_per_kv = heads_per_kv * HEAD_DIM // LANES
    for kk in range(N_KV):
        kb = kp[kk].astype(BF16)
        vb = vp[kk].astype(BF16)
        for jb in range(blocks_per_kv):
            blk = kk * blocks_per_kv + jb
            qb = q[:, blk * LANES:(blk + 1) * LANES]
            halves = []
            for half in range(2):
                head = 2 * blk + half
                qm = jnp.where(lo if half == 0 else ~lo, qb, 0.0).astype(BF16)
                s = lax.dot_general(qm, kb, (((1,), (1,)), ((), ())), preferred_element_type=F32) * scale
                if mask is not None:
                    s = jnp.where(mask, s, -jnp.inf)
                sk = sink_ref[head]
                m = jnp.maximum(jnp.max(s, axis=-1, keepdims=True), sk)
                p = jnp.exp(s - m)
                inv = 1.0 / (jnp.sum(p, axis=-1, keepdims=True) + jnp.exp(sk - m))
                halves.append(_dot((p * inv).astype(BF16), vb))
            o_ref[:, blk * LANES:(blk + 1) * LANES] = jnp.where(lo, halves[0], halves[1]).astype(o_ref.dtype)


def _ctx_attn_kernel(sink_ref, q_ref, k_ref, v_ref, o_ref):
    _attend(q_ref[...], k_ref[...], v_ref[...], sink_ref, None, o_ref)


def _context_attention(q, k, v, sink, B, T):
    R, dc = q.shape
    kvw = k.shape[1]
    return pl.pallas_call(
        _ctx_attn_kernel,
        grid=(B,),
        in_specs=[pl.BlockSpec(memory_space=pltpu.SMEM),
                  pl.BlockSpec((T, dc), lambda b: (b, 0)),
                  pl.BlockSpec((T, kvw), lambda b: (b, 0)),
                  pl.BlockSpec((T, kvw), lambda b: (b, 0))],
        out_specs=pl.BlockSpec((T, dc), lambda b: (b, 0)),
        out_shape=jax.ShapeDtypeStruct((R, dc), BF16),
        compiler_params=_params("parallel"), name="context_attention",
    )(sink, q, k, v)


def _win_attn_kernel(sink_ref, q_ref, kp_ref, kc_ref, kn_ref, vp_ref, vc_ref, vn_ref, ck_ref, cv_ref, o_ref, *, nblk):
    i = pl.program_id(1)
    past = ck_ref.shape[0]
    kcat = jnp.concatenate([kp_ref[...], kc_ref[...], kn_ref[...], ck_ref[...]], axis=0)
    vcat = jnp.concatenate([vp_ref[...], vc_ref[...], vn_ref[...], cv_ref[...]], axis=0)
    S = 3 * BLOCK + past
    qi = lax.broadcasted_iota(jnp.int32, (BLOCK, S), 0)
    kj = lax.broadcasted_iota(jnp.int32, (BLOCK, S), 1)
    big = 4 * BLOCK
    hi = jnp.where(kj < BLOCK, kj - jnp.where(i > 0, 0, BLOCK), big)
    lo = jnp.where((kj >= 2 * BLOCK) & (kj < 3 * BLOCK), kj - 2 * BLOCK + jnp.where(i < nblk - 1, 0, BLOCK), -big)
    mask = (qi <= hi) & (qi >= lo)
    _attend(q_ref[...], kcat, vcat, sink_ref, mask, o_ref)


def _window_attention(q, k, v, cache_k4, cache_v4, layer, sink, B, T):
    R, dc = q.shape
    kvw = k.shape[1]
    nblk = T // BLOCK
    past = cache_k4.shape[2]
    cur = lambda b, i: (b * nblk + i, 0)
    prv = lambda b, i: (b * nblk + jnp.maximum(i - 1, 0), 0)
    nxt = lambda b, i: (b * nblk + jnp.minimum(i + 1, nblk - 1), 0)
    kvs = lambda f: pl.BlockSpec((BLOCK, kvw), f)
    ctx = pl.BlockSpec((None, None, past, kvw), lambda b, i: (b, layer, 0, 0))
    return pl.pallas_call(
        functools.partial(_win_attn_kernel, nblk=nblk),
        grid=(B, nblk),
        in_specs=[pl.BlockSpec(memory_space=pltpu.SMEM),
                  pl.BlockSpec((BLOCK, dc), cur),
                  kvs(prv), kvs(cur), kvs(nxt), kvs(prv), kvs(cur), kvs(nxt), ctx, ctx],
        out_specs=pl.BlockSpec((BLOCK, dc), cur),
        out_shape=jax.ShapeDtypeStruct((R, dc), BF16),
        compiler_params=_params("parallel", "parallel"), name="window_attention",
    )(sink, q, k, k, k, v, v, v, cache_k4, cache_v4)


def _merge_kernel(x_ref, mod_ref, gt_ref, ya_ref, yb_ref, ao_ref, wc_ref, wo_ref, g_ref, o_ref):
    dm = x_ref.shape[1]
    yc = _dot(ao_ref[...], wc_ref[...])
    gts = jax.nn.sigmoid(gt_ref[...].astype(F32))
    merged = gts[:, 0:dm] * ya_ref[...] + gts[:, dm:2 * dm] * yb_ref[...] + gts[:, 2 * dm:3 * dm] * yc
    mo = _dot(merged.astype(BF16), wo_ref[...])
    m = mod_ref[...]
    o_ref[...] = x_ref[...] + m[2:3] * _rms(mo, g_ref[...])


def _merge(x2, mod, gates, ya_t, yb, ao, w_c_out, w_o, g_post, B, T):
    R, D = x2.shape
    tm = min(256, T)
    nt = T // tm
    bm = mod.shape[0]
    mod_map = (lambda i: (i // nt, 0, 0)) if bm > 1 else (lambda i: (0, 0, 0))
    row = lambda w: pl.BlockSpec((tm, w), lambda i: (i, 0))
    return pl.pallas_call(
        _merge_kernel,
        grid=(R // tm,),
        in_specs=[row(D), pl.BlockSpec((None, 8, D), mod_map), row(gates.shape[1]),
                  pl.BlockSpec((tm, D), lambda i: (i % nt, i // nt)),
                  row(D), row(ao.shape[1]),
                  pl.BlockSpec(w_c_out.shape, lambda i: (0, 0)),
                  pl.BlockSpec(w_o.shape, lambda i: (0, 0)),
                  pl.BlockSpec((1, D), lambda i: (0, 0))],
        out_specs=row(D),
        out_shape=jax.ShapeDtypeStruct((R, D), F32),
        compiler_params=_params("parallel"), name="merge_out_proj",
    )(x2, mod, gates, ya_t, yb, ao, w_c_out, w_o, g_post)


def _ffn_kernel(x_ref, xp_ref, xn_ref, mod_ref, gpre_ref, wg_ref, wv_ref, cwg_ref, cwv_ref, cbg_ref, cbv_ref,
                wd_ref, gpost_ref, o_ref, h_ref, acc_ref, *, seq):
    i = pl.program_id(0)
    j = pl.program_id(1)
    tm = x_ref.shape[0]
    m = mod_ref[...]

    @pl.when(j == 0)
    def _():
        norm = lambda x: (_rms(x, gpre_ref[...]) * (1.0 + m[4:5]) + m[3:4]).astype(BF16)
        h_ref[0:HALO, :] = norm(xp_ref[...])
        h_ref[HALO:HALO + tm, :] = norm(x_ref[...])
        h_ref[HALO + tm:, :] = norm(xn_ref[...])
        acc_ref[...] = jnp.zeros(acc_ref.shape, F32)

    tpos = (i * tm + lax.broadcasted_iota(jnp.int32, (tm, 1), 0)) % seq
    has_prev = tpos != 0
    has_next = tpos != seq - 1
    hx = h_ref[...]
    ext = tm + 2 * HALO

    def conv(w_ref, cw_ref, cb_ref):
        z = _dot(hx, w_ref[...])
        zp = pltpu.roll(z, 1, 0)[HALO:HALO + tm]
        zn = pltpu.roll(z, ext - 1, 0)[HALO:HALO + tm]
        zc = z[HALO:HALO + tm]
        cw = cw_ref[...]
        return (jnp.where(has_prev, zp, 0.0) * cw[0:1] + zc * cw[1:2]
                + jnp.where(has_next, zn, 0.0) * cw[2:3] + cb_ref[...])

    g = conv(wg_ref, cwg_ref, cbg_ref)
    val = conv(wv_ref, cwv_ref, cbv_ref)
    acc_ref[...] += _dot((jax.nn.silu(g) * val).astype(BF16), wd_ref[...])

    @pl.when(j == pl.num_programs(1) - 1)
    def _():
        o_ref[...] = x_ref[...] + m[5:6] * _rms(acc_ref[...], gpost_ref[...])


def _conv_ffn(x2, mod, g_pre, w_up, conv_w, conv_b, w_down, g_post, B, T):
    R, D = x2.shape
    dff = w_down.shape[0]
    tf = 256
    nf = dff // tf
    tm = 512 if (T >= 512 or mod.shape[0] == 1) else T
    tm = min(tm, R)
    nt = max(T // tm, 1)
    bm = mod.shape[0]
    mod_map = (lambda i, j: (i // nt, 0, 0)) if bm > 1 else (lambda i, j: (0, 0, 0))
    hb = tm // HALO
    nhb = R // HALO
    const = lambda i, j: (0, 0)
    return pl.pallas_call(
        functools.partial(_ffn_kernel, seq=T),
        grid=(R // tm, nf),
        in_specs=[pl.BlockSpec((tm, D), lambda i, j: (i, 0)),
                  pl.BlockSpec((HALO, D), lambda i, j: (jnp.maximum(i * hb - 1, 0), 0)),
                  pl.BlockSpec((HALO, D), lambda i, j: (jnp.minimum((i + 1) * hb, nhb - 1), 0)),
                  pl.BlockSpec((None, 8, D), mod_map),
                  pl.BlockSpec((1, D), const),
                  pl.BlockSpec((D, tf), lambda i, j: (0, j)),
                  pl.BlockSpec((D, tf), lambda i, j: (0, nf + j)),
                  pl.BlockSpec((3, tf), lambda i, j: (0, j)),
                  pl.BlockSpec((3, tf), lambda i, j: (0, nf + j)),
                  pl.BlockSpec((1, tf), lambda i, j: (0, j)),
                  pl.BlockSpec((1, tf), lambda i, j: (0, nf + j)),
                  pl.BlockSpec((tf, D), lambda i, j: (j, 0)),
                  pl.BlockSpec((1, D), const)],
        out_specs=pl.BlockSpec((tm, D), lambda i, j: (i, 0)),
        out_shape=jax.ShapeDtypeStruct((R, D), F32),
        scratch_shapes=[pltpu.VMEM((tm + 2 * HALO, D), BF16), pltpu.VMEM((tm, D), F32)],
        compiler_params=_params("parallel", "arbitrary"), name="conv_ffn",
    )(x2, x2, x2, mod, g_pre, w_up, w_up, conv_w, conv_w, conv_b, conv_b, w_down, g_post)


def _rope_tables(T):
    t = jnp.arange(T)
    row = (t // GRID_W).astype(F32)[:, None]
    col = (t % GRID_W).astype(F32)[:, None]
    half = HEAD_DIM // 2
    inv = ROPE_BASE ** (-jnp.arange(0, half, 2, dtype=F32) / half)
    lane = jnp.arange(LANES)
    freq = inv[lane % (half // 2)][None, :]
    ang = jnp.where(((lane % HEAD_DIM) < half)[None, :], row * freq, col * freq)
    sign = jnp.where((lane % half) < half // 2, -1.0, 1.0)[None, :]
    return jnp.cos(ang), jnp.sin(ang) * sign


def _layer(x2, mod, lp, B, T, ctx):
    D = x2.shape[1]
    rope_tabs = None if ctx is None else lp['rope']
    a_t, uv, q, k, v, gates = _in_proj(x2, mod, lp['g_pre_mix'], lp['w_in'], B, T, lp['dims'], rope_tabs)
    h0 = (None, None) if ctx is None else ctx[3]
    y_bwd, br_re, br_im = _s5_pass(a_t, lp['s5_tabs'][1], B, T, True, h0=h0[1])
    ya_t, fw_re, fw_im = _s5_pass(a_t, lp['s5_tabs'][0], B, T, False, h0=h0[0],
                                  fuse=(y_bwd, lp['d_skip'], lp['w_glu']))
    yb = _sgu(uv, lp['g_sgu'], lp['w_spatial'], lp['sgu_bias'], lp['w_b_out'])
    if ctx is None:
        ao = _context_attention(q, k, v, lp['sink'], B, T)
    else:
        ao = _window_attention(q, k, v, ctx[0], ctx[1], ctx[2], lp['sink'], B, T)
    x2 = _merge(x2, mod, gates, ya_t, yb, ao, lp['w_c_out'], lp['w_o'], lp['g_post_mix'], B, T)
    x2 = _conv_ffn(x2, mod, lp['g_pre_ffn'], lp['w_up'], lp['conv_w'], lp['conv_b'], lp['w_down'],
                   lp['g_post_ffn'], B, T)
    return x2, (k, v, (fw_re, br_re), (fw_im, br_im))


def kernel(x_prompt, x_sample, cache_k, cache_v, state_ssm_re, state_ssm_im, c, c_ctx,
           w_mod, b_mod, g_pre_mix, g_post_mix, g_pre_ffn, g_post_ffn, w_in,
           lam_re, lam_im, log_step, b_re, b_im, c_re, c_im, d_skip, w_glu,
           g_sgu, w_spatial, b_spatial, w_b_out, sink, w_c_out, w_o,
           w_up, conv_w, conv_b, w_down):
    BP, TP, D = x_prompt.shape
    BS, TS, _ = x_sample.shape
    L = w_in.shape[0]
    G, N = lam_re.shape[2], lam_re.shape[3]
    P = b_re.shape[-1]
    d_a = G * P
    d_b2 = 2 * g_sgu.shape[1]
    d_c = w_c_out.shape[1]
    kv_w = N_KV * HEAD_DIM
    d_g = w_in.shape[2] - (d_a + d_b2 + d_c + 2 * kv_w)
    past = cache_k.shape[2]

    nc = 16
    cvecs = jnp.zeros((nc, D), F32).at[0].set(c_ctx).at[1:1 + BS].set(c)
    mod_all = _modulation(cvecs, w_mod, b_mod).reshape(L, nc, 6, D)
    mod_all = jnp.pad(mod_all, ((0, 0), (0, 0), (0, 2), (0, 0)))

    rope = _rope_tables(TS)
    cache_k4 = cache_k.reshape(BS, L, past, kv_w)
    cache_v4 = cache_v.reshape(BS, L, past, kv_w)

    xp = x_prompt.reshape(BP * TP, D)
    xs = x_sample.reshape(BS * TS, D)
    new_k, new_v, new_re, new_im = [], [], [], []
    for l in range(L):
        lb_re, lb_im, bb_re, bb_im = _s5_discretise(lam_re[l], lam_im[l], log_step[l], b_re[l], b_im[l])
        tabs = []
        for d in range(2):
            sl = slice(d * G, (d + 1) * G)
            tabs.append(_s5_tables(lb_re[sl], lb_im[sl], bb_re[sl], bb_im[sl], c_re[l, d], c_im[l, d], G, N, P))
        lp = {
            'dims': (d_a, d_b2, d_c, kv_w, d_g),
            'g_pre_mix': g_pre_mix[l][None], 'g_post_mix': g_post_mix[l][None],
            'g_pre_ffn': g_pre_ffn[l][None], 'g_post_ffn': g_post_ffn[l][None],
            'w_in': w_in[l].astype(BF16), 's5_tabs': tabs, 'd_skip': d_skip[l][None],
            'w_glu': w_glu[l].astype(BF16), 'g_sgu': g_sgu[l][None],
            'w_spatial': w_spatial[l].astype(BF16),
            'sgu_bias': jnp.repeat(b_spatial[l].T, g_sgu.shape[1] // b_spatial.shape[1], axis=1),
            'w_b_out': w_b_out[l].astype(BF16), 'sink': sink[l],
            'w_c_out': w_c_out[l].astype(BF16), 'w_o': w_o[l].astype(BF16),
            'w_up': w_up[l].astype(BF16), 'conv_w': conv_w[l], 'conv_b': conv_b[l][None],
            'w_down': w_down[l].astype(BF16), 'rope': rope,
        }
        xp, (k_l, v_l, f_re, f_im) = _layer(xp, mod_all[l, 0:1], lp, BP, TP, None)
        new_k.append(k_l.reshape(BP, TP, N_KV, HEAD_DIM))
        new_v.append(v_l.reshape(BP, TP, N_KV, HEAD_DIM))
        new_re.append(jnp.stack([f.reshape(BP, G, N) for f in f_re], axis=1))
        new_im.append(jnp.stack([f.reshape(BP, G, N) for f in f_im], axis=1))
        h0 = [(state_ssm_re[:, l, d].reshape(BS, G * N), state_ssm_im[:, l, d].reshape(BS, G * N)) for d in range(2)]
        xs, _ = _layer(xs, mod_all[l, 1:1 + BS], lp, BS, TS, (cache_k4, cache_v4, l, h0))
    return (xp.reshape(BP, TP, D), xs.reshape(BS, TS, D),
            jnp.stack(new_k, axis=1), jnp.stack(new_v, axis=1),
            jnp.stack(new_re, axis=1), jnp.stack(new_im, axis=1))
```

```python
import functools
import math

import jax
import jax.numpy as jnp
from jax import lax
from jax.experimental import pallas as pl
from jax.experimental.pallas import tpu as pltpu

F32 = jnp.float32
BF16 = jnp.bfloat16

GRID_W = 64
SSM_P = 16
SSM_N = 64
CHUNK = 128
N_HEADS = 8
N_KV = 2
HEAD_DIM = 64
BLOCK = 128
ROPE_BASE = 10000.0
EPS = 1e-6

LANES = 128
SUBLANES = 8
HALO = 16
VMEM_LIMIT = 56 * 1024 * 1024


def _params(*sem):
    return pltpu.CompilerParams(dimension_semantics=sem, vmem_limit_bytes=VMEM_LIMIT)


def _dot(a, b):
    return jnp.dot(a, b, preferred_element_type=F32)


def _rms(x, g):
    return x * lax.rsqrt(jnp.mean(x * x, axis=-1, keepdims=True) + EPS) * g


def _mod_kernel(c_ref, w_ref, b_ref, o_ref):
    a = jax.nn.silu(c_ref[...]).astype(BF16)
    o_ref[...] = _dot(a, w_ref[...].astype(BF16)) + b_ref[...]


def _modulation(cvecs, w_mod, b_mod):
    L, D, D6 = w_mod.shape
    NC = cvecs.shape[0]
    tn = D6 // 4
    return pl.pallas_call(
        _mod_kernel,
        grid=(L, D6 // tn),
        in_specs=[pl.BlockSpec((NC, D), lambda l, j: (0, 0)),
                  pl.BlockSpec((None, D, tn), lambda l, j: (l, 0, j)),
                  pl.BlockSpec((None, 1, tn), lambda l, j: (l, 0, j))],
        out_specs=pl.BlockSpec((None, NC, tn), lambda l, j: (l, 0, j)),
        out_shape=jax.ShapeDtypeStruct((L, NC, D6), F32),
        compiler_params=_params("parallel", "parallel"),
        name="modulation",
    )(cvecs, w_mod, b_mod.reshape(L, 1, D6))


def _rope(x, cos, sin):
    lane = lax.broadcasted_iota(jnp.int32, cos.shape, 1)
    first = (lane % 32) < 16
    outs = []
    for j in range(x.shape[1] // LANES):
        xb = x[:, j * LANES:(j + 1) * LANES]
        partner = jnp.where(first, pltpu.roll(xb, LANES - 16, 1), pltpu.roll(xb, 16, 1))
        outs.append(xb * cos + partner * sin)
    return outs[0] if len(outs) == 1 else jnp.concatenate(outs, axis=1)


def _inproj_kernel(*refs, rope, offs):
    if rope:
        x_ref, mod_ref, g_ref, w_ref, cos_ref, sin_ref, a_ref, uv_ref, q_ref, k_ref, v_ref, gt_ref = refs
    else:
        x_ref, mod_ref, g_ref, w_ref, a_ref, uv_ref, q_ref, k_ref, v_ref, gt_ref = refs
    ob, oq, ok, ov, og, dn = offs
    m = mod_ref[...]
    h = _rms(x_ref[...], g_ref[...]) * (1.0 + m[1:2]) + m[0:1]
    hb = h.astype(BF16)
    a_ref[...] = _dot(hb, w_ref[:, 0:ob])
    uv_ref[...] = _dot(hb, w_ref[:, ob:oq]).astype(BF16)
    q = _dot(hb, w_ref[:, oq:ok])
    k = _dot(hb, w_ref[:, ok:ov])
    if rope:
        q = _rope(q, cos_ref[...], sin_ref[...])
        k = _rope(k, cos_ref[...], sin_ref[...])
    q_ref[...] = q
    k_ref[...] = k
    v_ref[...] = _dot(hb, w_ref[:, ov:og])
    gt_ref[...] = _dot(hb, w_ref[:, og:dn]).astype(BF16)


def _in_proj(x2, mod, g_pre, w_in, B, T, dims, rope_tabs=None):
    R, D = x2.shape
    d_a, d_b2, d_c, kv_w, d_g = dims
    ob = d_a
    oq = ob + d_b2
    ok = oq + d_c
    ov = ok + kv_w
    og = ov + kv_w
    dn = og + d_g
    tm = min(256, T)
    nt = T // tm
    bm = mod.shape[0]
    mod_map = (lambda i: (i // nt, 0, 0)) if bm > 1 else (lambda i: (0, 0, 0))
    in_specs = [pl.BlockSpec((tm, D), lambda i: (i, 0)),
                pl.BlockSpec((None, 8, D), mod_map),
                pl.BlockSpec((1, D), lambda i: (0, 0)),
                pl.BlockSpec((D, dn), lambda i: (0, 0))]
    args = [x2, mod, g_pre, w_in]
    if rope_tabs is not None:
        in_specs += [pl.BlockSpec((tm, LANES), lambda i: (i % nt, 0))] * 2
        args += list(rope_tabs)
    row = lambda w: pl.BlockSpec((tm, w), lambda i: (i, 0))
    out_specs = [pl.BlockSpec((tm, d_a), lambda i: (i % nt, i // nt)),
                 row(d_b2), row(d_c), row(kv_w), row(kv_w), row(d_g)]
    out_shape = [jax.ShapeDtypeStruct((T, B * d_a), F32),
                 jax.ShapeDtypeStruct((R, d_b2), BF16),
                 jax.ShapeDtypeStruct((R, d_c), F32),
                 jax.ShapeDtypeStruct((R, kv_w), F32),
                 jax.ShapeDtypeStruct((R, kv_w), F32),
                 jax.ShapeDtypeStruct((R, d_g), BF16)]
    return pl.pallas_call(
        functools.partial(_inproj_kernel, rope=rope_tabs is not None, offs=(ob, oq, ok, ov, og, dn)),
        grid=(R // tm,), in_specs=in_specs, out_specs=out_specs, out_shape=out_shape,
        compiler_params=_params("parallel"), name="in_proj",
    )(*args)


def _s5_disc_kernel(lr_ref, li_ref, ls_ref, br_ref, bi_ref, lbr_ref, lbi_ref, bbr_ref, bbi_ref):
    lr = lr_ref[...]
    li = li_ref[...]
    dt = jnp.exp(ls_ref[...])
    ar = lr * dt
    ai = li * dt
    mag = jnp.exp(ar)
    lb_re = mag * jnp.cos(ai)
    lb_im = mag * jnp.sin(ai)
    den = lr * lr + li * li
    f_re = ((lb_re - 1.0) * lr + lb_im * li) / den
    f_im = (lb_im * lr - (lb_re - 1.0) * li) / den
    lbr_ref[...] = lb_re
    lbi_ref[...] = lb_im
    br = br_ref[...]
    bi = bi_ref[...]
    bbr_ref[...] = f_re[:, None, :] * br - f_im[:, None, :] * bi
    bbi_ref[...] = f_re[:, None, :] * bi + f_im[:, None, :] * br


def _s5_discretise(lam_re, lam_im, log_step, b_re, b_im):
    two, G, N = lam_re.shape
    P = b_re.shape[-1]
    rows = two * G
    tr = lambda b: jnp.swapaxes(b, -1, -2).reshape(rows, P, N)
    vec = jax.ShapeDtypeStruct((rows, N), F32)
    mat = jax.ShapeDtypeStruct((rows, P, N), F32)
    return pl.pallas_call(_s5_disc_kernel, out_shape=[vec, vec, mat, mat], name="s5_discretise")(
        lam_re.reshape(rows, N), lam_im.reshape(rows, N), log_step.reshape(rows, 1), tr(b_re), tr(b_im))


def _s5_tables(lb_re, lb_im, bb_re, bb_im, c_re, c_im, G, N, P):
    GB = LANES // P
    NB = G // GB
    eye = jnp.eye(GB, dtype=F32)
    def wb(bb):
        bbk = bb.reshape(NB, GB, P, N)
        return jnp.einsum('kgpn,gh->kgphn', bbk, eye).reshape(NB, GB * P, GB * N)
    w_b = jnp.concatenate([wb(bb_re), wb(bb_im)], axis=-1).astype(BF16)
    def wc(c):
        ck = c.reshape(NB, GB, P, N)
        return jnp.einsum('kgpn,gh->kgnhp', ck, eye).reshape(NB, GB * N, GB * P)
    w_c = jnp.concatenate([wc(c_re), -wc(c_im)], axis=1).astype(BF16)
    lam = jnp.concatenate([lb_re.reshape(NB, 1, GB * N), lb_im.reshape(NB, 1, GB * N)], axis=-1)
    return w_b, w_c, lam


def _s5_kernel(*refs, reverse, has_h0, fuse, nb, sw):
    it = iter(refs)
    u_ref, wb_ref, wc_ref, lam_ref = next(it), next(it), next(it), next(it)
    if has_h0:
        h0r_ref, h0i_ref = next(it), next(it)
    if fuse:
        yb_ref, dsk_ref, wglu_ref = next(it), next(it), next(it)
    y_ref, fr_ref, fi_ref = next(it), next(it), next(it)
    xs_ref, st_ref = next(it), next(it)
    if fuse:
        yacc_ref = next(it)

    tc, bs, _ = u_ref.shape
    rows = tc * bs
    c = pl.program_id(1)

    @pl.when(c == 0)
    def _():
        for j in range(nb):
            if has_h0:
                st_ref[j, :, 0:sw] = h0r_ref[:, j * sw:(j + 1) * sw]
                st_ref[j, :, sw:2 * sw] = h0i_ref[:, j * sw:(j + 1) * sw]
            else:
                st_ref[j] = jnp.zeros(st_ref.shape[1:], F32)

    for j in range(nb):
        ub = u_ref[:, :, j * LANES:(j + 1) * LANES].reshape(rows, LANES).astype(BF16)
        xs_ref[...] = _dot(ub, wb_ref[j]).reshape(tc, bs, 2 * sw)
        lr = jnp.broadcast_to(lam_ref[j, :, 0:sw], (bs, sw))
        li = jnp.broadcast_to(lam_ref[j, :, sw:2 * sw], (bs, sw))

        def step(i, carry, lr=lr, li=li):
            s_re, s_im = carry
            t = (tc - 1 - i) if reverse else i
            n_re = lr * s_re - li * s_im + xs_ref[t, :, 0:sw]
            n_im = lr * s_im + li * s_re + xs_ref[t, :, sw:2 * sw]
            xs_ref[t, :, 0:sw] = n_re
            xs_ref[t, :, sw:2 * sw] = n_im
            return n_re, n_im

        s_re, s_im = lax.fori_loop(0, tc, step, (st_ref[j, :, 0:sw], st_ref[j, :, sw:2 * sw]), unroll=8)
        st_ref[j, :, 0:sw] = s_re
        st_ref[j, :, sw:2 * sw] = s_im
        fr_ref[:, j * sw:(j + 1) * sw] = s_re
        fi_ref[:, j * sw:(j + 1) * sw] = s_im
        yj = _dot(xs_ref[...].reshape(rows, 2 * sw).astype(BF16), wc_ref[j])
        if fuse:
            yacc_ref[:, j * LANES:(j + 1) * LANES] = yj
        else:
            y_ref[:, :, j * LANES:(j + 1) * LANES] = yj.reshape(tc, bs, LANES)

    if fuse:
        da = nb * LANES
        y = yacc_ref[...] + yb_ref[...].reshape(rows, da) + dsk_ref[...] * u_ref[...].reshape(rows, da)
        z = _dot(jax.nn.gelu(y).astype(BF16), wglu_ref[...])
        dm = z.shape[1] // 2
        y_ref[...] = (z[:, :dm] * jax.nn.sigmoid(z[:, dm:])).reshape(tc, bs, dm)


def _s5_pass(u_t, tabs, B, T, reverse, h0=None, fuse=None):
    w_b, w_c, lam = tabs
    nb, _, sw2 = w_b.shape
    sw = sw2 // 2
    da = nb * LANES
    bs = SUBLANES
    tc = 128
    nct = T // tc
    u3 = u_t.reshape(T, B, da)
    tmap = (lambda g, c: (nct - 1 - c, g, 0)) if reverse else (lambda g, c: (c, g, 0))
    const3 = lambda g, c: (0, 0, 0)
    in_specs = [pl.BlockSpec((tc, bs, da), tmap),
                pl.BlockSpec(w_b.shape, const3), pl.BlockSpec(w_c.shape, const3), pl.BlockSpec(lam.shape, const3)]
    args = [u3, w_b, w_c, lam]
    if h0 is not None:
        in_specs += [pl.BlockSpec((bs, nb * sw), lambda g, c: (g, 0))] * 2
        args += list(h0)
    scratch = [pltpu.VMEM((tc, bs, sw2), F32), pltpu.VMEM((nb, bs, sw2), F32)]
    dout = da
    if fuse is not None:
        y_other, d_skip, w_glu = fuse
        dout = w_glu.shape[1] // 2
        in_specs += [pl.BlockSpec((tc, bs, da), tmap),
                     pl.BlockSpec((1, da), lambda g, c: (0, 0)),
                     pl.BlockSpec(w_glu.shape, lambda g, c: (0, 0))]
        args += [y_other.reshape(T, B, da), d_skip, w_glu]
        scratch.append(pltpu.VMEM((tc * bs, da), F32))
    fin = jax.ShapeDtypeStruct((B, nb * sw), F32)
    y, f_re, f_im = pl.pallas_call(
        functools.partial(_s5_kernel, reverse=reverse, has_h0=h0 is not None, fuse=fuse is not None, nb=nb, sw=sw),
        grid=(B // bs, nct), in_specs=in_specs,
        out_specs=[pl.BlockSpec((tc, bs, dout), tmap),
                   pl.BlockSpec((bs, nb * sw), lambda g, c: (g, 0)),
                   pl.BlockSpec((bs, nb * sw), lambda g, c: (g, 0))],
        out_shape=[jax.ShapeDtypeStruct((T, B, dout), F32), fin, fin],
        scratch_shapes=scratch,
        compiler_params=_params("parallel", "arbitrary"),
        name="s5_scan_glu" if fuse is not None else "s5_scan",
    )(*args)
    return y.reshape(T, B * dout), f_re, f_im


def _sgu_kernel(uv_ref, g_ref, ws_ref, bias_ref, wo_ref, o_ref, *, nch):
    db = g_ref.shape[1]
    x = jax.nn.gelu(uv_ref[...].astype(F32))
    u = x[:, :db]
    v = x[:, db:]
    mu = jnp.mean(v, axis=-1, keepdims=True)
    vc = v - mu
    var = jnp.mean(vc * vc, axis=-1, keepdims=True)
    vn = (vc * lax.rsqrt(var + EPS) * g_ref[...]).astype(BF16)
    lane = lax.broadcasted_iota(jnp.int32, (CHUNK, LANES), 1)
    lo = lane < (LANES // 2)
    bias = bias_ref[...]
    for c in range(nch):
        r0 = c * CHUNK
        parts = []
        for hp in range(db // LANES):
            vb = vn[r0:r0 + CHUNK, hp * LANES:(hp + 1) * LANES]
            parts.append(jnp.where(lo, _dot(ws_ref[2 * hp], vb), _dot(ws_ref[2 * hp + 1], vb)))
        mixed = jnp.concatenate(parts, axis=1) + bias
        prod = (u[r0:r0 + CHUNK] * mixed).astype(BF16)
        o_ref[r0:r0 + CHUNK, :] = _dot(prod, wo_ref[...])


def _sgu(uv, g_sgu, w_spatial, bias_full, w_b_out):
    R, db2 = uv.shape
    db = db2 // 2
    dm = w_b_out.shape[1]
    nch = 2
    tm = nch * CHUNK
    return pl.pallas_call(
        functools.partial(_sgu_kernel, nch=nch),
        grid=(R // tm,),
        in_specs=[pl.BlockSpec((tm, db2), lambda i: (i, 0)),
                  pl.BlockSpec((1, db), lambda i: (0, 0)),
                  pl.BlockSpec(w_spatial.shape, lambda i: (0, 0, 0)),
                  pl.BlockSpec(bias_full.shape, lambda i: (0, 0)),
                  pl.BlockSpec(w_b_out.shape, lambda i: (0, 0))],
        out_specs=pl.BlockSpec((tm, dm), lambda i: (i, 0)),
        out_shape=jax.ShapeDtypeStruct((R, dm), F32),
        compiler_params=_params("parallel"), name="chunk_sgu",
    )(uv, g_sgu, w_spatial, bias_full, w_b_out)


def _pair(x):
    lane = lax.broadcasted_iota(jnp.int32, x.shape, 1)
    lo = lane < HEAD_DIM
    sw = pltpu.roll(x, HEAD_DIM, 1)
    return jnp.where(lo, x, sw), jnp.where(lo, sw, x)


def _attend(q, kcat, vcat, sink_ref, bias, o_ref):
    M = q.shape[0]
    scale = HEAD_DIM ** -0.5
    lane = lax.broadcasted_iota(jnp.int32, (M, LANES), 1)
    lo = lane < HEAD_DIM
    kp = _pair(kcat)
    vp = _pair(vcat)
    heads_per_kv = N_HEADS // N_KV
    blocks_per_kv = heads_per_kv * HEAD_DIM // LANES
    for kk in range(N_KV):
        kb = kp[kk].astype(BF16)
        vb = vp[kk].astype(BF16)
        rows = []
        for jb in range(blocks_per_kv):
            blk = kk * blocks_per_kv + jb
            qb = q[:, blk * LANES:(blk + 1) * LANES] * scale
            rows += [jnp.where(lo, qb, 0.0), jnp.where(lo, 0.0, qb)]
        qst = jnp.concatenate(rows, axis=0).astype(BF16)
        s = lax.dot_general(qst, kb, (((1,), (1,)), ((), ())), preferred_element_type=F32)
        if bias is not None:
            cols, at = [], 0
            for off, blk_bias in bias:
                if off > at:
                    cols.append(s[:, at:off])
                cols.append(s[:, off:off + LANES] + jnp.concatenate([blk_bias] * heads_per_kv, axis=0))
                at = off + LANES
            cols.append(s[:, at:])
            s = jnp.concatenate(cols, axis=1)
        sk = jnp.concatenate([jnp.full((M, 1), sink_ref[kk * heads_per_kv + h], F32)
                              for h in range(heads_per_kv)], axis=0)
        m = jnp.maximum(jnp.max(s, axis=-1, keepdims=True), sk)
        p = jnp.exp(s - m)
        inv = 1.0 / (jnp.sum(p, axis=-1, keepdims=True) + jnp.exp(sk - m))
        o = _dot(p.astype(BF16), vb) * inv
        for jb in range(blocks_per_kv):
            blk = kk * blocks_per_kv + jb
            r0 = 2 * jb * M
            o_ref[:, blk * LANES:(blk + 1) * LANES] = jnp.where(
                lo, o[r0:r0 + M], o[r0 + M:r0 + 2 * M]).astype(o_ref.dtype)


def _ctx_attn_kernel(sink_ref, q_ref, k_ref, v_ref, o_ref):
    _attend(q_ref[...], k_ref[...], v_ref[...], sink_ref, None, o_ref)


def _context_attention(q, k, v, sink, B, T):
    R, dc = q.shape
    kvw = k.shape[1]
    return pl.pallas_call(
        _ctx_attn_kernel,
        grid=(B,),
        in_specs=[pl.BlockSpec(memory_space=pltpu.SMEM),
                  pl.BlockSpec((T, dc), lambda b: (b, 0)),
                  pl.BlockSpec((T, kvw), lambda b: (b, 0)),
                  pl.BlockSpec((T, kvw), lambda b: (b, 0))],
        out_specs=pl.BlockSpec((T, dc), lambda b: (b, 0)),
        out_shape=jax.ShapeDtypeStruct((R, dc), BF16),
        compiler_params=_params("parallel"), name="context_attention",
    )(sink, q, k, v)


def _win_attn_kernel(sink_ref, q_ref, kp_ref, kc_ref, kn_ref, vp_ref, vc_ref, vn_ref, ck_ref, cv_ref, o_ref, *, nblk):
    i = pl.program_id(1)
    past = ck_ref.shape[0]
    kcat = jnp.concatenate([kp_ref[...], kc_ref[...], kn_ref[...], ck_ref[...]], axis=0)
    vcat = jnp.concatenate([vp_ref[...], vc_ref[...], vn_ref[...], cv_ref[...]], axis=0)
    qi = lax.broadcasted_iota(jnp.int32, (BLOCK, BLOCK), 0)
    kj = lax.broadcasted_iota(jnp.int32, (BLOCK, BLOCK), 1)
    prev_bias = jnp.where(qi + jnp.where(i > 0, 0, BLOCK) <= kj, 0.0, -jnp.inf)
    next_bias = jnp.where(kj + jnp.where(i < nblk - 1, 0, BLOCK) <= qi, 0.0, -jnp.inf)
    _attend(q_ref[...], kcat, vcat, sink_ref, [(0, prev_bias), (2 * BLOCK, next_bias)], o_ref)


def _window_attention(q, k, v, cache_k4, cache_v4, layer, sink, B, T):
    R, dc = q.shape
    kvw = k.shape[1]
    nblk = T // BLOCK
    past = cache_k4.shape[2]
    cur = lambda b, i: (b * nblk + i, 0)
    prv = lambda b, i: (b * nblk + jnp.maximum(i - 1, 0), 0)
    nxt = lambda b, i: (b * nblk + jnp.minimum(i + 1, nblk - 1), 0)
    kvs = lambda f: pl.BlockSpec((BLOCK, kvw), f)
    ctx = pl.BlockSpec((None, None, past, kvw), lambda b, i: (b, layer, 0, 0))
    return pl.pallas_call(
        functools.partial(_win_attn_kernel, nblk=nblk),
        grid=(B, nblk),
        in_specs=[pl.BlockSpec(memory_space=pltpu.SMEM),
                  pl.BlockSpec((BLOCK, dc), cur),
                  kvs(prv), kvs(cur), kvs(nxt), kvs(prv), kvs(cur), kvs(nxt), ctx, ctx],
        out_specs=pl.BlockSpec((BLOCK, dc), cur),
        out_shape=jax.ShapeDtypeStruct((R, dc), BF16),
        compiler_params=_params("parallel", "parallel"), name="window_attention",
    )(sink, q, k, k, k, v, v, v, cache_k4, cache_v4)


def _merge_kernel(x_ref, mod_ref, gt_ref, ya_ref, yb_ref, ao_ref, wc_ref, wo_ref, g_ref, o_ref):
    dm = x_ref.shape[1]
    yc = _dot(ao_ref[...], wc_ref[...])
    gts = jax.nn.sigmoid(gt_ref[...].astype(F32))
    merged = gts[:, 0:dm] * ya_ref[...] + gts[:, dm:2 * dm] * yb_ref[...] + gts[:, 2 * dm:3 * dm] * yc
    mo = _dot(merged.astype(BF16), wo_ref[...])
    m = mod_ref[...]
    o_ref[...] = x_ref[...] + m[2:3] * _rms(mo, g_ref[...])


def _merge(x2, mod, gates, ya_t, yb, ao, w_c_out, w_o, g_post, B, T):
    R, D = x2.shape
    tm = min(256, T)
    nt = T // tm
    bm = mod.shape[0]
    mod_map = (lambda i: (i // nt, 0, 0)) if bm > 1 else (lambda i: (0, 0, 0))
    row = lambda w: pl.BlockSpec((tm, w), lambda i: (i, 0))
    return pl.pallas_call(
        _merge_kernel,
        grid=(R // tm,),
        in_specs=[row(D), pl.BlockSpec((None, 8, D), mod_map), row(gates.shape[1]),
                  pl.BlockSpec((tm, D), lambda i: (i % nt, i // nt)),
                  row(D), row(ao.shape[1]),
                  pl.BlockSpec(w_c_out.shape, lambda i: (0, 0)),
                  pl.BlockSpec(w_o.shape, lambda i: (0, 0)),
                  pl.BlockSpec((1, D), lambda i: (0, 0))],
        out_specs=row(D),
        out_shape=jax.ShapeDtypeStruct((R, D), F32),
        compiler_params=_params("parallel"), name="merge_out_proj",
    )(x2, mod, gates, ya_t, yb, ao, w_c_out, w_o, g_post)


def _ffn_kernel(*refs, halo, nt, tf):
    if halo:
        (x_ref, xp_ref, xn_ref, mod_ref, gpre_ref, wu_ref, cw_ref, cb_ref, wd_ref, gpost_ref,
         o_ref, h_ref, z_ref, a_ref) = refs
    else:
        x_ref, mod_ref, gpre_ref, wu_ref, cw_ref, cb_ref, wd_ref, gpost_ref, o_ref, h_ref, z_ref, a_ref = refs
    i = pl.program_id(0)
    tm = x_ref.shape[0]
    dff = wd_ref.shape[0]
    m = mod_ref[...]
    norm = lambda x: _rms(x, gpre_ref[...]) * (1.0 + m[4:5]) + m[3:4]
    h_ref[HALO:HALO + tm, :] = norm(x_ref[...]).astype(BF16)
    if halo:
        keep_prev = jnp.where(i % nt == 0, 0.0, 1.0)
        keep_next = jnp.where(i % nt == nt - 1, 0.0, 1.0)
        h_ref[0:HALO, :] = (norm(xp_ref[...]) * keep_prev).astype(BF16)
        h_ref[HALO + tm:, :] = (norm(xn_ref[...]) * keep_next).astype(BF16)
    else:
        h_ref[0:HALO, :] = jnp.zeros((HALO, h_ref.shape[1]), BF16)
        h_ref[HALO + tm:, :] = jnp.zeros((HALO, h_ref.shape[1]), BF16)
    hx = h_ref[...]

    def conv(slot, col):
        z_ref[slot] = _dot(hx, wu_ref[:, col:col + tf])
        cw = cw_ref[:, col:col + tf]
        return (z_ref[slot, HALO - 1:HALO - 1 + tm, :] * cw[0:1] + z_ref[slot, HALO:HALO + tm, :] * cw[1:2]
                + z_ref[slot, HALO + 1:HALO + 1 + tm, :] * cw[2:3] + cb_ref[:, col:col + tf])

    for c in range(dff // tf):
        g = conv(2 * (c % 2), c * tf)
        val = conv(2 * (c % 2) + 1, dff + c * tf)
        a_ref[:, c * tf:(c + 1) * tf] = (jax.nn.silu(g) * val).astype(BF16)
    y = _dot(a_ref[...], wd_ref[...])
    o_ref[...] = x_ref[...] + m[5:6] * _rms(y, gpost_ref[...])


def _conv_ffn(x2, mod, g_pre, w_up, conv_w, conv_b, w_down, g_post, B, T):
    R, D = x2.shape
    dff = w_down.shape[0]
    tf = 256
    tm = min(512, T)
    nt = T // tm
    halo = nt > 1
    bm = mod.shape[0]
    mod_map = (lambda i: (i // nt, 0, 0)) if bm > 1 else (lambda i: (0, 0, 0))
    hb = tm // HALO
    nhb = R // HALO
    const = lambda i: (0, 0)
    whole = lambda a: pl.BlockSpec(a.shape, const, pipeline_mode=pl.Buffered(1))
    in_specs = [pl.BlockSpec((tm, D), lambda i: (i, 0))]
    args = [x2]
    if halo:
        in_specs += [pl.BlockSpec((HALO, D), lambda i: (jnp.maximum(i * hb - 1, 0), 0)),
                     pl.BlockSpec((HALO, D), lambda i: (jnp.minimum((i + 1) * hb, nhb - 1), 0))]
        args += [x2, x2]
    in_specs += [pl.BlockSpec((None, 8, D), mod_map), pl.BlockSpec((1, D), const),
                 whole(w_up), whole(conv_w), whole(conv_b), whole(w_down), pl.BlockSpec((1, D), const)]
    args += [mod, g_pre, w_up, conv_w, conv_b, w_down, g_post]
    return pl.pallas_call(
        functools.partial(_ffn_kernel, halo=halo, nt=nt, tf=tf),
        grid=(R // tm,),
        in_specs=in_specs,
        out_specs=pl.BlockSpec((tm, D), lambda i: (i, 0)),
        out_shape=jax.ShapeDtypeStruct((R, D), F32),
        scratch_shapes=[pltpu.VMEM((tm + 2 * HALO, D), BF16),
                        pltpu.VMEM((4, tm + 2 * HALO, tf), F32),
                        pltpu.VMEM((tm, dff), BF16)],
        compiler_params=_params("parallel"), name="conv_ffn",
    )(*args)


def _rope_tables(T):
    t = jnp.arange(T)
    row = (t // GRID_W).astype(F32)[:, None]
    col = (t % GRID_W).astype(F32)[:, None]
    half = HEAD_DIM // 2
    inv = ROPE_BASE ** (-jnp.arange(0, half, 2, dtype=F32) / half)
    lane = jnp.arange(LANES)
    freq = inv[lane % (half // 2)][None, :]
    ang = jnp.where(((lane % HEAD_DIM) < half)[None, :], row * freq, col * freq)
    sign = jnp.where((lane % half) < half // 2, -1.0, 1.0)[None, :]
    return jnp.cos(ang), jnp.sin(ang) * sign


def _layer(x2, mod, lp, B, T, ctx):
    D = x2.shape[1]
    rope_tabs = None if ctx is None else lp['rope']
    a_t, uv, q, k, v, gates = _in_proj(x2, mod, lp['g_pre_mix'], lp['w_in'], B, T, lp['dims'], rope_tabs)
    h0 = (None, None) if ctx is None else ctx[3]
    y_bwd, br_re, br_im = _s5_pass(a_t, lp['s5_tabs'][1], B, T, True, h0=h0[1])
    ya_t, fw_re, fw_im = _s5_pass(a_t, lp['s5_tabs'][0], B, T, False, h0=h0[0],
                                  fuse=(y_bwd, lp['d_skip'], lp['w_glu']))
    yb = _sgu(uv, lp['g_sgu'], lp['w_spatial'], lp['sgu_bias'], lp['w_b_out'])
    if ctx is None:
        ao = _context_attention(q, k, v, lp['sink'], B, T)
    else:
        ao = _window_attention(q, k, v, ctx[0], ctx[1], ctx[2], lp['sink'], B, T)
    x2 = _merge(x2, mod, gates, ya_t, yb, ao, lp['w_c_out'], lp['w_o'], lp['g_post_mix'], B, T)
    x2 = _conv_ffn(x2, mod, lp['g_pre_ffn'], lp['w_up'], lp['conv_w'], lp['conv_b'], lp['w_down'],
                   lp['g_post_ffn'], B, T)
    return x2, (k, v, (fw_re, br_re), (fw_im, br_im))


def kernel(x_prompt, x_sample, cache_k, cache_v, state_ssm_re, state_ssm_im, c, c_ctx,
           w_mod, b_mod, g_pre_mix, g_post_mix, g_pre_ffn, g_post_ffn, w_in,
           lam_re, lam_im, log_step, b_re, b_im, c_re, c_im, d_skip, w_glu,
           g_sgu, w_spatial, b_spatial, w_b_out, sink, w_c_out, w_o,
           w_up, conv_w, conv_b, w_down):
    BP, TP, D = x_prompt.shape
    BS, TS, _ = x_sample.shape
    L = w_in.shape[0]
    G, N = lam_re.shape[2], lam_re.shape[3]
    P = b_re.shape[-1]
    d_a = G * P
    d_b2 = 2 * g_sgu.shape[1]
    d_c = w_c_out.shape[1]
    kv_w = N_KV * HEAD_DIM
    d_g = w_in.shape[2] - (d_a + d_b2 + d_c + 2 * kv_w)
    past = cache_k.shape[2]

    nc = 16
    cvecs = jnp.zeros((nc, D), F32).at[0].set(c_ctx).at[1:1 + BS].set(c)
    mod_all = _modulation(cvecs, w_mod, b_mod).reshape(L, nc, 6, D)
    mod_all = jnp.pad(mod_all, ((0, 0), (0, 0), (0, 2), (0, 0)))

    rope = _rope_tables(TS)
    cache_k4 = cache_k.reshape(BS, L, past, kv_w)
    cache_v4 = cache_v.reshape(BS, L, past, kv_w)

    xp = x_prompt.reshape(BP * TP, D)
    xs = x_sample.reshape(BS * TS, D)
    new_k, new_v, new_re, new_im = [], [], [], []
    for l in range(L):
        lb_re, lb_im, bb_re, bb_im = _s5_discretise(lam_re[l], lam_im[l], log_step[l], b_re[l], b_im[l])
        tabs = []
        for d in range(2):
            sl = slice(d * G, (d + 1) * G)
            tabs.append(_s5_tables(lb_re[sl], lb_im[sl], bb_re[sl], bb_im[sl], c_re[l, d], c_im[l, d], G, N, P))
        lp = {
            'dims': (d_a, d_b2, d_c, kv_w, d_g),
            'g_pre_mix': g_pre_mix[l][None], 'g_post_mix': g_post_mix[l][None],
            'g_pre_ffn': g_pre_ffn[l][None], 'g_post_ffn': g_post_ffn[l][None],
            'w_in': w_in[l].astype(BF16), 's5_tabs': tabs, 'd_skip': d_skip[l][None],
            'w_glu': w_glu[l].astype(BF16), 'g_sgu': g_sgu[l][None],
            'w_spatial': w_spatial[l].astype(BF16),
            'sgu_bias': jnp.repeat(b_spatial[l].T, g_sgu.shape[1] // b_spatial.shape[1], axis=1),
            'w_b_out': w_b_out[l].astype(BF16), 'sink': sink[l],
            'w_c_out': w_c_out[l].astype(BF16), 'w_o': w_o[l].astype(BF16),
            'w_up': w_up[l].astype(BF16), 'conv_w': conv_w[l], 'conv_b': conv_b[l][None],
            'w_down': w_down[l].astype(BF16), 'rope': rope,
        }
        xp, (k_l, v_l, f_re, f_im) = _layer(xp, mod_all[l, 0:1], lp, BP, TP, None)
        new_k.append(k_l.reshape(BP, TP, N_KV, HEAD_DIM))
        new_v.append(v_l.reshape(BP, TP, N_KV, HEAD_DIM))
        new_re.append(jnp.stack([f.reshape(BP, G, N) for f in f_re], axis=1))
        new_im.append(jnp.stack([f.reshape(BP, G, N) for f in f_im], axis=1))
        h0 = [(state_ssm_re[:, l, d].reshape(BS, G * N), state_ssm_im[:, l, d].reshape(BS, G * N)) for d in range(2)]
        xs, _ = _layer(xs, mod_all[l, 1:1 + BS], lp, BS, TS, (cache_k4, cache_v4, l, h0))
    return (xp.reshape(BP, TP, D), xs.reshape(BS, TS, D),
            jnp.stack(new_k, axis=1), jnp.stack(new_v, axis=1),
            jnp.stack(new_re, axis=1), jnp.stack(new_im, axis=1))
```

```python
import functools
import math

import jax
import jax.numpy as jnp
from jax import lax
from jax.experimental import pallas as pl
from jax.experimental.pallas import tpu as pltpu

F32 = jnp.float32
BF16 = jnp.bfloat16

GRID_W = 64
SSM_P = 16
SSM_N = 64
CHUNK = 128
N_HEADS = 8
N_KV = 2
HEAD_DIM = 64
BLOCK = 128
ROPE_BASE = 10000.0
EPS = 1e-6

LANES = 128
SUBLANES = 8
HALO = 16
VMEM_LIMIT = 56 * 1024 * 1024


def _params(*sem):
    return pltpu.CompilerParams(dimension_semantics=sem, vmem_limit_bytes=VMEM_LIMIT)


def _dot(a, b):
    return jnp.dot(a, b, preferred_element_type=F32)


def _rms(x, g):
    return x * lax.rsqrt(jnp.mean(x * x, axis=-1, keepdims=True) + EPS) * g


def _mod_kernel(c_ref, w_ref, b_ref, o_ref):
    a = jax.nn.silu(c_ref[...]).astype(BF16)
    o_ref[...] = _dot(a, w_ref[...].astype(BF16)) + b_ref[...]


def _modulation(cvecs, w_mod, b_mod):
    L, D, D6 = w_mod.shape
    NC = cvecs.shape[0]
    tn = D6 // 4
    return pl.pallas_call(
        _mod_kernel,
        grid=(L, D6 // tn),
        in_specs=[pl.BlockSpec((NC, D), lambda l, j: (0, 0)),
                  pl.BlockSpec((None, D, tn), lambda l, j: (l, 0, j)),
                  pl.BlockSpec((None, 1, tn), lambda l, j: (l, 0, j))],
        out_specs=pl.BlockSpec((None, NC, tn), lambda l, j: (l, 0, j)),
        out_shape=jax.ShapeDtypeStruct((L, NC, D6), F32),
        compiler_params=_params("parallel", "parallel"),
        name="modulation",
    )(cvecs, w_mod, b_mod.reshape(L, 1, D6))


def _rope(x, cos, sin):
    lane = lax.broadcasted_iota(jnp.int32, cos.shape, 1)
    first = (lane % 32) < 16
    outs = []
    for j in range(x.shape[1] // LANES):
        xb = x[:, j * LANES:(j + 1) * LANES]
        partner = jnp.where(first, pltpu.roll(xb, LANES - 16, 1), pltpu.roll(xb, 16, 1))
        outs.append(xb * cos + partner * sin)
    return outs[0] if len(outs) == 1 else jnp.concatenate(outs, axis=1)


def _inproj_kernel(*refs, rope, offs):
    if rope:
        x_ref, mod_ref, g_ref, w_ref, cos_ref, sin_ref, a_ref, uv_ref, q_ref, k_ref, v_ref, gt_ref = refs
    else:
        x_ref, mod_ref, g_ref, w_ref, a_ref, uv_ref, q_ref, k_ref, v_ref, gt_ref = refs
    ob, oq, ok, ov, og, dn = offs
    m = mod_ref[...]
    h = _rms(x_ref[...], g_ref[...]) * (1.0 + m[1:2]) + m[0:1]
    hb = h.astype(BF16)
    a_ref[...] = _dot(hb, w_ref[:, 0:ob])
    uv_ref[...] = _dot(hb, w_ref[:, ob:oq]).astype(BF16)
    q = _dot(hb, w_ref[:, oq:ok])
    k = _dot(hb, w_ref[:, ok:ov])
    if rope:
        q = _rope(q, cos_ref[...], sin_ref[...])
        k = _rope(k, cos_ref[...], sin_ref[...])
    q_ref[...] = q
    k_ref[...] = k
    v_ref[...] = _dot(hb, w_ref[:, ov:og])
    gt_ref[...] = _dot(hb, w_ref[:, og:dn]).astype(BF16)


def _in_proj(x2, mod, g_pre, w_in, B, T, dims, rope_tabs=None):
    R, D = x2.shape
    d_a, d_b2, d_c, kv_w, d_g = dims
    ob = d_a
    oq = ob + d_b2
    ok = oq + d_c
    ov = ok + kv_w
    og = ov + kv_w
    dn = og + d_g
    tm = min(256, T)
    nt = T // tm
    bm = mod.shape[0]
    mod_map = (lambda i: (i // nt, 0, 0)) if bm > 1 else (lambda i: (0, 0, 0))
    in_specs = [pl.BlockSpec((tm, D), lambda i: (i, 0)),
                pl.BlockSpec((None, 8, D), mod_map),
                pl.BlockSpec((1, D), lambda i: (0, 0)),
                pl.BlockSpec((D, dn), lambda i: (0, 0))]
    args = [x2, mod, g_pre, w_in]
    if rope_tabs is not None:
        in_specs += [pl.BlockSpec((tm, LANES), lambda i: (i % nt, 0))] * 2
        args += list(rope_tabs)
    row = lambda w: pl.BlockSpec((tm, w), lambda i: (i, 0))
    out_specs = [pl.BlockSpec((tm, d_a), lambda i: (i % nt, i // nt)),
                 row(d_b2), row(d_c), row(kv_w), row(kv_w), row(d_g)]
    out_shape = [jax.ShapeDtypeStruct((T, B * d_a), F32),
                 jax.ShapeDtypeStruct((R, d_b2), BF16),
                 jax.ShapeDtypeStruct((R, d_c), F32),
                 jax.ShapeDtypeStruct((R, kv_w), F32),
                 jax.ShapeDtypeStruct((R, kv_w), F32),
                 jax.ShapeDtypeStruct((R, d_g), BF16)]
    return pl.pallas_call(
        functools.partial(_inproj_kernel, rope=rope_tabs is not None, offs=(ob, oq, ok, ov, og, dn)),
        grid=(R // tm,), in_specs=in_specs, out_specs=out_specs, out_shape=out_shape,
        compiler_params=_params("parallel"), name="in_proj",
    )(*args)


def _s5_disc_kernel(lr_ref, li_ref, ls_ref, br_ref, bi_ref, cr_ref, ci_ref,
                    l2r_ref, l2i_ref, bbr_ref, bbi_ref, lbbr_ref, lbbi_ref,
                    c1r_ref, c1i_ref, c2r_ref, c2i_ref, cb_ref, clb_ref):
    lr = lr_ref[...]
    li = li_ref[...]
    dt = jnp.exp(ls_ref[...])
    ar = lr * dt
    ai = li * dt
    mag = jnp.exp(ar)
    lb_re = (mag * jnp.cos(ai))[:, None, :]
    lb_im = (mag * jnp.sin(ai))[:, None, :]
    lr3 = lr[:, None, :]
    li3 = li[:, None, :]
    den = lr3 * lr3 + li3 * li3
    f_re = ((lb_re - 1.0) * lr3 + lb_im * li3) / den
    f_im = (lb_im * lr3 - (lb_re - 1.0) * li3) / den
    l2_re = lb_re * lb_re - lb_im * lb_im
    l2_im = 2.0 * lb_re * lb_im
    l2r_ref[...] = l2_re
    l2i_ref[...] = l2_im
    br = br_ref[...]
    bi = bi_ref[...]
    bb_re = f_re * br - f_im * bi
    bb_im = f_re * bi + f_im * br
    lbb_re = lb_re * bb_re - lb_im * bb_im
    lbb_im = lb_re * bb_im + lb_im * bb_re
    bbr_ref[...] = bb_re
    bbi_ref[...] = bb_im
    lbbr_ref[...] = lbb_re
    lbbi_ref[...] = lbb_im
    cr = cr_ref[...]
    ci = ci_ref[...]
    c1r_ref[...] = cr * lb_re - ci * lb_im
    c1i_ref[...] = cr * lb_im + ci * lb_re
    c2r_ref[...] = cr * l2_re - ci * l2_im
    c2i_ref[...] = cr * l2_im + ci * l2_re
    for q in range(br.shape[1]):
        cb_ref[:, q, :] = jnp.sum(cr * bb_re[:, q:q + 1, :] - ci * bb_im[:, q:q + 1, :], axis=-1)
        clb_ref[:, q, :] = jnp.sum(cr * lbb_re[:, q:q + 1, :] - ci * lbb_im[:, q:q + 1, :], axis=-1)


def _s5_discretise(lam_re, lam_im, log_step, b_re, b_im, c_re, c_im):
    two, G, N = lam_re.shape
    P = b_re.shape[-1]
    rows = two * G
    tr = lambda b: jnp.swapaxes(b, -1, -2).reshape(rows, P, N)
    vec = jax.ShapeDtypeStruct((rows, 1, N), F32)
    mat = jax.ShapeDtypeStruct((rows, P, N), F32)
    sq = jax.ShapeDtypeStruct((rows, P, P), F32)
    return pl.pallas_call(_s5_disc_kernel, out_shape=[vec, vec] + [mat] * 8 + [sq, sq], name="s5_discretise")(
        lam_re.reshape(rows, N), lam_im.reshape(rows, N), log_step.reshape(rows, 1), tr(b_re), tr(b_im),
        c_re.reshape(rows, P, N), c_im.reshape(rows, P, N))


def _s5_tables(disc, G, N, P):
    l2_re, l2_im, bb_re, bb_im, lbb_re, lbb_im, c1_re, c1_im, c2_re, c2_im, cb, clb = disc
    GB = LANES // P
    NB = G // GB
    eye = jnp.eye(GB, dtype=F32)

    def w_in(x):
        return jnp.einsum('kgpn,gh->kgphn', x.reshape(NB, GB, P, N), eye).reshape(NB, GB * P, GB * N)

    def w_out(x):
        return jnp.einsum('kgpn,gh->kgnhp', x.reshape(NB, GB, P, N), eye).reshape(NB, GB * N, GB * P)

    def w_dir(x):
        return jnp.einsum('kgqp,gh->kgqhp', x.reshape(NB, GB, P, P), eye).reshape(NB, GB * P, GB * P)

    w_x = jnp.concatenate([jnp.concatenate([w_in(lbb_re), w_in(lbb_im)], axis=-1),
                           jnp.concatenate([w_in(bb_re), w_in(bb_im)], axis=-1)], axis=1).astype(BF16)
    w_y = jnp.concatenate([jnp.concatenate([w_out(c1_re), -w_out(c1_im)], axis=1),
                           jnp.concatenate([w_out(c2_re), -w_out(c2_im)], axis=1)], axis=-1).astype(BF16)
    d0, d1 = w_dir(cb), w_dir(clb)
    w_u = jnp.concatenate([jnp.concatenate([d0, d1], axis=-1),
                           jnp.concatenate([jnp.zeros_like(d0), d0], axis=-1)], axis=1).astype(BF16)
    lam2 = jnp.concatenate([l2_re.reshape(NB, 1, GB * N), l2_im.reshape(NB, 1, GB * N)], axis=-1)
    return w_x, w_y, w_u, lam2


def _s5_kernel(*refs, reverse, has_h0, fuse, nb, sw):
    it = iter(refs)
    u_ref, wx_ref, wy_ref, wu_ref, lam_ref = next(it), next(it), next(it), next(it), next(it)
    if has_h0:
        h0r_ref, h0i_ref = next(it), next(it)
    if fuse:
        yb_ref, dsk_ref = next(it), next(it)
    y_ref, fr_ref, fi_ref = next(it), next(it), next(it)
    xs_ref, st_ref = next(it), next(it)
    if fuse:
        yacc_ref = next(it)

    tc, _, bs, _ = u_ref.shape
    rows = tc * bs
    c = pl.program_id(1)
    first, second = (1, 0) if reverse else (0, 1)

    @pl.when(c == 0)
    def _():
        for j in range(nb):
            if has_h0:
                st_ref[j, :, 0:sw] = h0r_ref[:, j * sw:(j + 1) * sw]
                st_ref[j, :, sw:2 * sw] = h0i_ref[:, j * sw:(j + 1) * sw]
            else:
                st_ref[j] = jnp.zeros(st_ref.shape[1:], F32)

    for j in range(nb):
        cs = slice(j * LANES, (j + 1) * LANES)
        u2 = jnp.concatenate([u_ref[:, first, :, cs].reshape(rows, LANES),
                              u_ref[:, second, :, cs].reshape(rows, LANES)], axis=1).astype(BF16)
        xs_ref[...] = _dot(u2, wx_ref[j]).reshape(tc, bs, 2 * sw)
        lr = jnp.broadcast_to(lam_ref[j, :, 0:sw], (bs, sw))
        li = jnp.broadcast_to(lam_ref[j, :, sw:2 * sw], (bs, sw))

        def step(i, carry, lr=lr, li=li):
            s_re, s_im = carry
            t = (tc - 1 - i) if reverse else i
            x_re = xs_ref[t, :, 0:sw]
            x_im = xs_ref[t, :, sw:2 * sw]
            xs_ref[t, :, 0:sw] = s_re
            xs_ref[t, :, sw:2 * sw] = s_im
            return lr * s_re - li * s_im + x_re, lr * s_im + li * s_re + x_im

        s_re, s_im = lax.fori_loop(0, tc, step, (st_ref[j, :, 0:sw], st_ref[j, :, sw:2 * sw]), unroll=8)
        st_ref[j, :, 0:sw] = s_re
        st_ref[j, :, sw:2 * sw] = s_im
        fr_ref[:, j * sw:(j + 1) * sw] = s_re
        fi_ref[:, j * sw:(j + 1) * sw] = s_im
        yj = _dot(xs_ref[...].reshape(rows, 2 * sw).astype(BF16), wy_ref[j]) + _dot(u2, wu_ref[j])
        for half, parity in ((0, first), (1, second)):
            yh = yj[:, half * LANES:(half + 1) * LANES]
            if fuse:
                yacc_ref[parity, :, cs] = yh
            else:
                y_ref[:, parity, :, cs] = yh.reshape(tc, bs, LANES)

    if fuse:
        da = nb * LANES
        for parity in range(2):
            y = (yacc_ref[parity] + yb_ref[:, parity].reshape(rows, da)
                 + dsk_ref[...] * u_ref[:, parity].reshape(rows, da))
            y_ref[:, parity] = jax.nn.gelu(y).reshape(tc, bs, da)


def _s5_pass(u_t, tabs, B, T, reverse, h0=None, fuse=None):
    w_x, w_y, w_u, lam2 = tabs
    nb, _, sw2 = w_x.shape
    sw = sw2 // 2
    da = nb * LANES
    bs = SUBLANES
    tc = 64
    nct = T // (2 * tc)
    u4 = u_t.reshape(T // 2, 2, B, da)
    tmap = (lambda g, c: (nct - 1 - c, 0, g, 0)) if reverse else (lambda g, c: (c, 0, g, 0))
    const3 = lambda g, c: (0, 0, 0)
    blk = pl.BlockSpec((tc, 2, bs, da), tmap)
    in_specs = [blk] + [pl.BlockSpec(a.shape, const3) for a in (w_x, w_y, w_u, lam2)]
    args = [u4, w_x, w_y, w_u, lam2]
    if h0 is not None:
        in_specs += [pl.BlockSpec((bs, nb * sw), lambda g, c: (g, 0))] * 2
        args += list(h0)
    scratch = [pltpu.VMEM((tc, bs, sw2), F32), pltpu.VMEM((nb, bs, sw2), F32)]
    if fuse is not None:
        y_other, d_skip = fuse
        in_specs += [blk, pl.BlockSpec((1, da), lambda g, c: (0, 0))]
        args += [y_other.reshape(T // 2, 2, B, da), d_skip]
        scratch.append(pltpu.VMEM((2, tc * bs, da), F32))
    fin = jax.ShapeDtypeStruct((B, nb * sw), F32)
    y, f_re, f_im = pl.pallas_call(
        functools.partial(_s5_kernel, reverse=reverse, has_h0=h0 is not None, fuse=fuse is not None, nb=nb, sw=sw),
        grid=(B // bs, nct), in_specs=in_specs,
        out_specs=[blk,
                   pl.BlockSpec((bs, nb * sw), lambda g, c: (g, 0)),
                   pl.BlockSpec((bs, nb * sw), lambda g, c: (g, 0))],
        out_shape=[jax.ShapeDtypeStruct((T // 2, 2, B, da), F32), fin, fin],
        scratch_shapes=scratch,
        compiler_params=_params("parallel", "arbitrary"),
        name="s5_scan_gelu" if fuse is not None else "s5_scan",
    )(*args)
    return y.reshape(T, B * da), f_re, f_im


def _sgu_kernel(uv_ref, g_ref, ws_ref, bias_ref, o_ref, *, nch):
    db = g_ref.shape[1]
    x = jax.nn.gelu(uv_ref[...].astype(F32))
    u = x[:, :db]
    v = x[:, db:]
    mu = jnp.mean(v, axis=-1, keepdims=True)
    vc = v - mu
    var = jnp.mean(vc * vc, axis=-1, keepdims=True)
    vn = (vc * lax.rsqrt(var + EPS) * g_ref[...]).astype(BF16)
    lane = lax.broadcasted_iota(jnp.int32, (CHUNK, LANES), 1)
    lo = lane < (LANES // 2)
    bias = bias_ref[...]
    for c in range(nch):
        r0 = c * CHUNK
        parts = []
        for hp in range(db // LANES):
            vb = vn[r0:r0 + CHUNK, hp * LANES:(hp + 1) * LANES]
            parts.append(jnp.where(lo, _dot(ws_ref[2 * hp], vb), _dot(ws_ref[2 * hp + 1], vb)))
        mixed = jnp.concatenate(parts, axis=1) + bias
        o_ref[r0:r0 + CHUNK, :] = (u[r0:r0 + CHUNK] * mixed).astype(BF16)


def _sgu(uv, g_sgu, w_spatial, bias_full):
    R, db2 = uv.shape
    db = db2 // 2
    nch = 4
    tm = nch * CHUNK
    return pl.pallas_call(
        functools.partial(_sgu_kernel, nch=nch),
        grid=(R // tm,),
        in_specs=[pl.BlockSpec((tm, db2), lambda i: (i, 0)),
                  pl.BlockSpec((1, db), lambda i: (0, 0)),
                  pl.BlockSpec(w_spatial.shape, lambda i: (0, 0, 0)),
                  pl.BlockSpec(bias_full.shape, lambda i: (0, 0))],
        out_specs=pl.BlockSpec((tm, db), lambda i: (i, 0)),
        out_shape=jax.ShapeDtypeStruct((R, db), BF16),
        compiler_params=_params("parallel"), name="chunk_sgu",
    )(uv, g_sgu, w_spatial, bias_full)


def _attend(q, kcat, vcat, sink_ref, bias, o_ref):
    assert N_KV * HEAD_DIM == LANES and kcat.shape[1] == LANES
    M = q.shape[0]
    scale = HEAD_DIM ** -0.5
    lane = lax.broadcasted_iota(jnp.int32, (M, LANES), 1)
    lo = lane < HEAD_DIM
    heads_per_kv = N_HEADS // N_KV
    rows = []
    for blk in range(N_HEADS * HEAD_DIM // LANES):
        kv = (2 * blk) // heads_per_kv
        qb = q[:, blk * LANES:(blk + 1) * LANES] * scale
        qr = pltpu.roll(qb, HEAD_DIM, 1)
        if kv == 0:
            rows += [jnp.where(lo, qb, 0.0), jnp.where(lo, qr, 0.0)]
        else:
            rows += [jnp.where(lo, 0.0, qr), jnp.where(lo, 0.0, qb)]
    qst = jnp.concatenate(rows, axis=0).astype(BF16)
    s = lax.dot_general(qst, kcat.astype(BF16), (((1,), (1,)), ((), ())), preferred_element_type=F32)
    if bias is not None:
        cols, at = [], 0
        for off, blk_bias in bias:
            if off > at:
                cols.append(s[:, at:off])
            cols.append(s[:, off:off + LANES] + jnp.concatenate([blk_bias] * N_HEADS, axis=0))
            at = off + LANES
        cols.append(s[:, at:])
        s = jnp.concatenate(cols, axis=1)
    ps, es = [], []
    for h in range(N_HEADS):
        sh = s[h * M:(h + 1) * M]
        mh = jnp.maximum(jnp.max(sh, axis=-1, keepdims=True), sink_ref[h])
        ps.append(jnp.exp(sh - mh).astype(BF16))
        es.append(jnp.exp(sink_ref[h] - mh))
    vaug = jnp.concatenate([vcat.astype(BF16), jnp.ones(vcat.shape, BF16)], axis=1)
    oa = _dot(jnp.concatenate(ps, axis=0), vaug)
    o = [oa[h * M:(h + 1) * M, :LANES] * (1.0 / (oa[h * M:(h + 1) * M, LANES:] + es[h])) for h in range(N_HEADS)]
    for blk in range(N_HEADS * HEAD_DIM // LANES):
        kv = (2 * blk) // heads_per_kv
        oa_, ob_ = o[2 * blk], o[2 * blk + 1]
        if kv == 0:
            out = jnp.where(lo, oa_, pltpu.roll(ob_, HEAD_DIM, 1))
        else:
            out = jnp.where(lo, pltpu.roll(oa_, HEAD_DIM, 1), ob_)
        o_ref[:, blk * LANES:(blk + 1) * LANES] = out.astype(o_ref.dtype)


def _ctx_attn_kernel(sink_ref, q_ref, k_ref, v_ref, o_ref):
    _attend(q_ref[...], k_ref[...], v_ref[...], sink_ref, None, o_ref)


def _context_attention(q, k, v, sink, B, T):
    R, dc = q.shape
    kvw = k.shape[1]
    nq = T // BLOCK
    return pl.pallas_call(
        _ctx_attn_kernel,
        grid=(B, nq),
        in_specs=[pl.BlockSpec(memory_space=pltpu.SMEM),
                  pl.BlockSpec((BLOCK, dc), lambda b, i: (b * nq + i, 0)),
                  pl.BlockSpec((T, kvw), lambda b, i: (b, 0)),
                  pl.BlockSpec((T, kvw), lambda b, i: (b, 0))],
        out_specs=pl.BlockSpec((BLOCK, dc), lambda b, i: (b * nq + i, 0)),
        out_shape=jax.ShapeDtypeStruct((R, dc), BF16),
        compiler_params=_params("parallel", "parallel"), name="context_attention",
    )(sink, q, k, v)


def _win_attn_kernel(sink_ref, q_ref, kp_ref, kc_ref, kn_ref, vp_ref, vc_ref, vn_ref, ck_ref, cv_ref, o_ref, *, nblk):
    i = pl.program_id(1)
    k_blocks = [kp_ref[...], kc_ref[0:BLOCK], kc_ref[BLOCK:2 * BLOCK], kn_ref[...]]
    v_blocks = [vp_ref[...], vc_ref[0:BLOCK], vc_ref[BLOCK:2 * BLOCK], vn_ref[...]]
    qi = lax.broadcasted_iota(jnp.int32, (BLOCK, BLOCK), 0)
    kj = lax.broadcasted_iota(jnp.int32, (BLOCK, BLOCK), 1)
    edge = [jnp.where(i > 0, 0, BLOCK), 0, 0, jnp.where(2 * i + 2 < nblk, 0, BLOCK)]
    for half in range(2):
        kcat = jnp.concatenate(k_blocks[half:half + 3] + [ck_ref[...]], axis=0)
        vcat = jnp.concatenate(v_blocks[half:half + 3] + [cv_ref[...]], axis=0)
        prev_bias = jnp.where(qi + edge[half] <= kj, 0.0, -jnp.inf)
        next_bias = jnp.where(kj + edge[half + 2] <= qi, 0.0, -jnp.inf)
        _attend(q_ref[half * BLOCK:(half + 1) * BLOCK, :], kcat, vcat, sink_ref,
                [(0, prev_bias), (2 * BLOCK, next_bias)], o_ref.at[half * BLOCK:(half + 1) * BLOCK, :])


def _window_attention(q, k, v, cache_k4, cache_v4, layer, sink, B, T):
    R, dc = q.shape
    kvw = k.shape[1]
    nblk = T // BLOCK
    past = cache_k4.shape[2]
    npair = nblk // 2
    assert nblk == 2 * npair
    cur = lambda b, i: (b * npair + i, 0)
    prv = lambda b, i: (b * nblk + jnp.maximum(2 * i - 1, 0), 0)
    nxt = lambda b, i: (b * nblk + jnp.minimum(2 * i + 2, nblk - 1), 0)
    kvs = lambda f: pl.BlockSpec((BLOCK, kvw), f)
    kv2 = pl.BlockSpec((2 * BLOCK, kvw), cur)
    ctx = pl.BlockSpec((None, None, past, kvw), lambda b, i: (b, layer, 0, 0))
    return pl.pallas_call(
        functools.partial(_win_attn_kernel, nblk=nblk),
        grid=(B, npair),
        in_specs=[pl.BlockSpec(memory_space=pltpu.SMEM),
                  pl.BlockSpec((2 * BLOCK, dc), cur),
                  kvs(prv), kv2, kvs(nxt), kvs(prv), kv2, kvs(nxt), ctx, ctx],
        out_specs=pl.BlockSpec((2 * BLOCK, dc), cur),
        out_shape=jax.ShapeDtypeStruct((R, dc), BF16),
        compiler_params=_params("parallel", "parallel"), name="window_attention",
    )(sink, q, k, k, k, v, v, v, cache_k4, cache_v4)


def _merge_kernel(x_ref, mod_ref, gt_ref, ga_ref, pb_ref, ao_ref, wglu_ref, wb_ref, wc_ref, wo_ref, g_ref, o_ref):
    dm = x_ref.shape[1]
    z = _dot(ga_ref[...].astype(BF16), wglu_ref[...])
    ya = z[:, :dm] * jax.nn.sigmoid(z[:, dm:])
    yb = _dot(pb_ref[...], wb_ref[...])
    yc = _dot(ao_ref[...], wc_ref[...])
    gts = jax.nn.sigmoid(gt_ref[...].astype(F32))
    merged = gts[:, 0:dm] * ya + gts[:, dm:2 * dm] * yb + gts[:, 2 * dm:3 * dm] * yc
    mo = _dot(merged.astype(BF16), wo_ref[...])
    m = mod_ref[...]
    o_ref[...] = x_ref[...] + m[2:3] * _rms(mo, g_ref[...])


def _merge(x2, mod, gates, ga_t, pb, ao, w_glu, w_b_out, w_c_out, w_o, g_post, B, T):
    R, D = x2.shape
    tm = min(256, T)
    nt = T // tm
    bm = mod.shape[0]
    da = w_glu.shape[0]
    mod_map = (lambda i: (i // nt, 0, 0)) if bm > 1 else (lambda i: (0, 0, 0))
    row = lambda w: pl.BlockSpec((tm, w), lambda i: (i, 0))
    whole = lambda a: pl.BlockSpec(a.shape, lambda i: (0, 0))
    return pl.pallas_call(
        _merge_kernel,
        grid=(R // tm,),
        in_specs=[row(D), pl.BlockSpec((None, 8, D), mod_map), row(gates.shape[1]),
                  pl.BlockSpec((tm, da), lambda i: (i % nt, i // nt)),
                  row(pb.shape[1]), row(ao.shape[1]),
                  whole(w_glu), whole(w_b_out), whole(w_c_out), whole(w_o),
                  pl.BlockSpec((1, D), lambda i: (0, 0))],
        out_specs=row(D),
        out_shape=jax.ShapeDtypeStruct((R, D), F32),
        compiler_params=_params("parallel"), name="merge_out_proj",
    )(x2, mod, gates, ga_t, pb, ao, w_glu, w_b_out, w_c_out, w_o, g_post)


def _ffn_kernel(*refs, halo, nt, tf):
    if halo:
        (x_ref, xp_ref, xn_ref, mod_ref, gpre_ref, wu_ref, cw_ref, cb_ref, wd_ref, gpost_ref,
         o_ref, h_ref, z_ref, a_ref) = refs
    else:
        x_ref, mod_ref, gpre_ref, wu_ref, cw_ref, cb_ref, wd_ref, gpost_ref, o_ref, h_ref, z_ref, a_ref = refs
    i = pl.program_id(0)
    tm = x_ref.shape[0]
    dff = wd_ref.shape[0]
    m = mod_ref[...]
    norm = lambda x: _rms(x, gpre_ref[...]) * (1.0 + m[4:5]) + m[3:4]
    h_ref[HALO:HALO + tm, :] = norm(x_ref[...]).astype(BF16)
    if halo:
        keep_prev = jnp.where(i % nt == 0, 0.0, 1.0)
        keep_next = jnp.where(i % nt == nt - 1, 0.0, 1.0)
        h_ref[0:HALO, :] = (norm(xp_ref[...]) * keep_prev).astype(BF16)
        h_ref[HALO + tm:, :] = (norm(xn_ref[...]) * keep_next).astype(BF16)
    else:
        h_ref[0:HALO, :] = jnp.zeros((HALO, h_ref.shape[1]), BF16)
        h_ref[HALO + tm:, :] = jnp.zeros((HALO, h_ref.shape[1]), BF16)
    hx = h_ref[...]

    def conv(slot, col):
        z_ref[slot] = _dot(hx, wu_ref[:, col:col + tf])
        cw = cw_ref[:, col:col + tf]
        return (z_ref[slot, HALO - 1:HALO - 1 + tm, :] * cw[0:1] + z_ref[slot, HALO:HALO + tm, :] * cw[1:2]
                + z_ref[slot, HALO + 1:HALO + 1 + tm, :] * cw[2:3] + cb_ref[:, col:col + tf])

    for c in range(dff // tf):
        g = conv(2 * (c % 2), c * tf)
        val = conv(2 * (c % 2) + 1, dff + c * tf)
        a_ref[:, c * tf:(c + 1) * tf] = (jax.nn.silu(g) * val).astype(BF16)
    y = _dot(a_ref[...], wd_ref[...])
    o_ref[...] = x_ref[...] + m[5:6] * _rms(y, gpost_ref[...])


def _conv_ffn(x2, mod, g_pre, w_up, conv_w, conv_b, w_down, g_post, B, T):
    R, D = x2.shape
    dff = w_down.shape[0]
    tf = 256
    tm = min(512, T)
    nt = T // tm
    halo = nt > 1
    bm = mod.shape[0]
    mod_map = (lambda i: (i // nt, 0, 0)) if bm > 1 else (lambda i: (0, 0, 0))
    hb = tm // HALO
    nhb = R // HALO
    const = lambda i: (0, 0)
    whole = lambda a: pl.BlockSpec(a.shape, const, pipeline_mode=pl.Buffered(1))
    in_specs = [pl.BlockSpec((tm, D), lambda i: (i, 0))]
    args = [x2]
    if halo:
        in_specs += [pl.BlockSpec((HALO, D), lambda i: (jnp.maximum(i * hb - 1, 0), 0)),
                     pl.BlockSpec((HALO, D), lambda i: (jnp.minimum((i + 1) * hb, nhb - 1), 0))]
        args += [x2, x2]
    in_specs += [pl.BlockSpec((None, 8, D), mod_map), pl.BlockSpec((1, D), const),
                 whole(w_up), whole(conv_w), whole(conv_b), whole(w_down), pl.BlockSpec((1, D), const)]
    args += [mod, g_pre, w_up, conv_w, conv_b, w_down, g_post]
    return pl.pallas_call(
        functools.partial(_ffn_kernel, halo=halo, nt=nt, tf=tf),
        grid=(R // tm,),
        in_specs=in_specs,
        out_specs=pl.BlockSpec((tm, D), lambda i: (i, 0)),
        out_shape=jax.ShapeDtypeStruct((R, D), F32),
        scratch_shapes=[pltpu.VMEM((tm + 2 * HALO, D), BF16),
                        pltpu.VMEM((4, tm + 2 * HALO, tf), F32),
                        pltpu.VMEM((tm, dff), BF16)],
        compiler_params=_params("parallel"), name="conv_ffn",
    )(*args)


def _rope_tables(T):
    t = jnp.arange(T)
    row = (t // GRID_W).astype(F32)[:, None]
    col = (t % GRID_W).astype(F32)[:, None]
    half = HEAD_DIM // 2
    inv = ROPE_BASE ** (-jnp.arange(0, half, 2, dtype=F32) / half)
    lane = jnp.arange(LANES)
    freq = inv[lane % (half // 2)][None, :]
    ang = jnp.where(((lane % HEAD_DIM) < half)[None, :], row * freq, col * freq)
    sign = jnp.where((lane % half) < half // 2, -1.0, 1.0)[None, :]
    return jnp.cos(ang), jnp.sin(ang) * sign


def _layer(x2, mod, lp, B, T, ctx):
    D = x2.shape[1]
    rope_tabs = None if ctx is None else lp['rope']
    a_t, uv, q, k, v, gates = _in_proj(x2, mod, lp['g_pre_mix'], lp['w_in'], B, T, lp['dims'], rope_tabs)
    h0 = (None, None) if ctx is None else ctx[3]
    y_bwd, br_re, br_im = _s5_pass(a_t, lp['s5_tabs'][1], B, T, True, h0=h0[1])
    ga_t, fw_re, fw_im = _s5_pass(a_t, lp['s5_tabs'][0], B, T, False, h0=h0[0], fuse=(y_bwd, lp['d_skip']))
    pb = _sgu(uv, lp['g_sgu'], lp['w_spatial'], lp['sgu_bias'])
    if ctx is None:
        ao = _context_attention(q, k, v, lp['sink'], B, T)
    else:
        ao = _window_attention(q, k, v, ctx[0], ctx[1], ctx[2], lp['sink'], B, T)
    x2 = _merge(x2, mod, gates, ga_t, pb, ao, lp['w_glu'], lp['w_b_out'], lp['w_c_out'], lp['w_o'],
                lp['g_post_mix'], B, T)
    x2 = _conv_ffn(x2, mod, lp['g_pre_ffn'], lp['w_up'], lp['conv_w'], lp['conv_b'], lp['w_down'],
                   lp['g_post_ffn'], B, T)
    return x2, (k, v, (fw_re, br_re), (fw_im, br_im))


def kernel(x_prompt, x_sample, cache_k, cache_v, state_ssm_re, state_ssm_im, c, c_ctx,
           w_mod, b_mod, g_pre_mix, g_post_mix, g_pre_ffn, g_post_ffn, w_in,
           lam_re, lam_im, log_step, b_re, b_im, c_re, c_im, d_skip, w_glu,
           g_sgu, w_spatial, b_spatial, w_b_out, sink, w_c_out, w_o,
           w_up, conv_w, conv_b, w_down):
    BP, TP, D = x_prompt.shape
    BS, TS, _ = x_sample.shape
    L = w_in.shape[0]
    G, N = lam_re.shape[2], lam_re.shape[3]
    P = b_re.shape[-1]
    d_a = G * P
    d_b2 = 2 * g_sgu.shape[1]
    d_c = w_c_out.shape[1]
    kv_w = N_KV * HEAD_DIM
    d_g = w_in.shape[2] - (d_a + d_b2 + d_c + 2 * kv_w)
    past = cache_k.shape[2]

    nc = 16
    cvecs = jnp.zeros((nc, D), F32).at[0].set(c_ctx).at[1:1 + BS].set(c)
    mod_all = _modulation(cvecs, w_mod, b_mod).reshape(L, nc, 6, D)
    mod_all = jnp.pad(mod_all, ((0, 0), (0, 0), (0, 2), (0, 0)))

    rope = _rope_tables(TS)
    cache_k4 = cache_k.reshape(BS, L, past, kv_w)
    cache_v4 = cache_v.reshape(BS, L, past, kv_w)

    xp = x_prompt.reshape(BP * TP, D)
    xs = x_sample.reshape(BS * TS, D)
    new_k, new_v, new_re, new_im = [], [], [], []
    for l in range(L):
        disc = _s5_discretise(lam_re[l], lam_im[l], log_step[l], b_re[l], b_im[l], c_re[l], c_im[l])
        tabs = [_s5_tables([a[d * G:(d + 1) * G] for a in disc], G, N, P) for d in range(2)]
        lp = {
            'dims': (d_a, d_b2, d_c, kv_w, d_g),
            'g_pre_mix': g_pre_mix[l][None], 'g_post_mix': g_post_mix[l][None],
            'g_pre_ffn': g_pre_ffn[l][None], 'g_post_ffn': g_post_ffn[l][None],
            'w_in': w_in[l].astype(BF16), 's5_tabs': tabs, 'd_skip': d_skip[l][None],
            'w_glu': w_glu[l].astype(BF16), 'g_sgu': g_sgu[l][None],
            'w_spatial': w_spatial[l].astype(BF16),
            'sgu_bias': jnp.repeat(b_spatial[l].T, g_sgu.shape[1] // b_spatial.shape[1], axis=1),
            'w_b_out': w_b_out[l].astype(BF16), 'sink': sink[l],
            'w_c_out': w_c_out[l].astype(BF16), 'w_o': w_o[l].astype(BF16),
            'w_up': w_up[l].astype(BF16), 'conv_w': conv_w[l], 'conv_b': conv_b[l][None],
            'w_down': w_down[l].astype(BF16), 'rope': rope,
        }
        xp, (k_l, v_l, f_re, f_im) = _layer(xp, mod_all[l, 0:1], lp, BP, TP, None)
        new_k.append(k_l.reshape(BP, TP, N_KV, HEAD_DIM))
        new_v.append(v_l.reshape(BP, TP, N_KV, HEAD_DIM))
        new_re.append(jnp.stack([f.reshape(BP, G, N) for f in f_re], axis=1))
        new_im.append(jnp.stack([f.reshape(BP, G, N) for f in f_im], axis=1))
        h0 = [(state_ssm_re[:, l, d].reshape(BS, G * N), state_ssm_im[:, l, d].reshape(BS, G * N)) for d in range(2)]
        xs, _ = _layer(xs, mod_all[l, 1:1 + BS], lp, BS, TS, (cache_k4, cache_v4, l, h0))
    return (xp.reshape(BP, TP, D), xs.reshape(BS, TS, D),
            jnp.stack(new_k, axis=1), jnp.stack(new_v, axis=1),
            jnp.stack(new_re, axis=1), jnp.stack(new_im, axis=1))
```

```python
import functools
import math

import jax
import jax.numpy as jnp
from jax import lax
from jax.experimental import pallas as pl
from jax.experimental.pallas import tpu as pltpu

F32 = jnp.float32
BF16 = jnp.bfloat16

GRID_W = 64
SSM_P = 16
SSM_N = 64
CHUNK = 128
N_HEADS = 8
N_KV = 2
HEAD_DIM = 64
BLOCK = 128
ROPE_BASE = 10000.0
EPS = 1e-6

LANES = 128
SUBLANES = 8
HALO = 16
VMEM_LIMIT = 56 * 1024 * 1024


def _params(*sem):
    return pltpu.CompilerParams(dimension_semantics=sem, vmem_limit_bytes=VMEM_LIMIT)


def _dot(a, b):
    return jnp.dot(a, b, preferred_element_type=F32)


def _rms(x, g):
    return x * lax.rsqrt(jnp.mean(x * x, axis=-1, keepdims=True) + EPS) * g


def _mod_kernel(c_ref, w_ref, b_ref, o_ref):
    a = jax.nn.silu(c_ref[...]).astype(BF16)
    o_ref[...] = _dot(a, w_ref[...].astype(BF16)) + b_ref[...]


def _modulation(cvecs, w_mod, b_mod):
    L, D, D6 = w_mod.shape
    NC = cvecs.shape[0]
    tn = D6 // 4
    return pl.pallas_call(
        _mod_kernel,
        grid=(L, D6 // tn),
        in_specs=[pl.BlockSpec((NC, D), lambda l, j: (0, 0)),
                  pl.BlockSpec((None, D, tn), lambda l, j: (l, 0, j)),
                  pl.BlockSpec((None, 1, tn), lambda l, j: (l, 0, j))],
        out_specs=pl.BlockSpec((None, NC, tn), lambda l, j: (l, 0, j)),
        out_shape=jax.ShapeDtypeStruct((L, NC, D6), F32),
        compiler_params=_params("parallel", "parallel"),
        name="modulation",
    )(cvecs, w_mod, b_mod.reshape(L, 1, D6))


def _rope(x, cos, sin):
    lane = lax.broadcasted_iota(jnp.int32, cos.shape, 1)
    first = (lane % 32) < 16
    outs = []
    for j in range(x.shape[1] // LANES):
        xb = x[:, j * LANES:(j + 1) * LANES]
        partner = jnp.where(first, pltpu.roll(xb, LANES - 16, 1), pltpu.roll(xb, 16, 1))
        outs.append(xb * cos + partner * sin)
    return outs[0] if len(outs) == 1 else jnp.concatenate(outs, axis=1)


INPROJ_TT = 32
INPROJ_PITCH = INPROJ_TT + 8


def _inproj_kernel(*refs, rope, offs):
    if rope:
        (x_ref, mod_ref, g_ref, w_ref, cos_ref, sin_ref,
         a_ref, uv_ref, q_ref, k_ref, v_ref, gt_ref, a_scr) = refs
    else:
        x_ref, mod_ref, g_ref, w_ref, a_ref, uv_ref, q_ref, k_ref, v_ref, gt_ref, a_scr = refs
    ob, oq, ok, ov, og, dn = offs
    bs, tt, D = x_ref.shape
    rows = bs * tt
    m = mod_ref[...]
    x = x_ref[...]
    h = x * lax.rsqrt(jnp.mean(x * x, axis=-1, keepdims=True) + EPS) * g_ref[...]
    h = h * (1.0 + m[:, 1:2, :]) + m[:, 0:1, :]
    hb = h.reshape(rows, D).astype(BF16)

    a = _dot(hb, w_ref[:, 0:ob])
    for b in range(bs):
        for j in range(ob // LANES):
            a_scr[j, b * INPROJ_PITCH:b * INPROJ_PITCH + tt, :] = a[b * tt:(b + 1) * tt, j * LANES:(j + 1) * LANES]
    for t in range(tt):
        for j in range(ob // LANES):
            a_ref[t // 2, t % 2, :, j * LANES:(j + 1) * LANES] = a_scr[j, pl.ds(t, bs, stride=INPROJ_PITCH), :]

    uv_ref[...] = _dot(hb, w_ref[:, ob:oq]).astype(BF16).reshape(bs, tt, oq - ob)
    q = _dot(hb, w_ref[:, oq:ok])
    k = _dot(hb, w_ref[:, ok:ov])
    if rope:
        cos = jnp.concatenate([cos_ref[...]] * bs, axis=0)
        sin = jnp.concatenate([sin_ref[...]] * bs, axis=0)
        q = _rope(q, cos, sin)
        k = _rope(k, cos, sin)
    q_ref[...] = q.reshape(bs, tt, ok - oq)
    k_ref[...] = k.reshape(bs, tt, ov - ok)
    v_ref[...] = _dot(hb, w_ref[:, ov:og]).reshape(bs, tt, og - ov)
    gt_ref[...] = _dot(hb, w_ref[:, og:dn]).astype(BF16).reshape(bs, tt, dn - og)


def _in_proj(x2, mod, g_pre, w_in, B, T, dims, rope_tabs=None):
    R, D = x2.shape
    d_a, d_b2, d_c, kv_w, d_g = dims
    ob = d_a
    oq = ob + d_b2
    ok = oq + d_c
    ov = ok + kv_w
    og = ov + kv_w
    dn = og + d_g
    bs, tt = SUBLANES, INPROJ_TT
    bm = mod.shape[0]
    mod_spec = (pl.BlockSpec((bs, 8, D), lambda g, i: (g, 0, 0)) if bm > 1
                else pl.BlockSpec((1, 8, D), lambda g, i: (0, 0, 0)))
    in_specs = [pl.BlockSpec((bs, tt, D), lambda g, i: (g, i, 0)),
                mod_spec,
                pl.BlockSpec((1, D), lambda g, i: (0, 0)),
                pl.BlockSpec((D, dn), lambda g, i: (0, 0), pipeline_mode=pl.Buffered(1))]
    args = [x2.reshape(B, T, D), mod, g_pre, w_in]
    if rope_tabs is not None:
        in_specs += [pl.BlockSpec((tt, LANES), lambda g, i: (i, 0))] * 2
        args += list(rope_tabs)
    seq = lambda w: pl.BlockSpec((bs, tt, w), lambda g, i: (g, i, 0))
    out_specs = [pl.BlockSpec((tt // 2, 2, bs, d_a), lambda g, i: (i, 0, g, 0)),
                 seq(d_b2), seq(d_c), seq(kv_w), seq(kv_w), seq(d_g)]
    out_shape = [jax.ShapeDtypeStruct((T // 2, 2, B, d_a), F32),
                 jax.ShapeDtypeStruct((B, T, d_b2), BF16),
                 jax.ShapeDtypeStruct((B, T, d_c), F32),
                 jax.ShapeDtypeStruct((B, T, kv_w), F32),
                 jax.ShapeDtypeStruct((B, T, kv_w), F32),
                 jax.ShapeDtypeStruct((B, T, d_g), BF16)]
    a4, uv, q, k, v, gates = pl.pallas_call(
        functools.partial(_inproj_kernel, rope=rope_tabs is not None, offs=(ob, oq, ok, ov, og, dn)),
        grid=(B // bs, T // tt), in_specs=in_specs, out_specs=out_specs, out_shape=out_shape,
        scratch_shapes=[pltpu.VMEM((d_a // LANES, bs * INPROJ_PITCH, LANES), F32)],
        compiler_params=_params("parallel", "parallel"), name="in_proj",
    )(*args)
    flat = lambda y: y.reshape(R, y.shape[-1])
    return a4, flat(uv), flat(q), flat(k), flat(v), flat(gates)


def _s5_disc_kernel(lr_ref, li_ref, ls_ref, br_ref, bi_ref, cr_ref, ci_ref,
                    l2r_ref, l2i_ref, bbr_ref, bbi_ref, lbbr_ref, lbbi_ref,
                    c1r_ref, c1i_ref, c2r_ref, c2i_ref, cb_ref, clb_ref):
    lr = lr_ref[...]
    li = li_ref[...]
    dt = jnp.exp(ls_ref[...])
    ar = lr * dt
    ai = li * dt
    mag = jnp.exp(ar)
    lb_re = (mag * jnp.cos(ai))[:, None, :]
    lb_im = (mag * jnp.sin(ai))[:, None, :]
    lr3 = lr[:, None, :]
    li3 = li[:, None, :]
    den = lr3 * lr3 + li3 * li3
    f_re = ((lb_re - 1.0) * lr3 + lb_im * li3) / den
    f_im = (lb_im * lr3 - (lb_re - 1.0) * li3) / den
    l2_re = lb_re * lb_re - lb_im * lb_im
    l2_im = 2.0 * lb_re * lb_im
    l2r_ref[...] = l2_re
    l2i_ref[...] = l2_im
    br = br_ref[...]
    bi = bi_ref[...]
    bb_re = f_re * br - f_im * bi
    bb_im = f_re * bi + f_im * br
    lbb_re = lb_re * bb_re - lb_im * bb_im
    lbb_im = lb_re * bb_im + lb_im * bb_re
    bbr_ref[...] = bb_re
    bbi_ref[...] = bb_im
    lbbr_ref[...] = lbb_re
    lbbi_ref[...] = lbb_im
    cr = cr_ref[...]
    ci = ci_ref[...]
    c1r_ref[...] = cr * lb_re - ci * lb_im
    c1i_ref[...] = cr * lb_im + ci * lb_re
    c2r_ref[...] = cr * l2_re - ci * l2_im
    c2i_ref[...] = cr * l2_im + ci * l2_re
    for q in range(br.shape[1]):
        cb_ref[:, q, :] = jnp.sum(cr * bb_re[:, q:q + 1, :] - ci * bb_im[:, q:q + 1, :], axis=-1)
        clb_ref[:, q, :] = jnp.sum(cr * lbb_re[:, q:q + 1, :] - ci * lbb_im[:, q:q + 1, :], axis=-1)


def _s5_discretise(lam_re, lam_im, log_step, b_re, b_im, c_re, c_im):
    N = lam_re.shape[-1]
    P = b_re.shape[-1]
    rows = lam_re.size // N
    tr = lambda b: jnp.swapaxes(b, -1, -2).reshape(rows, P, N)
    vec = jax.ShapeDtypeStruct((rows, 1, N), F32)
    mat = jax.ShapeDtypeStruct((rows, P, N), F32)
    sq = jax.ShapeDtypeStruct((rows, P, P), F32)
    return pl.pallas_call(_s5_disc_kernel, out_shape=[vec, vec] + [mat] * 8 + [sq, sq], name="s5_discretise")(
        lam_re.reshape(rows, N), lam_im.reshape(rows, N), log_step.reshape(rows, 1), tr(b_re), tr(b_im),
        c_re.reshape(rows, P, N), c_im.reshape(rows, P, N))


def _s5_tables(disc, G, N, P):
    l2_re, l2_im, bb_re, bb_im, lbb_re, lbb_im, c1_re, c1_im, c2_re, c2_im, cb, clb = disc
    GB = LANES // P
    NB = G // GB
    X = l2_re.shape[0] // G
    eye = jnp.eye(GB, dtype=F32)

    def w_in(x):
        return jnp.einsum('xkgpn,gh->xkgphn', x.reshape(X, NB, GB, P, N), eye).reshape(X, NB, GB * P, GB * N)

    def w_out(x):
        return jnp.einsum('xkgpn,gh->xkgnhp', x.reshape(X, NB, GB, P, N), eye).reshape(X, NB, GB * N, GB * P)

    def w_dir(x):
        return jnp.einsum('xkgqp,gh->xkgqhp', x.reshape(X, NB, GB, P, P), eye).reshape(X, NB, GB * P, GB * P)

    w_x = jnp.concatenate([jnp.concatenate([w_in(lbb_re), w_in(lbb_im)], axis=-1),
                           jnp.concatenate([w_in(bb_re), w_in(bb_im)], axis=-1)], axis=2).astype(BF16)
    w_y = jnp.concatenate([jnp.concatenate([w_out(c1_re), -w_out(c1_im)], axis=2),
                           jnp.concatenate([w_out(c2_re), -w_out(c2_im)], axis=2)], axis=-1).astype(BF16)
    d0, d1 = w_dir(cb), w_dir(clb)
    w_u = jnp.concatenate([jnp.concatenate([d0, d1], axis=-1),
                           jnp.concatenate([jnp.zeros_like(d0), d0], axis=-1)], axis=2).astype(BF16)
    lam2 = jnp.concatenate([l2_re.reshape(X, NB, 1, GB * N), l2_im.reshape(X, NB, 1, GB * N)], axis=-1)
    return w_x, w_y, w_u, lam2


def _s5_kernel(*refs, reverse, has_h0, fuse, nb, sw):
    it = iter(refs)
    u_ref, wx_ref, wy_ref, wu_ref, lam_ref = next(it), next(it), next(it), next(it), next(it)
    if has_h0:
        h0r_ref, h0i_ref = next(it), next(it)
    if fuse:
        yb_ref, dsk_ref = next(it), next(it)
    y_ref, fr_ref, fi_ref = next(it), next(it), next(it)
    xs_ref, st_ref = next(it), next(it)
    if fuse:
        yacc_ref = next(it)

    tc, _, bs, _ = u_ref.shape
    rows = tc * bs
    c = pl.program_id(1)
    first, second = (1, 0) if reverse else (0, 1)

    @pl.when(c == 0)
    def _():
        for j in range(nb):
            if has_h0:
                st_ref[j, :, 0:sw] = h0r_ref[:, j * sw:(j + 1) * sw]
                st_ref[j, :, sw:2 * sw] = h0i_ref[:, j * sw:(j + 1) * sw]
            else:
                st_ref[j] = jnp.zeros(st_ref.shape[1:], F32)

    for j in range(nb):
        cs = slice(j * LANES, (j + 1) * LANES)
        u2 = jnp.concatenate([u_ref[:, first, :, cs].reshape(rows, LANES),
                              u_ref[:, second, :, cs].reshape(rows, LANES)], axis=1).astype(BF16)
        xs_ref[...] = _dot(u2, wx_ref[j]).reshape(tc, bs, 2 * sw)
        lr = jnp.broadcast_to(lam_ref[j, :, 0:sw], (bs, sw))
        li = jnp.broadcast_to(lam_ref[j, :, sw:2 * sw], (bs, sw))

        def step(i, carry, lr=lr, li=li):
            s_re, s_im = carry
            t = (tc - 1 - i) if reverse else i
            x_re = xs_ref[t, :, 0:sw]
            x_im = xs_ref[t, :, sw:2 * sw]
            xs_ref[t, :, 0:sw] = s_re
            xs_ref[t, :, sw:2 * sw] = s_im
            return lr * s_re - li * s_im + x_re, lr * s_im + li * s_re + x_im

        s_re, s_im = lax.fori_loop(0, tc, step, (st_ref[j, :, 0:sw], st_ref[j, :, sw:2 * sw]), unroll=8)
        st_ref[j, :, 0:sw] = s_re
        st_ref[j, :, sw:2 * sw] = s_im
        fr_ref[:, j * sw:(j + 1) * sw] = s_re
        fi_ref[:, j * sw:(j + 1) * sw] = s_im
        yj = _dot(xs_ref[...].reshape(rows, 2 * sw).astype(BF16), wy_ref[j]) + _dot(u2, wu_ref[j])
        for half, parity in ((0, first), (1, second)):
            yh = yj[:, half * LANES:(half + 1) * LANES]
            if fuse:
                yacc_ref[parity, :, cs] = yh
            else:
                y_ref[:, parity, :, cs] = yh.reshape(tc, bs, LANES)

    if fuse:
        da = nb * LANES
        for parity in range(2):
            y = (yacc_ref[parity] + yb_ref[:, parity].reshape(rows, da)
                 + dsk_ref[...] * u_ref[:, parity].reshape(rows, da))
            y_ref[:, parity] = jax.nn.gelu(y).reshape(tc, bs, da)


def _s5_pass(u4, tabs, B, T, reverse, h0=None, fuse=None):
    (w_x, w_y, w_u, lam2), which = tabs
    _, nb, _, sw2 = w_x.shape
    sw = sw2 // 2
    da = nb * LANES
    bs = SUBLANES
    tc = 64
    nct = T // (2 * tc)
    tmap = (lambda g, c: (nct - 1 - c, 0, g, 0)) if reverse else (lambda g, c: (c, 0, g, 0))
    pick = lambda g, c: (which, 0, 0, 0)
    blk = pl.BlockSpec((tc, 2, bs, da), tmap)
    in_specs = [blk] + [pl.BlockSpec((None,) + a.shape[1:], pick) for a in (w_x, w_y, w_u, lam2)]
    args = [u4, w_x, w_y, w_u, lam2]
    if h0 is not None:
        in_specs += [pl.BlockSpec((bs, nb * sw), lambda g, c: (g, 0))] * 2
        args += list(h0)
    scratch = [pltpu.VMEM((tc, bs, sw2), F32), pltpu.VMEM((nb, bs, sw2), F32)]
    if fuse is not None:
        y_other, d_skip = fuse
        in_specs += [blk, pl.BlockSpec((1, da), lambda g, c: (0, 0))]
        args += [y_other, d_skip]
        scratch.append(pltpu.VMEM((2, tc * bs, da), F32))
    fin = jax.ShapeDtypeStruct((B, nb * sw), F32)
    y, f_re, f_im = pl.pallas_call(
        functools.partial(_s5_kernel, reverse=reverse, has_h0=h0 is not None, fuse=fuse is not None, nb=nb, sw=sw),
        grid=(B // bs, nct), in_specs=in_specs,
        out_specs=[blk,
                   pl.BlockSpec((bs, nb * sw), lambda g, c: (g, 0)),
                   pl.BlockSpec((bs, nb * sw), lambda g, c: (g, 0))],
        out_shape=[jax.ShapeDtypeStruct((T // 2, 2, B, da), F32), fin, fin],
        scratch_shapes=scratch,
        compiler_params=_params("parallel", "arbitrary"),
        name="s5_scan_gelu" if fuse is not None else "s5_scan",
    )(*args)
    return y, f_re, f_im


def _sgu_kernel(uv_ref, g_ref, ws_ref, bias_ref, o_ref, *, nch):
    db = g_ref.shape[1]
    x = jax.nn.gelu(uv_ref[...].astype(F32))
    u = x[:, :db]
    v = x[:, db:]
    mu = jnp.mean(v, axis=-1, keepdims=True)
    vc = v - mu
    var = jnp.mean(vc * vc, axis=-1, keepdims=True)
    vn = (vc * lax.rsqrt(var + EPS) * g_ref[...]).astype(BF16)
    lane = lax.broadcasted_iota(jnp.int32, (CHUNK, LANES), 1)
    lo = lane < (LANES // 2)
    bias = bias_ref[...]
    for c in range(nch):
        r0 = c * CHUNK
        parts = []
        for hp in range(db // LANES):
            vb = vn[r0:r0 + CHUNK, hp * LANES:(hp + 1) * LANES]
            parts.append(jnp.where(lo, _dot(ws_ref[2 * hp], vb), _dot(ws_ref[2 * hp + 1], vb)))
        mixed = jnp.concatenate(parts, axis=1) + bias
        o_ref[r0:r0 + CHUNK, :] = (u[r0:r0 + CHUNK] * mixed).astype(BF16)


def _sgu(uv, g_sgu, w_spatial, bias_full):
    R, db2 = uv.shape
    db = db2 // 2
    nch = 4
    tm = nch * CHUNK
    return pl.pallas_call(
        functools.partial(_sgu_kernel, nch=nch),
        grid=(R // tm,),
        in_specs=[pl.BlockSpec((tm, db2), lambda i: (i, 0)),
                  pl.BlockSpec((1, db), lambda i: (0, 0)),
                  pl.BlockSpec(w_spatial.shape, lambda i: (0, 0, 0)),
                  pl.BlockSpec(bias_full.shape, lambda i: (0, 0))],
        out_specs=pl.BlockSpec((tm, db), lambda i: (i, 0)),
        out_shape=jax.ShapeDtypeStruct((R, db), BF16),
        compiler_params=_params("parallel"), name="chunk_sgu",
    )(uv, g_sgu, w_spatial, bias_full)


def _attend(q, kcat, vcat, sink_ref, bias, o_ref):
    assert N_KV * HEAD_DIM == LANES and kcat.shape[1] == LANES
    M = q.shape[0]
    scale = HEAD_DIM ** -0.5
    lane = lax.broadcasted_iota(jnp.int32, (M, LANES), 1)
    lo = lane < HEAD_DIM
    heads_per_kv = N_HEADS // N_KV
    rows = []
    for blk in range(N_HEADS * HEAD_DIM // LANES):
        kv = (2 * blk) // heads_per_kv
        qb = q[:, blk * LANES:(blk + 1) * LANES] * scale
        qr = pltpu.roll(qb, HEAD_DIM, 1)
        if kv == 0:
            rows += [jnp.where(lo, qb, 0.0), jnp.where(lo, qr, 0.0)]
        else:
            rows += [jnp.where(lo, 0.0, qr), jnp.where(lo, 0.0, qb)]
    qst = jnp.concatenate(rows, axis=0).astype(BF16)
    s = lax.dot_general(qst, kcat.astype(BF16), (((1,), (1,)), ((), ())), preferred_element_type=F32)
    if bias is not None:
        cols, at = [], 0
        for off, blk_bias in bias:
            if off > at:
                cols.append(s[:, at:off])
            cols.append(s[:, off:off + LANES] + jnp.concatenate([blk_bias] * N_HEADS, axis=0))
            at = off + LANES
        cols.append(s[:, at:])
        s = jnp.concatenate(cols, axis=1)
    ps, es = [], []
    for h in range(N_HEADS):
        sh = s[h * M:(h + 1) * M]
        mh = jnp.maximum(jnp.max(sh, axis=-1, keepdims=True), sink_ref[h])
        ps.append(jnp.exp(sh - mh).astype(BF16))
        es.append(jnp.exp(sink_ref[h] - mh))
    vaug = jnp.concatenate([vcat.astype(BF16), jnp.ones(vcat.shape, BF16)], axis=1)
    oa = _dot(jnp.concatenate(ps, axis=0), vaug)
    o = [oa[h * M:(h + 1) * M, :LANES] * (1.0 / (oa[h * M:(h + 1) * M, LANES:] + es[h])) for h in range(N_HEADS)]
    for blk in range(N_HEADS * HEAD_DIM // LANES):
        kv = (2 * blk) // heads_per_kv
        oa_, ob_ = o[2 * blk], o[2 * blk + 1]
        if kv == 0:
            out = jnp.where(lo, oa_, pltpu.roll(ob_, HEAD_DIM, 1))
        else:
            out = jnp.where(lo, pltpu.roll(oa_, HEAD_DIM, 1), ob_)
        o_ref[:, blk * LANES:(blk + 1) * LANES] = out.astype(o_ref.dtype)


def _ctx_attn_kernel(sink_ref, q_ref, k_ref, v_ref, o_ref):
    _attend(q_ref[...], k_ref[...], v_ref[...], sink_ref, None, o_ref)


def _context_attention(q, k, v, sink, B, T):
    R, dc = q.shape
    kvw = k.shape[1]
    nq = T // BLOCK
    return pl.pallas_call(
        _ctx_attn_kernel,
        grid=(B, nq),
        in_specs=[pl.BlockSpec(memory_space=pltpu.SMEM),
                  pl.BlockSpec((BLOCK, dc), lambda b, i: (b * nq + i, 0)),
                  pl.BlockSpec((T, kvw), lambda b, i: (b, 0)),
                  pl.BlockSpec((T, kvw), lambda b, i: (b, 0))],
        out_specs=pl.BlockSpec((BLOCK, dc), lambda b, i: (b * nq + i, 0)),
        out_shape=jax.ShapeDtypeStruct((R, dc), BF16),
        compiler_params=_params("parallel", "parallel"), name="context_attention",
    )(sink, q, k, v)


def _win_attn_kernel(sink_ref, q_ref, kp_ref, kc_ref, kn_ref, vp_ref, vc_ref, vn_ref, ck_ref, cv_ref, o_ref, *, nblk):
    i = pl.program_id(1)
    k_blocks = [kp_ref[...], kc_ref[0:BLOCK], kc_ref[BLOCK:2 * BLOCK], kn_ref[...]]
    v_blocks = [vp_ref[...], vc_ref[0:BLOCK], vc_ref[BLOCK:2 * BLOCK], vn_ref[...]]
    qi = lax.broadcasted_iota(jnp.int32, (BLOCK, BLOCK), 0)
    kj = lax.broadcasted_iota(jnp.int32, (BLOCK, BLOCK), 1)
    edge = [jnp.where(i > 0, 0, BLOCK), 0, 0, jnp.where(2 * i + 2 < nblk, 0, BLOCK)]
    for half in range(2):
        kcat = jnp.concatenate(k_blocks[half:half + 3] + [ck_ref[...]], axis=0)
        vcat = jnp.concatenate(v_blocks[half:half + 3] + [cv_ref[...]], axis=0)
        prev_bias = jnp.where(qi + edge[half] <= kj, 0.0, -jnp.inf)
        next_bias = jnp.where(kj + edge[half + 2] <= qi, 0.0, -jnp.inf)
        _attend(q_ref[half * BLOCK:(half + 1) * BLOCK, :], kcat, vcat, sink_ref,
                [(0, prev_bias), (2 * BLOCK, next_bias)], o_ref.at[half * BLOCK:(half + 1) * BLOCK, :])


def _window_attention(q, k, v, cache_k4, cache_v4, layer, sink, B, T):
    R, dc = q.shape
    kvw = k.shape[1]
    nblk = T // BLOCK
    past = cache_k4.shape[2]
    npair = nblk // 2
    assert nblk == 2 * npair
    cur = lambda b, i: (b * npair + i, 0)
    prv = lambda b, i: (b * nblk + jnp.maximum(2 * i - 1, 0), 0)
    nxt = lambda b, i: (b * nblk + jnp.minimum(2 * i + 2, nblk - 1), 0)
    kvs = lambda f: pl.BlockSpec((BLOCK, kvw), f)
    kv2 = pl.BlockSpec((2 * BLOCK, kvw), cur)
    ctx = pl.BlockSpec((None, None, past, kvw), lambda b, i: (b, layer, 0, 0))
    return pl.pallas_call(
        functools.partial(_win_attn_kernel, nblk=nblk),
        grid=(B, npair),
        in_specs=[pl.BlockSpec(memory_space=pltpu.SMEM),
                  pl.BlockSpec((2 * BLOCK, dc), cur),
                  kvs(prv), kv2, kvs(nxt), kvs(prv), kv2, kvs(nxt), ctx, ctx],
        out_specs=pl.BlockSpec((2 * BLOCK, dc), cur),
        out_shape=jax.ShapeDtypeStruct((R, dc), BF16),
        compiler_params=_params("parallel", "parallel"), name="window_attention",
    )(sink, q, k, k, k, v, v, v, cache_k4, cache_v4)


def _merge_kernel(x_ref, mod_ref, gt_ref, ga_ref, pb_ref, ao_ref, wglu_ref, wb_ref, wc_ref, wo_ref, g_ref, o_ref):
    dm = x_ref.shape[1]
    z = _dot(ga_ref[...].astype(BF16), wglu_ref[...])
    ya = z[:, :dm] * jax.nn.sigmoid(z[:, dm:])
    yb = _dot(pb_ref[...], wb_ref[...])
    yc = _dot(ao_ref[...], wc_ref[...])
    gts = jax.nn.sigmoid(gt_ref[...].astype(F32))
    merged = gts[:, 0:dm] * ya + gts[:, dm:2 * dm] * yb + gts[:, 2 * dm:3 * dm] * yc
    mo = _dot(merged.astype(BF16), wo_ref[...])
    m = mod_ref[...]
    o_ref[...] = x_ref[...] + m[2:3] * _rms(mo, g_ref[...])


def _merge(x2, mod, gates, ga_t, pb, ao, w_glu, w_b_out, w_c_out, w_o, g_post, B, T):
    R, D = x2.shape
    tm = min(512, T)
    nt = T // tm
    bm = mod.shape[0]
    da = w_glu.shape[0]
    mod_map = (lambda i: (i // nt, 0, 0)) if bm > 1 else (lambda i: (0, 0, 0))
    row = lambda w: pl.BlockSpec((tm, w), lambda i: (i, 0))
    whole = lambda a: pl.BlockSpec(a.shape, lambda i: (0, 0), pipeline_mode=pl.Buffered(1))
    return pl.pallas_call(
        _merge_kernel,
        grid=(R // tm,),
        in_specs=[row(D), pl.BlockSpec((None, 8, D), mod_map), row(gates.shape[1]),
                  pl.BlockSpec((tm, da), lambda i: (i % nt, i // nt)),
                  row(pb.shape[1]), row(ao.shape[1]),
                  whole(w_glu), whole(w_b_out), whole(w_c_out), whole(w_o),
                  pl.BlockSpec((1, D), lambda i: (0, 0))],
        out_specs=row(D),
        out_shape=jax.ShapeDtypeStruct((R, D), F32),
        compiler_params=_params("parallel"), name="merge_out_proj",
    )(x2, mod, gates, ga_t, pb, ao, w_glu, w_b_out, w_c_out, w_o, g_post)


def _ffn_kernel(*refs, halo, nt, tf):
    if halo:
        (x_ref, xp_ref, xn_ref, mod_ref, gpre_ref, wu_ref, cw_ref, cb_ref, wd_ref, gpost_ref,
         o_ref, h_ref, z_ref, a_ref) = refs
    else:
        x_ref, mod_ref, gpre_ref, wu_ref, cw_ref, cb_ref, wd_ref, gpost_ref, o_ref, h_ref, z_ref, a_ref = refs
    i = pl.program_id(0)
    tm = x_ref.shape[0]
    dff = wd_ref.shape[0]
    m = mod_ref[...]
    norm = lambda x: _rms(x, gpre_ref[...]) * (1.0 + m[4:5]) + m[3:4]
    h_ref[HALO:HALO + tm, :] = norm(x_ref[...]).astype(BF16)
    if halo:
        keep_prev = jnp.where(i % nt == 0, 0.0, 1.0)
        keep_next = jnp.where(i % nt == nt - 1, 0.0, 1.0)
        h_ref[0:HALO, :] = (norm(xp_ref[...]) * keep_prev).astype(BF16)
        h_ref[HALO + tm:, :] = (norm(xn_ref[...]) * keep_next).astype(BF16)
    else:
        h_ref[0:HALO, :] = jnp.zeros((HALO, h_ref.shape[1]), BF16)
        h_ref[HALO + tm:, :] = jnp.zeros((HALO, h_ref.shape[1]), BF16)
    hx = h_ref[...]

    def conv(slot, col):
        z_ref[slot] = _dot(hx, wu_ref[:, col:col + tf])
        cw = cw_ref[:, col:col + tf]
        return (z_ref[slot, HALO - 1:HALO - 1 + tm, :] * cw[0:1] + z_ref[slot, HALO:HALO + tm, :] * cw[1:2]
                + z_ref[slot, HALO + 1:HALO + 1 + tm, :] * cw[2:3] + cb_ref[:, col:col + tf])

    for c in range(dff // tf):
        g = conv(2 * (c % 2), c * tf)
        val = conv(2 * (c % 2) + 1, dff + c * tf)
        a_ref[:, c * tf:(c + 1) * tf] = (jax.nn.silu(g) * val).astype(BF16)
    y = _dot(a_ref[...], wd_ref[...])
    o_ref[...] = x_ref[...] + m[5:6] * _rms(y, gpost_ref[...])


def _conv_ffn(x2, mod, g_pre, w_up, conv_w, conv_b, w_down, g_post, B, T):
    R, D = x2.shape
    dff = w_down.shape[0]
    tf = 256
    tm = min(512, T)
    nt = T // tm
    halo = nt > 1
    bm = mod.shape[0]
    mod_map = (lambda i: (i // nt, 0, 0)) if bm > 1 else (lambda i: (0, 0, 0))
    hb = tm // HALO
    nhb = R // HALO
    const = lambda i: (0, 0)
    whole = lambda a: pl.BlockSpec(a.shape, const, pipeline_mode=pl.Buffered(1))
    in_specs = [pl.BlockSpec((tm, D), lambda i: (i, 0))]
    args = [x2]
    if halo:
        in_specs += [pl.BlockSpec((HALO, D), lambda i: (jnp.maximum(i * hb - 1, 0), 0)),
                     pl.BlockSpec((HALO, D), lambda i: (jnp.minimum((i + 1) * hb, nhb - 1), 0))]
        args += [x2, x2]
    in_specs += [pl.BlockSpec((None, 8, D), mod_map), pl.BlockSpec((1, D), const),
                 whole(w_up), whole(conv_w), whole(conv_b), whole(w_down), pl.BlockSpec((1, D), const)]
    args += [mod, g_pre, w_up, conv_w, conv_b, w_down, g_post]
    return pl.pallas_call(
        functools.partial(_ffn_kernel, halo=halo, nt=nt, tf=tf),
        grid=(R // tm,),
        in_specs=in_specs,
        out_specs=pl.BlockSpec((tm, D), lambda i: (i, 0)),
        out_shape=jax.ShapeDtypeStruct((R, D), F32),
        scratch_shapes=[pltpu.VMEM((tm + 2 * HALO, D), BF16),
                        pltpu.VMEM((4, tm + 2 * HALO, tf), F32),
                        pltpu.VMEM((tm, dff), BF16)],
        compiler_params=_params("parallel"), name="conv_ffn",
    )(*args)


def _rope_tables(T):
    t = jnp.arange(T)
    row = (t // GRID_W).astype(F32)[:, None]
    col = (t % GRID_W).astype(F32)[:, None]
    half = HEAD_DIM // 2
    inv = ROPE_BASE ** (-jnp.arange(0, half, 2, dtype=F32) / half)
    lane = jnp.arange(LANES)
    freq = inv[lane % (half // 2)][None, :]
    ang = jnp.where(((lane % HEAD_DIM) < half)[None, :], row * freq, col * freq)
    sign = jnp.where((lane % half) < half // 2, -1.0, 1.0)[None, :]
    return jnp.cos(ang), jnp.sin(ang) * sign


def _layer(x2, mod, lp, B, T, ctx):
    D = x2.shape[1]
    rope_tabs = None if ctx is None else lp['rope']
    a_t, uv, q, k, v, gates = _in_proj(x2, mod, lp['g_pre_mix'], lp['w_in'], B, T, lp['dims'], rope_tabs)
    h0 = (None, None) if ctx is None else ctx[3]
    tabs, row0 = lp['s5_tabs']
    y_bwd, br_re, br_im = _s5_pass(a_t, (tabs, row0 + 1), B, T, True, h0=h0[1])
    ga_t, fw_re, fw_im = _s5_pass(a_t, (tabs, row0), B, T, False, h0=h0[0], fuse=(y_bwd, lp['d_skip']))
    pb = _sgu(uv, lp['g_sgu'], lp['w_spatial'], lp['sgu_bias'])
    if ctx is None:
        ao = _context_attention(q, k, v, lp['sink'], B, T)
    else:
        ao = _window_attention(q, k, v, ctx[0], ctx[1], ctx[2], lp['sink'], B, T)
    x2 = _merge(x2, mod, gates, ga_t.reshape(T, -1), pb, ao, lp['w_glu'], lp['w_b_out'], lp['w_c_out'],
                lp['w_o'], lp['g_post_mix'], B, T)
    x2 = _conv_ffn(x2, mod, lp['g_pre_ffn'], lp['w_up'], lp['conv_w'], lp['conv_b'], lp['w_down'],
                   lp['g_post_ffn'], B, T)
    return x2, (k, v, (fw_re, br_re), (fw_im, br_im))


def kernel(x_prompt, x_sample, cache_k, cache_v, state_ssm_re, state_ssm_im, c, c_ctx,
           w_mod, b_mod, g_pre_mix, g_post_mix, g_pre_ffn, g_post_ffn, w_in,
           lam_re, lam_im, log_step, b_re, b_im, c_re, c_im, d_skip, w_glu,
           g_sgu, w_spatial, b_spatial, w_b_out, sink, w_c_out, w_o,
           w_up, conv_w, conv_b, w_down):
    BP, TP, D = x_prompt.shape
    BS, TS, _ = x_sample.shape
    L = w_in.shape[0]
    G, N = lam_re.shape[2], lam_re.shape[3]
    P = b_re.shape[-1]
    d_a = G * P
    d_b2 = 2 * g_sgu.shape[1]
    d_c = w_c_out.shape[1]
    kv_w = N_KV * HEAD_DIM
    d_g = w_in.shape[2] - (d_a + d_b2 + d_c + 2 * kv_w)
    past = cache_k.shape[2]

    nc = 16
    cvecs = jnp.zeros((nc, D), F32).at[0].set(c_ctx).at[1:1 + BS].set(c)
    mod_all = _modulation(cvecs, w_mod, b_mod).reshape(L, nc, 6, D)
    mod_all = jnp.pad(mod_all, ((0, 0), (0, 0), (0, 2), (0, 0)))

    rope = _rope_tables(TS)
    cache_k4 = cache_k.reshape(BS, L, past, kv_w)
    cache_v4 = cache_v.reshape(BS, L, past, kv_w)

    xp = x_prompt.reshape(BP * TP, D)
    xs = x_sample.reshape(BS * TS, D)
    new_k, new_v, new_re, new_im = [], [], [], []
    s5_tabs = _s5_tables(_s5_discretise(lam_re, lam_im, log_step, b_re, b_im, c_re, c_im), G, N, P)
    for l in range(L):
        tabs = (s5_tabs, 2 * l)
        lp = {
            'dims': (d_a, d_b2, d_c, kv_w, d_g),
            'g_pre_mix': g_pre_mix[l][None], 'g_post_mix': g_post_mix[l][None],
            'g_pre_ffn': g_pre_ffn[l][None], 'g_post_ffn': g_post_ffn[l][None],
            'w_in': w_in[l].astype(BF16), 's5_tabs': tabs, 'd_skip': d_skip[l][None],
            'w_glu': w_glu[l].astype(BF16), 'g_sgu': g_sgu[l][None],
            'w_spatial': w_spatial[l].astype(BF16),
            'sgu_bias': jnp.repeat(b_spatial[l].T, g_sgu.shape[1] // b_spatial.shape[1], axis=1),
            'w_b_out': w_b_out[l].astype(BF16), 'sink': sink[l],
            'w_c_out': w_c_out[l].astype(BF16), 'w_o': w_o[l].astype(BF16),
            'w_up': w_up[l].astype(BF16), 'conv_w': conv_w[l], 'conv_b': conv_b[l][None],
            'w_down': w_down[l].astype(BF16), 'rope': rope,
        }
        xp, (k_l, v_l, f_re, f_im) = _layer(xp, mod_all[l, 0:1], lp, BP, TP, None)
        new_k.append(k_l.reshape(BP, TP, N_KV, HEAD_DIM))
        new_v.append(v_l.reshape(BP, TP, N_KV, HEAD_DIM))
        new_re.append(jnp.stack([f.reshape(BP, G, N) for f in f_re], axis=1))
        new_im.append(jnp.stack([f.reshape(BP, G, N) for f in f_im], axis=1))
        h0 = [(state_ssm_re[:, l, d].reshape(BS, G * N), state_ssm_im[:, l, d].reshape(BS, G * N)) for d in range(2)]
        xs, _ = _layer(xs, mod_all[l, 1:1 + BS], lp, BS, TS, (cache_k4, cache_v4, l, h0))
    return (xp.reshape(BP, TP, D), xs.reshape(BS, TS, D),
            jnp.stack(new_k, axis=1), jnp.stack(new_v, axis=1),
            jnp.stack(new_re, axis=1), jnp.stack(new_im, axis=1))
```

```python
import functools
import math

import jax
import jax.numpy as jnp
from jax import lax
from jax.experimental import pallas as pl
from jax.experimental.pallas import tpu as pltpu

F32 = jnp.float32
BF16 = jnp.bfloat16

GRID_W = 64
SSM_P = 16
SSM_N = 64
CHUNK = 128
N_HEADS = 8
N_KV = 2
HEAD_DIM = 64
BLOCK = 128
ROPE_BASE = 10000.0
EPS = 1e-6

LANES = 128
SUBLANES = 8
HALO = 16
VMEM_LIMIT = 56 * 1024 * 1024


def _params(*sem):
    return pltpu.CompilerParams(dimension_semantics=sem, vmem_limit_bytes=VMEM_LIMIT)


def _dot(a, b):
    return jnp.dot(a, b, preferred_element_type=F32)


def _wspec(w):
    arr, layer = w
    idx = (layer,) + (0,) * (arr.ndim - 1)
    return pl.BlockSpec((None,) + arr.shape[1:], lambda *_: idx, pipeline_mode=pl.Buffered(1))


def _rms(x, g):
    return x * lax.rsqrt(jnp.mean(x * x, axis=-1, keepdims=True) + EPS) * g


def _mod_kernel(c_ref, w_ref, b_ref, o_ref):
    a = jax.nn.silu(c_ref[...]).astype(BF16)
    o_ref[...] = _dot(a, w_ref[...].astype(BF16)) + b_ref[...]


def _modulation(cvecs, w_mod, b_mod):
    L, D, D6 = w_mod.shape
    NC = cvecs.shape[0]
    tn = D6 // 4
    return pl.pallas_call(
        _mod_kernel,
        grid=(L, D6 // tn),
        in_specs=[pl.BlockSpec((NC, D), lambda l, j: (0, 0)),
                  pl.BlockSpec((None, D, tn), lambda l, j: (l, 0, j)),
                  pl.BlockSpec((None, 1, tn), lambda l, j: (l, 0, j))],
        out_specs=pl.BlockSpec((None, NC, tn), lambda l, j: (l, 0, j)),
        out_shape=jax.ShapeDtypeStruct((L, NC, D6), F32),
        compiler_params=_params("parallel", "parallel"),
        name="modulation",
    )(cvecs, w_mod, b_mod.reshape(L, 1, D6))


def _rope(x, cos, sin):
    lane = lax.broadcasted_iota(jnp.int32, cos.shape, 1)
    first = (lane % 32) < 16
    outs = []
    for j in range(x.shape[1] // LANES):
        xb = x[:, j * LANES:(j + 1) * LANES]
        partner = jnp.where(first, pltpu.roll(xb, LANES - 16, 1), pltpu.roll(xb, 16, 1))
        outs.append(xb * cos + partner * sin)
    return outs[0] if len(outs) == 1 else jnp.concatenate(outs, axis=1)


INPROJ_TT = 32
INPROJ_PITCH = INPROJ_TT + 8


def _inproj_kernel(*refs, rope, offs):
    if rope:
        (x_ref, mod_ref, g_ref, w_ref, cos_ref, sin_ref,
         a_ref, uv_ref, q_ref, k_ref, v_ref, gt_ref, a_scr) = refs
    else:
        x_ref, mod_ref, g_ref, w_ref, a_ref, uv_ref, q_ref, k_ref, v_ref, gt_ref, a_scr = refs
    ob, oq, ok, ov, og, dn = offs
    bs, tt, D = x_ref.shape
    rows = bs * tt
    m = mod_ref[...]
    x = x_ref[...]
    h = x * lax.rsqrt(jnp.mean(x * x, axis=-1, keepdims=True) + EPS) * g_ref[...]
    h = h * (1.0 + m[:, 1:2, :]) + m[:, 0:1, :]
    hb = h.reshape(rows, D).astype(BF16)

    a = _dot(hb, w_ref[:, 0:ob])
    for b in range(bs):
        for j in range(ob // LANES):
            a_scr[j, b * INPROJ_PITCH:b * INPROJ_PITCH + tt, :] = a[b * tt:(b + 1) * tt, j * LANES:(j + 1) * LANES]
    for t in range(tt):
        for j in range(ob // LANES):
            a_ref[t // 2, t % 2, :, j * LANES:(j + 1) * LANES] = a_scr[j, pl.ds(t, bs, stride=INPROJ_PITCH), :]

    uv_ref[...] = _dot(hb, w_ref[:, ob:oq]).astype(BF16).reshape(bs, tt, oq - ob)
    q = _dot(hb, w_ref[:, oq:ok])
    k = _dot(hb, w_ref[:, ok:ov])
    if rope:
        cos = jnp.concatenate([cos_ref[...]] * bs, axis=0)
        sin = jnp.concatenate([sin_ref[...]] * bs, axis=0)
        q = _rope(q, cos, sin)
        k = _rope(k, cos, sin)
    q_ref[...] = q.reshape(bs, tt, ok - oq)
    k_ref[...] = k.reshape(bs, tt, ov - ok)
    v_ref[...] = _dot(hb, w_ref[:, ov:og]).reshape(bs, tt, og - ov)
    gt_ref[...] = _dot(hb, w_ref[:, og:dn]).astype(BF16).reshape(bs, tt, dn - og)


def _in_proj(x2, mod, g_pre, w_in, B, T, dims, rope_tabs=None):
    R, D = x2.shape
    d_a, d_b2, d_c, kv_w, d_g = dims
    ob = d_a
    oq = ob + d_b2
    ok = oq + d_c
    ov = ok + kv_w
    og = ov + kv_w
    dn = og + d_g
    bs, tt = SUBLANES, INPROJ_TT
    bm = mod.shape[0]
    mod_spec = (pl.BlockSpec((bs, 8, D), lambda g, i: (g, 0, 0)) if bm > 1
                else pl.BlockSpec((1, 8, D), lambda g, i: (0, 0, 0)))
    in_specs = [pl.BlockSpec((bs, tt, D), lambda g, i: (g, i, 0)),
                mod_spec,
                pl.BlockSpec((1, D), lambda g, i: (0, 0)),
                _wspec(w_in)]
    args = [x2.reshape(B, T, D), mod, g_pre, w_in[0]]
    if rope_tabs is not None:
        in_specs += [pl.BlockSpec((tt, LANES), lambda g, i: (i, 0))] * 2
        args += list(rope_tabs)
    seq = lambda w: pl.BlockSpec((bs, tt, w), lambda g, i: (g, i, 0))
    out_specs = [pl.BlockSpec((tt // 2, 2, bs, d_a), lambda g, i: (i, 0, g, 0)),
                 seq(d_b2), seq(d_c), seq(kv_w), seq(kv_w), seq(d_g)]
    out_shape = [jax.ShapeDtypeStruct((T // 2, 2, B, d_a), F32),
                 jax.ShapeDtypeStruct((B, T, d_b2), BF16),
                 jax.ShapeDtypeStruct((B, T, d_c), F32),
                 jax.ShapeDtypeStruct((B, T, kv_w), F32),
                 jax.ShapeDtypeStruct((B, T, kv_w), F32),
                 jax.ShapeDtypeStruct((B, T, d_g), BF16)]
    a4, uv, q, k, v, gates = pl.pallas_call(
        functools.partial(_inproj_kernel, rope=rope_tabs is not None, offs=(ob, oq, ok, ov, og, dn)),
        grid=(B // bs, T // tt), in_specs=in_specs, out_specs=out_specs, out_shape=out_shape,
        scratch_shapes=[pltpu.VMEM((d_a // LANES, bs * INPROJ_PITCH, LANES), F32)],
        compiler_params=_params("parallel", "parallel"), name="in_proj",
    )(*args)
    flat = lambda y: y.reshape(R, y.shape[-1])
    return a4, flat(uv), flat(q), flat(k), flat(v), flat(gates)


def _s5_disc_kernel(lr_ref, li_ref, ls_ref, br_ref, bi_ref, cr_ref, ci_ref,
                    l2r_ref, l2i_ref, bbr_ref, bbi_ref, lbbr_ref, lbbi_ref,
                    c1r_ref, c1i_ref, c2r_ref, c2i_ref, cb_ref, clb_ref):
    lr = lr_ref[...]
    li = li_ref[...]
    dt = jnp.exp(ls_ref[...])
    ar = lr * dt
    ai = li * dt
    mag = jnp.exp(ar)
    lb_re = (mag * jnp.cos(ai))[:, None, :]
    lb_im = (mag * jnp.sin(ai))[:, None, :]
    lr3 = lr[:, None, :]
    li3 = li[:, None, :]
    den = lr3 * lr3 + li3 * li3
    f_re = ((lb_re - 1.0) * lr3 + lb_im * li3) / den
    f_im = (lb_im * lr3 - (lb_re - 1.0) * li3) / den
    l2_re = lb_re * lb_re - lb_im * lb_im
    l2_im = 2.0 * lb_re * lb_im
    l2r_ref[...] = l2_re
    l2i_ref[...] = l2_im
    br = br_ref[...]
    bi = bi_ref[...]
    bb_re = f_re * br - f_im * bi
    bb_im = f_re * bi + f_im * br
    lbb_re = lb_re * bb_re - lb_im * bb_im
    lbb_im = lb_re * bb_im + lb_im * bb_re
    bbr_ref[...] = bb_re
    bbi_ref[...] = bb_im
    lbbr_ref[...] = lbb_re
    lbbi_ref[...] = lbb_im
    cr = cr_ref[...]
    ci = ci_ref[...]
    c1r_ref[...] = cr * lb_re - ci * lb_im
    c1i_ref[...] = cr * lb_im + ci * lb_re
    c2r_ref[...] = cr * l2_re - ci * l2_im
    c2i_ref[...] = cr * l2_im + ci * l2_re
    for q in range(br.shape[1]):
        cb_ref[:, q, :] = jnp.sum(cr * bb_re[:, q:q + 1, :] - ci * bb_im[:, q:q + 1, :], axis=-1)
        clb_ref[:, q, :] = jnp.sum(cr * lbb_re[:, q:q + 1, :] - ci * lbb_im[:, q:q + 1, :], axis=-1)


def _s5_discretise(lam_re, lam_im, log_step, b_re, b_im, c_re, c_im):
    N = lam_re.shape[-1]
    P = b_re.shape[-1]
    rows = lam_re.size // N
    tr = lambda b: jnp.swapaxes(b, -1, -2).reshape(rows, P, N)
    vec = jax.ShapeDtypeStruct((rows, 1, N), F32)
    mat = jax.ShapeDtypeStruct((rows, P, N), F32)
    sq = jax.ShapeDtypeStruct((rows, P, P), F32)
    return pl.pallas_call(_s5_disc_kernel, out_shape=[vec, vec] + [mat] * 8 + [sq, sq], name="s5_discretise")(
        lam_re.reshape(rows, N), lam_im.reshape(rows, N), log_step.reshape(rows, 1), tr(b_re), tr(b_im),
        c_re.reshape(rows, P, N), c_im.reshape(rows, P, N))


def _s5_tables(disc, G, N, P):
    l2_re, l2_im, bb_re, bb_im, lbb_re, lbb_im, c1_re, c1_im, c2_re, c2_im, cb, clb = disc
    GB = LANES // P
    NB = G // GB
    X = l2_re.shape[0] // G
    eye = jnp.eye(GB, dtype=F32)

    def w_in(x):
        return jnp.einsum('xkgpn,gh->xkgphn', x.reshape(X, NB, GB, P, N), eye).reshape(X, NB, GB * P, GB * N)

    def w_out(x):
        return jnp.einsum('xkgpn,gh->xkgnhp', x.reshape(X, NB, GB, P, N), eye).reshape(X, NB, GB * N, GB * P)

    def w_dir(x):
        return jnp.einsum('xkgqp,gh->xkgqhp', x.reshape(X, NB, GB, P, P), eye).reshape(X, NB, GB * P, GB * P)

    w_x = jnp.concatenate([jnp.concatenate([w_in(lbb_re), w_in(lbb_im)], axis=-1),
                           jnp.concatenate([w_in(bb_re), w_in(bb_im)], axis=-1)], axis=2).astype(BF16)
    w_y = jnp.concatenate([jnp.concatenate([w_out(c1_re), -w_out(c1_im)], axis=2),
                           jnp.concatenate([w_out(c2_re), -w_out(c2_im)], axis=2)], axis=-1).astype(BF16)
    d0, d1 = w_dir(cb), w_dir(clb)
    w_u = jnp.concatenate([jnp.concatenate([d0, d1], axis=-1),
                           jnp.concatenate([jnp.zeros_like(d0), d0], axis=-1)], axis=2).astype(BF16)
    lam2 = jnp.concatenate([l2_re.reshape(X, NB, 1, GB * N), l2_im.reshape(X, NB, 1, GB * N)], axis=-1)
    return w_x, w_y, w_u, lam2


def _s5_kernel(*refs, reverse, has_h0, fuse, nb, sw):
    it = iter(refs)
    u_ref, wx_ref, wy_ref, wu_ref, lam_ref = next(it), next(it), next(it), next(it), next(it)
    if has_h0:
        h0r_ref, h0i_ref = next(it), next(it)
    if fuse:
        yb_ref, dsk_ref = next(it), next(it)
    y_ref, fr_ref, fi_ref = next(it), next(it), next(it)
    xs_ref, st_ref = next(it), next(it)
    if fuse:
        yacc_ref = next(it)

    tc, _, bs, _ = u_ref.shape
    rows = tc * bs
    c = pl.program_id(1)
    first, second = (1, 0) if reverse else (0, 1)

    @pl.when(c == 0)
    def _():
        for j in range(nb):
            if has_h0:
                st_ref[j, :, 0:sw] = h0r_ref[:, j * sw:(j + 1) * sw]
                st_ref[j, :, sw:2 * sw] = h0i_ref[:, j * sw:(j + 1) * sw]
            else:
                st_ref[j] = jnp.zeros(st_ref.shape[1:], F32)

    for j in range(nb):
        cs = slice(j * LANES, (j + 1) * LANES)
        u2 = jnp.concatenate([u_ref[:, first, :, cs].reshape(rows, LANES),
                              u_ref[:, second, :, cs].reshape(rows, LANES)], axis=1).astype(BF16)
        xs_ref[j] = _dot(u2, wx_ref[j]).reshape(tc, bs, 2 * sw)
        lr = jnp.broadcast_to(lam_ref[j, :, 0:sw], (bs, sw))
        li = jnp.broadcast_to(lam_ref[j, :, sw:2 * sw], (bs, sw))
        s_re, s_im = st_ref[j, :, 0:sw], st_ref[j, :, sw:2 * sw]
        for i in range(tc):
            t = (tc - 1 - i) if reverse else i
            x_re = xs_ref[j, t, :, 0:sw]
            x_im = xs_ref[j, t, :, sw:2 * sw]
            xs_ref[j, t, :, 0:sw] = s_re
            xs_ref[j, t, :, sw:2 * sw] = s_im
            s_re, s_im = lr * s_re - li * s_im + x_re, lr * s_im + li * s_re + x_im
        st_ref[j, :, 0:sw] = s_re
        st_ref[j, :, sw:2 * sw] = s_im
        fr_ref[:, j * sw:(j + 1) * sw] = s_re
        fi_ref[:, j * sw:(j + 1) * sw] = s_im
        yj = _dot(xs_ref[j].reshape(rows, 2 * sw).astype(BF16), wy_ref[j]) + _dot(u2, wu_ref[j])
        for half, parity in ((0, first), (1, second)):
            yh = yj[:, half * LANES:(half + 1) * LANES]
            if fuse:
                yacc_ref[parity, :, cs] = yh
            else:
                y_ref[:, parity, :, cs] = yh.reshape(tc, bs, LANES)

    if fuse:
        da = nb * LANES
        for parity in range(2):
            y = (yacc_ref[parity] + yb_ref[:, parity].reshape(rows, da)
                 + dsk_ref[...] * u_ref[:, parity].reshape(rows, da))
            y_ref[:, parity] = jax.nn.gelu(y).reshape(tc, bs, da)


def _s5_pass(u4, tabs, B, T, reverse, h0=None, fuse=None):
    (w_x, w_y, w_u, lam2), which = tabs
    _, nb, _, sw2 = w_x.shape
    sw = sw2 // 2
    da = nb * LANES
    bs = SUBLANES
    tc = 64
    nct = T // (2 * tc)
    tmap = (lambda g, c: (nct - 1 - c, 0, g, 0)) if reverse else (lambda g, c: (c, 0, g, 0))
    pick = lambda g, c: (which, 0, 0, 0)
    blk = pl.BlockSpec((tc, 2, bs, da), tmap)
    in_specs = [blk] + [pl.BlockSpec((None,) + a.shape[1:], pick) for a in (w_x, w_y, w_u, lam2)]
    args = [u4, w_x, w_y, w_u, lam2]
    if h0 is not None:
        in_specs += [pl.BlockSpec((bs, nb * sw), lambda g, c: (g, 0))] * 2
        args += list(h0)
    scratch = [pltpu.VMEM((nb, tc, bs, sw2), F32), pltpu.VMEM((nb, bs, sw2), F32)]
    if fuse is not None:
        y_other, d_skip = fuse
        in_specs += [blk, pl.BlockSpec((1, da), lambda g, c: (0, 0))]
        args += [y_other, d_skip]
        scratch.append(pltpu.VMEM((2, tc * bs, da), F32))
    fin = jax.ShapeDtypeStruct((B, nb * sw), F32)
    y, f_re, f_im = pl.pallas_call(
        functools.partial(_s5_kernel, reverse=reverse, has_h0=h0 is not None, fuse=fuse is not None, nb=nb, sw=sw),
        grid=(B // bs, nct), in_specs=in_specs,
        out_specs=[blk,
                   pl.BlockSpec((bs, nb * sw), lambda g, c: (g, 0)),
                   pl.BlockSpec((bs, nb * sw), lambda g, c: (g, 0))],
        out_shape=[jax.ShapeDtypeStruct((T // 2, 2, B, da), F32), fin, fin],
        scratch_shapes=scratch,
        compiler_params=_params("parallel", "arbitrary"),
        name="s5_scan_gelu" if fuse is not None else "s5_scan",
    )(*args)
    return y, f_re, f_im


def _sgu_kernel(uv_ref, g_ref, ws_ref, bias_ref, o_ref, *, nch):
    db = g_ref.shape[1]
    x = jax.nn.gelu(uv_ref[...].astype(F32))
    u = x[:, :db]
    v = x[:, db:]
    mu = jnp.mean(v, axis=-1, keepdims=True)
    vc = v - mu
    var = jnp.mean(vc * vc, axis=-1, keepdims=True)
    vn = (vc * lax.rsqrt(var + EPS) * g_ref[...]).astype(BF16)
    lane = lax.broadcasted_iota(jnp.int32, (CHUNK, LANES), 1)
    lo = lane < (LANES // 2)
    bias = bias_ref[...]
    for c in range(nch):
        r0 = c * CHUNK
        parts = []
        for hp in range(db // LANES):
            vb = vn[r0:r0 + CHUNK, hp * LANES:(hp + 1) * LANES]
            parts.append(jnp.where(lo, _dot(ws_ref[2 * hp], vb), _dot(ws_ref[2 * hp + 1], vb)))
        mixed = jnp.concatenate(parts, axis=1) + bias
        o_ref[r0:r0 + CHUNK, :] = (u[r0:r0 + CHUNK] * mixed).astype(BF16)


def _sgu(uv, g_sgu, w_spatial, bias_full):
    R, db2 = uv.shape
    db = db2 // 2
    nch = 4
    tm = nch * CHUNK
    return pl.pallas_call(
        functools.partial(_sgu_kernel, nch=nch),
        grid=(R // tm,),
        in_specs=[pl.BlockSpec((tm, db2), lambda i: (i, 0)),
                  pl.BlockSpec((1, db), lambda i: (0, 0)),
                  _wspec(w_spatial),
                  pl.BlockSpec(bias_full.shape, lambda i: (0, 0))],
        out_specs=pl.BlockSpec((tm, db), lambda i: (i, 0)),
        out_shape=jax.ShapeDtypeStruct((R, db), BF16),
        compiler_params=_params("parallel"), name="chunk_sgu",
    )(uv, g_sgu, w_spatial[0], bias_full)


def _attend(q, kcat, vcat, sink_ref, bias, o_ref):
    assert N_KV * HEAD_DIM == LANES and kcat.shape[1] == LANES
    M = q.shape[0]
    scale = HEAD_DIM ** -0.5
    lane = lax.broadcasted_iota(jnp.int32, (M, LANES), 1)
    lo = lane < HEAD_DIM
    heads_per_kv = N_HEADS // N_KV
    rows = []
    for blk in range(N_HEADS * HEAD_DIM // LANES):
        kv = (2 * blk) // heads_per_kv
        qb = q[:, blk * LANES:(blk + 1) * LANES] * scale
        qr = pltpu.roll(qb, HEAD_DIM, 1)
        if kv == 0:
            rows += [jnp.where(lo, qb, 0.0), jnp.where(lo, qr, 0.0)]
        else:
            rows += [jnp.where(lo, 0.0, qr), jnp.where(lo, 0.0, qb)]
    qst = jnp.concatenate(rows, axis=0).astype(BF16)
    s = lax.dot_general(qst, kcat.astype(BF16), (((1,), (1,)), ((), ())), preferred_element_type=F32)
    if bias is not None:
        cols, at = [], 0
        for off, blk_bias in bias:
            if off > at:
                cols.append(s[:, at:off])
            cols.append(s[:, off:off + LANES] + jnp.concatenate([blk_bias] * N_HEADS, axis=0))
            at = off + LANES
        cols.append(s[:, at:])
        s = jnp.concatenate(cols, axis=1)
    ps, es = [], []
    for h in range(N_HEADS):
        sh = s[h * M:(h + 1) * M]
        mh = jnp.maximum(jnp.max(sh, axis=-1, keepdims=True), sink_ref[h])
        ps.append(jnp.exp(sh - mh).astype(BF16))
        es.append(jnp.exp(sink_ref[h] - mh))
    vaug = jnp.concatenate([vcat.astype(BF16), jnp.ones(vcat.shape, BF16)], axis=1)
    oa = _dot(jnp.concatenate(ps, axis=0), vaug)
    o = [oa[h * M:(h + 1) * M, :LANES] * (1.0 / (oa[h * M:(h + 1) * M, LANES:] + es[h])) for h in range(N_HEADS)]
    for blk in range(N_HEADS * HEAD_DIM // LANES):
        kv = (2 * blk) // heads_per_kv
        oa_, ob_ = o[2 * blk], o[2 * blk + 1]
        if kv == 0:
            out = jnp.where(lo, oa_, pltpu.roll(ob_, HEAD_DIM, 1))
        else:
            out = jnp.where(lo, pltpu.roll(oa_, HEAD_DIM, 1), ob_)
        o_ref[:, blk * LANES:(blk + 1) * LANES] = out.astype(o_ref.dtype)


def _ctx_attn_kernel(sink_ref, q_ref, k_ref, v_ref, o_ref):
    _attend(q_ref[...], k_ref[...], v_ref[...], sink_ref, None, o_ref)


def _context_attention(q, k, v, sink, B, T):
    R, dc = q.shape
    kvw = k.shape[1]
    nq = T // BLOCK
    return pl.pallas_call(
        _ctx_attn_kernel,
        grid=(B, nq),
        in_specs=[pl.BlockSpec(memory_space=pltpu.SMEM),
                  pl.BlockSpec((BLOCK, dc), lambda b, i: (b * nq + i, 0)),
                  pl.BlockSpec((T, kvw), lambda b, i: (b, 0)),
                  pl.BlockSpec((T, kvw), lambda b, i: (b, 0))],
        out_specs=pl.BlockSpec((BLOCK, dc), lambda b, i: (b * nq + i, 0)),
        out_shape=jax.ShapeDtypeStruct((R, dc), BF16),
        compiler_params=_params("parallel", "parallel"), name="context_attention",
    )(sink, q, k, v)


def _win_attn_kernel(sink_ref, q_ref, kp_ref, kc_ref, kn_ref, vp_ref, vc_ref, vn_ref, ck_ref, cv_ref, o_ref, *, nblk):
    i = pl.program_id(1)
    k_blocks = [kp_ref[...], kc_ref[0:BLOCK], kc_ref[BLOCK:2 * BLOCK], kn_ref[...]]
    v_blocks = [vp_ref[...], vc_ref[0:BLOCK], vc_ref[BLOCK:2 * BLOCK], vn_ref[...]]
    qi = lax.broadcasted_iota(jnp.int32, (BLOCK, BLOCK), 0)
    kj = lax.broadcasted_iota(jnp.int32, (BLOCK, BLOCK), 1)
    edge = [jnp.where(i > 0, 0, BLOCK), 0, 0, jnp.where(2 * i + 2 < nblk, 0, BLOCK)]
    for half in range(2):
        kcat = jnp.concatenate(k_blocks[half:half + 3] + [ck_ref[...]], axis=0)
        vcat = jnp.concatenate(v_blocks[half:half + 3] + [cv_ref[...]], axis=0)
        prev_bias = jnp.where(qi + edge[half] <= kj, 0.0, -jnp.inf)
        next_bias = jnp.where(kj + edge[half + 2] <= qi, 0.0, -jnp.inf)
        _attend(q_ref[half * BLOCK:(half + 1) * BLOCK, :], kcat, vcat, sink_ref,
                [(0, prev_bias), (2 * BLOCK, next_bias)], o_ref.at[half * BLOCK:(half + 1) * BLOCK, :])


def _window_attention(q, k, v, cache_k4, cache_v4, layer, sink, B, T):
    R, dc = q.shape
    kvw = k.shape[1]
    nblk = T // BLOCK
    past = cache_k4.shape[2]
    npair = nblk // 2
    assert nblk == 2 * npair
    cur = lambda b, i: (b * npair + i, 0)
    prv = lambda b, i: (b * nblk + jnp.maximum(2 * i - 1, 0), 0)
    nxt = lambda b, i: (b * nblk + jnp.minimum(2 * i + 2, nblk - 1), 0)
    kvs = lambda f: pl.BlockSpec((BLOCK, kvw), f)
    kv2 = pl.BlockSpec((2 * BLOCK, kvw), cur)
    ctx = pl.BlockSpec((None, None, past, kvw), lambda b, i: (b, layer, 0, 0))
    return pl.pallas_call(
        functools.partial(_win_attn_kernel, nblk=nblk),
        grid=(B, npair),
        in_specs=[pl.BlockSpec(memory_space=pltpu.SMEM),
                  pl.BlockSpec((2 * BLOCK, dc), cur),
                  kvs(prv), kv2, kvs(nxt), kvs(prv), kv2, kvs(nxt), ctx, ctx],
        out_specs=pl.BlockSpec((2 * BLOCK, dc), cur),
        out_shape=jax.ShapeDtypeStruct((R, dc), BF16),
        compiler_params=_params("parallel", "parallel"), name="window_attention",
    )(sink, q, k, k, k, v, v, v, cache_k4, cache_v4)


def _merge_kernel(x_ref, mod_ref, gt_ref, ga_ref, pb_ref, ao_ref, wglu_ref, wb_ref, wc_ref, wo_ref, g_ref, o_ref):
    dm = x_ref.shape[1]
    z = _dot(ga_ref[...].astype(BF16), wglu_ref[...])
    ya = z[:, :dm] * jax.nn.sigmoid(z[:, dm:])
    yb = _dot(pb_ref[...], wb_ref[...])
    yc = _dot(ao_ref[...], wc_ref[...])
    gts = jax.nn.sigmoid(gt_ref[...].astype(F32))
    merged = gts[:, 0:dm] * ya + gts[:, dm:2 * dm] * yb + gts[:, 2 * dm:3 * dm] * yc
    mo = _dot(merged.astype(BF16), wo_ref[...])
    m = mod_ref[...]
    o_ref[...] = x_ref[...] + m[2:3] * _rms(mo, g_ref[...])


def _merge(x2, mod, gates, ga_t, pb, ao, w_glu, w_b_out, w_c_out, w_o, g_post, B, T):
    R, D = x2.shape
    tm = min(512, T)
    nt = T // tm
    bm = mod.shape[0]
    da = w_glu[0].shape[1]
    mod_map = (lambda i: (i // nt, 0, 0)) if bm > 1 else (lambda i: (0, 0, 0))
    row = lambda w: pl.BlockSpec((tm, w), lambda i: (i, 0))
    return pl.pallas_call(
        _merge_kernel,
        grid=(R // tm,),
        in_specs=[row(D), pl.BlockSpec((None, 8, D), mod_map), row(gates.shape[1]),
                  pl.BlockSpec((tm, da), lambda i: (i % nt, i // nt)),
                  row(pb.shape[1]), row(ao.shape[1]),
                  _wspec(w_glu), _wspec(w_b_out), _wspec(w_c_out), _wspec(w_o),
                  pl.BlockSpec((1, D), lambda i: (0, 0))],
        out_specs=row(D),
        out_shape=jax.ShapeDtypeStruct((R, D), F32),
        compiler_params=_params("parallel"), name="merge_out_proj",
    )(x2, mod, gates, ga_t, pb, ao, w_glu[0], w_b_out[0], w_c_out[0], w_o[0], g_post)


def _ffn_kernel(*refs, halo, nt, tf):
    if halo:
        (x_ref, xp_ref, xn_ref, mod_ref, gpre_ref, wu_ref, cw_ref, cb_ref, wd_ref, gpost_ref,
         o_ref, h_ref, z_ref, a_ref) = refs
    else:
        x_ref, mod_ref, gpre_ref, wu_ref, cw_ref, cb_ref, wd_ref, gpost_ref, o_ref, h_ref, z_ref, a_ref = refs
    i = pl.program_id(0)
    tm = x_ref.shape[0]
    dff = wd_ref.shape[0]
    m = mod_ref[...]
    norm = lambda x: _rms(x, gpre_ref[...]) * (1.0 + m[4:5]) + m[3:4]
    h_ref[HALO:HALO + tm, :] = norm(x_ref[...]).astype(BF16)
    if halo:
        keep_prev = jnp.where(i % nt == 0, 0.0, 1.0)
        keep_next = jnp.where(i % nt == nt - 1, 0.0, 1.0)
        h_ref[0:HALO, :] = (norm(xp_ref[...]) * keep_prev).astype(BF16)
        h_ref[HALO + tm:, :] = (norm(xn_ref[...]) * keep_next).astype(BF16)
    else:
        h_ref[0:HALO, :] = jnp.zeros((HALO, h_ref.shape[1]), BF16)
        h_ref[HALO + tm:, :] = jnp.zeros((HALO, h_ref.shape[1]), BF16)
    hx = h_ref[...]

    def conv(slot, col):
        z_ref[slot] = _dot(hx, wu_ref[:, col:col + tf])
        cw = cw_ref[:, col:col + tf]
        return (z_ref[slot, HALO - 1:HALO - 1 + tm, :] * cw[0:1] + z_ref[slot, HALO:HALO + tm, :] * cw[1:2]
                + z_ref[slot, HALO + 1:HALO + 1 + tm, :] * cw[2:3] + cb_ref[:, col:col + tf])

    for c in range(dff // tf):
        g = conv(2 * (c % 2), c * tf)
        val = conv(2 * (c % 2) + 1, dff + c * tf)
        a_ref[:, c * tf:(c + 1) * tf] = (jax.nn.silu(g) * val).astype(BF16)
    y = _dot(a_ref[...], wd_ref[...])
    o_ref[...] = x_ref[...] + m[5:6] * _rms(y, gpost_ref[...])


def _conv_ffn(x2, mod, g_pre, w_up, conv_w, conv_b, w_down, g_post, B, T):
    R, D = x2.shape
    dff = w_down[0].shape[1]
    tf = 256
    tm = min(512, T)
    nt = T // tm
    halo = nt > 1
    bm = mod.shape[0]
    mod_map = (lambda i: (i // nt, 0, 0)) if bm > 1 else (lambda i: (0, 0, 0))
    hb = tm // HALO
    nhb = R // HALO
    const = lambda i: (0, 0)
    whole = lambda a: pl.BlockSpec(a.shape, const, pipeline_mode=pl.Buffered(1))
    in_specs = [pl.BlockSpec((tm, D), lambda i: (i, 0))]
    args = [x2]
    if halo:
        in_specs += [pl.BlockSpec((HALO, D), lambda i: (jnp.maximum(i * hb - 1, 0), 0)),
                     pl.BlockSpec((HALO, D), lambda i: (jnp.minimum((i + 1) * hb, nhb - 1), 0))]
        args += [x2, x2]
    in_specs += [pl.BlockSpec((None, 8, D), mod_map), pl.BlockSpec((1, D), const),
                 _wspec(w_up), whole(conv_w), whole(conv_b), _wspec(w_down), pl.BlockSpec((1, D), const)]
    args += [mod, g_pre, w_up[0], conv_w, conv_b, w_down[0], g_post]
    return pl.pallas_call(
        functools.partial(_ffn_kernel, halo=halo, nt=nt, tf=tf),
        grid=(R // tm,),
        in_specs=in_specs,
        out_specs=pl.BlockSpec((tm, D), lambda i: (i, 0)),
        out_shape=jax.ShapeDtypeStruct((R, D), F32),
        scratch_shapes=[pltpu.VMEM((tm + 2 * HALO, D), BF16),
                        pltpu.VMEM((4, tm + 2 * HALO, tf), F32),
                        pltpu.VMEM((tm, dff), BF16)],
        compiler_params=_params("parallel"), name="conv_ffn",
    )(*args)


def _rope_tables(T):
    t = jnp.arange(T)
    row = (t // GRID_W).astype(F32)[:, None]
    col = (t % GRID_W).astype(F32)[:, None]
    half = HEAD_DIM // 2
    inv = ROPE_BASE ** (-jnp.arange(0, half, 2, dtype=F32) / half)
    lane = jnp.arange(LANES)
    freq = inv[lane % (half // 2)][None, :]
    ang = jnp.where(((lane % HEAD_DIM) < half)[None, :], row * freq, col * freq)
    sign = jnp.where((lane % half) < half // 2, -1.0, 1.0)[None, :]
    return jnp.cos(ang), jnp.sin(ang) * sign


def _layer(x2, mod, lp, B, T, ctx):
    D = x2.shape[1]
    rope_tabs = None if ctx is None else lp['rope']
    a_t, uv, q, k, v, gates = _in_proj(x2, mod, lp['g_pre_mix'], lp['w_in'], B, T, lp['dims'], rope_tabs)
    h0 = (None, None) if ctx is None else ctx[3]
    tabs, row0 = lp['s5_tabs']
    y_bwd, br_re, br_im = _s5_pass(a_t, (tabs, row0 + 1), B, T, True, h0=h0[1])
    ga_t, fw_re, fw_im = _s5_pass(a_t, (tabs, row0), B, T, False, h0=h0[0], fuse=(y_bwd, lp['d_skip']))
    pb = _sgu(uv, lp['g_sgu'], lp['w_spatial'], lp['sgu_bias'])
    if ctx is None:
        ao = _context_attention(q, k, v, lp['sink'], B, T)
    else:
        ao = _window_attention(q, k, v, ctx[0], ctx[1], ctx[2], lp['sink'], B, T)
    x2 = _merge(x2, mod, gates, ga_t.reshape(T, -1), pb, ao, lp['w_glu'], lp['w_b_out'], lp['w_c_out'],
                lp['w_o'], lp['g_post_mix'], B, T)
    x2 = _conv_ffn(x2, mod, lp['g_pre_ffn'], lp['w_up'], lp['conv_w'], lp['conv_b'], lp['w_down'],
                   lp['g_post_ffn'], B, T)
    return x2, (k, v, (fw_re, br_re), (fw_im, br_im))


def kernel(x_prompt, x_sample, cache_k, cache_v, state_ssm_re, state_ssm_im, c, c_ctx,
           w_mod, b_mod, g_pre_mix, g_post_mix, g_pre_ffn, g_post_ffn, w_in,
           lam_re, lam_im, log_step, b_re, b_im, c_re, c_im, d_skip, w_glu,
           g_sgu, w_spatial, b_spatial, w_b_out, sink, w_c_out, w_o,
           w_up, conv_w, conv_b, w_down):
    BP, TP, D = x_prompt.shape
    BS, TS, _ = x_sample.shape
    L = w_in.shape[0]
    G, N = lam_re.shape[2], lam_re.shape[3]
    P = b_re.shape[-1]
    d_a = G * P
    d_b2 = 2 * g_sgu.shape[1]
    d_c = w_c_out.shape[1]
    kv_w = N_KV * HEAD_DIM
    d_g = w_in.shape[2] - (d_a + d_b2 + d_c + 2 * kv_w)
    past = cache_k.shape[2]

    nc = 16
    cvecs = jnp.zeros((nc, D), F32).at[0].set(c_ctx).at[1:1 + BS].set(c)
    mod_all = _modulation(cvecs, w_mod, b_mod).reshape(L, nc, 6, D)
    mod_all = jnp.pad(mod_all, ((0, 0), (0, 0), (0, 2), (0, 0)))

    rope = _rope_tables(TS)
    cache_k4 = cache_k.reshape(BS, L, past, kv_w)
    cache_v4 = cache_v.reshape(BS, L, past, kv_w)

    xp = x_prompt.reshape(BP * TP, D)
    xs = x_sample.reshape(BS * TS, D)
    new_k, new_v, new_re, new_im = [], [], [], []
    s5_tabs = _s5_tables(_s5_discretise(lam_re, lam_im, log_step, b_re, b_im, c_re, c_im), G, N, P)
    wb16 = {name: w.astype(BF16) for name, w in (
        ('w_in', w_in), ('w_glu', w_glu), ('w_spatial', w_spatial), ('w_b_out', w_b_out),
        ('w_c_out', w_c_out), ('w_o', w_o), ('w_up', w_up), ('w_down', w_down))}
    for l in range(L):
        tabs = (s5_tabs, 2 * l)
        lp = {
            'dims': (d_a, d_b2, d_c, kv_w, d_g),
            'g_pre_mix': g_pre_mix[l][None], 'g_post_mix': g_post_mix[l][None],
            'g_pre_ffn': g_pre_ffn[l][None], 'g_post_ffn': g_post_ffn[l][None],
            's5_tabs': tabs, 'd_skip': d_skip[l][None], 'g_sgu': g_sgu[l][None],
            'sgu_bias': jnp.repeat(b_spatial[l].T, g_sgu.shape[1] // b_spatial.shape[1], axis=1),
            'sink': sink[l], 'conv_w': conv_w[l], 'conv_b': conv_b[l][None], 'rope': rope,
        }
        lp.update({name: (w, l) for name, w in wb16.items()})
        xp, (k_l, v_l, f_re, f_im) = _layer(xp, mod_all[l, 0:1], lp, BP, TP, None)
        new_k.append(k_l.reshape(BP, TP, N_KV, HEAD_DIM))
        new_v.append(v_l.reshape(BP, TP, N_KV, HEAD_DIM))
        new_re.append(jnp.stack([f.reshape(BP, G, N) for f in f_re], axis=1))
        new_im.append(jnp.stack([f.reshape(BP, G, N) for f in f_im], axis=1))
        h0 = [(state_ssm_re[:, l, d].reshape(BS, G * N), state_ssm_im[:, l, d].reshape(BS, G * N)) for d in range(2)]
        xs, _ = _layer(xs, mod_all[l, 1:1 + BS], lp, BS, TS, (cache_k4, cache_v4, l, h0))
    return (xp.reshape(BP, TP, D), xs.reshape(BS, TS, D),
            jnp.stack(new_k, axis=1), jnp.stack(new_v, axis=1),
            jnp.stack(new_re, axis=1), jnp.stack(new_im, axis=1))
```

```python
import functools
import math

import jax
import jax.numpy as jnp
from jax import lax
from jax.experimental import pallas as pl
from jax.experimental.pallas import tpu as pltpu

F32 = jnp.float32
BF16 = jnp.bfloat16

GRID_W = 64
SSM_P = 16
SSM_N = 64
CHUNK = 128
N_HEADS = 8
N_KV = 2
HEAD_DIM = 64
BLOCK = 128
ROPE_BASE = 10000.0
EPS = 1e-6

LANES = 128
SUBLANES = 8
HALO = 16
VMEM_LIMIT = 56 * 1024 * 1024


def _params(*sem):
    return pltpu.CompilerParams(dimension_semantics=sem, vmem_limit_bytes=VMEM_LIMIT)


def _dot(a, b):
    return jnp.dot(a, b, preferred_element_type=F32)


def _wspec(w):
    arr, layer = w
    idx = (layer,) + (0,) * (arr.ndim - 1)
    return pl.BlockSpec((None,) + arr.shape[1:], lambda *_: idx, pipeline_mode=pl.Buffered(1))


def _rms(x, g):
    return x * lax.rsqrt(jnp.mean(x * x, axis=-1, keepdims=True) + EPS) * g


def _mod_kernel(c_ref, w_ref, b_ref, o_ref):
    a = jax.nn.silu(c_ref[...]).astype(BF16)
    o_ref[...] = _dot(a, w_ref[...].astype(BF16)) + b_ref[...]


def _modulation(cvecs, w_mod, b_mod):
    L, D, D6 = w_mod.shape
    NC = cvecs.shape[0]
    tn = D6 // 4
    return pl.pallas_call(
        _mod_kernel,
        grid=(L, D6 // tn),
        in_specs=[pl.BlockSpec((NC, D), lambda l, j: (0, 0)),
                  pl.BlockSpec((None, D, tn), lambda l, j: (l, 0, j)),
                  pl.BlockSpec((None, 1, tn), lambda l, j: (l, 0, j))],
        out_specs=pl.BlockSpec((None, NC, tn), lambda l, j: (l, 0, j)),
        out_shape=jax.ShapeDtypeStruct((L, NC, D6), F32),
        compiler_params=_params("parallel", "parallel"),
        name="modulation",
    )(cvecs, w_mod, b_mod.reshape(L, 1, D6))


def _rope(x, cos, sin):
    lane = lax.broadcasted_iota(jnp.int32, cos.shape, 1)
    first = (lane % 32) < 16
    outs = []
    for j in range(x.shape[1] // LANES):
        xb = x[:, j * LANES:(j + 1) * LANES]
        partner = jnp.where(first, pltpu.roll(xb, LANES - 16, 1), pltpu.roll(xb, 16, 1))
        outs.append(xb * cos + partner * sin)
    return outs[0] if len(outs) == 1 else jnp.concatenate(outs, axis=1)


INPROJ_TT = 32
INPROJ_PITCH = INPROJ_TT + 8


def _inproj_kernel(*refs, rope, offs):
    if rope:
        (x_ref, mod_ref, g_ref, w_ref, cos_ref, sin_ref,
         a_ref, uv_ref, q_ref, k_ref, v_ref, gt_ref, a_scr) = refs
    else:
        x_ref, mod_ref, g_ref, w_ref, a_ref, uv_ref, q_ref, k_ref, v_ref, gt_ref, a_scr = refs
    ob, oq, ok, ov, og, dn = offs
    bs, tt, D = x_ref.shape
    rows = bs * tt
    m = mod_ref[...]
    x = x_ref[...]
    h = x * lax.rsqrt(jnp.mean(x * x, axis=-1, keepdims=True) + EPS) * g_ref[...]
    h = h * (1.0 + m[:, 1:2, :]) + m[:, 0:1, :]
    hb = h.reshape(rows, D).astype(BF16)

    a = _dot(hb, w_ref[:, 0:ob])
    for b in range(bs):
        for j in range(ob // LANES):
            a_scr[j, b * INPROJ_PITCH:b * INPROJ_PITCH + tt, :] = a[b * tt:(b + 1) * tt, j * LANES:(j + 1) * LANES]
    for t in range(tt):
        for j in range(ob // LANES):
            a_ref[t // 2, t % 2, :, j * LANES:(j + 1) * LANES] = a_scr[j, pl.ds(t, bs, stride=INPROJ_PITCH), :]

    uv_ref[...] = _dot(hb, w_ref[:, ob:oq]).astype(BF16).reshape(bs, tt, oq - ob)
    q = _dot(hb, w_ref[:, oq:ok])
    k = _dot(hb, w_ref[:, ok:ov])
    if rope:
        cos = jnp.concatenate([cos_ref[...]] * bs, axis=0)
        sin = jnp.concatenate([sin_ref[...]] * bs, axis=0)
        q = _rope(q, cos, sin)
        k = _rope(k, cos, sin)
    q_ref[...] = q.reshape(bs, tt, ok - oq)
    k_ref[...] = k.reshape(bs, tt, ov - ok)
    v_ref[...] = _dot(hb, w_ref[:, ov:og]).reshape(bs, tt, og - ov)
    gt_ref[...] = _dot(hb, w_ref[:, og:dn]).astype(BF16).reshape(bs, tt, dn - og)


def _in_proj(x2, mod, g_pre, w_in, B, T, dims, rope_tabs=None):
    R, D = x2.shape
    d_a, d_b2, d_c, kv_w, d_g = dims
    ob = d_a
    oq = ob + d_b2
    ok = oq + d_c
    ov = ok + kv_w
    og = ov + kv_w
    dn = og + d_g
    bs, tt = SUBLANES, INPROJ_TT
    bm = mod.shape[0]
    mod_spec = (pl.BlockSpec((bs, 8, D), lambda g, i: (g, 0, 0)) if bm > 1
                else pl.BlockSpec((1, 8, D), lambda g, i: (0, 0, 0)))
    in_specs = [pl.BlockSpec((bs, tt, D), lambda g, i: (g, i, 0)),
                mod_spec,
                pl.BlockSpec((1, D), lambda g, i: (0, 0)),
                _wspec(w_in)]
    args = [x2.reshape(B, T, D), mod, g_pre, w_in[0]]
    if rope_tabs is not None:
        in_specs += [pl.BlockSpec((tt, LANES), lambda g, i: (i, 0))] * 2
        args += list(rope_tabs)
    seq = lambda w: pl.BlockSpec((bs, tt, w), lambda g, i: (g, i, 0))
    out_specs = [pl.BlockSpec((tt // 2, 2, bs, d_a), lambda g, i: (i, 0, g, 0)),
                 seq(d_b2), seq(d_c), seq(kv_w), seq(kv_w), seq(d_g)]
    out_shape = [jax.ShapeDtypeStruct((T // 2, 2, B, d_a), F32),
                 jax.ShapeDtypeStruct((B, T, d_b2), BF16),
                 jax.ShapeDtypeStruct((B, T, d_c), F32),
                 jax.ShapeDtypeStruct((B, T, kv_w), F32),
                 jax.ShapeDtypeStruct((B, T, kv_w), F32),
                 jax.ShapeDtypeStruct((B, T, d_g), BF16)]
    a4, uv, q, k, v, gates = pl.pallas_call(
        functools.partial(_inproj_kernel, rope=rope_tabs is not None, offs=(ob, oq, ok, ov, og, dn)),
        grid=(B // bs, T // tt), in_specs=in_specs, out_specs=out_specs, out_shape=out_shape,
        scratch_shapes=[pltpu.VMEM((d_a // LANES, bs * INPROJ_PITCH, LANES), F32)],
        compiler_params=_params("parallel", "parallel"), name="in_proj",
    )(*args)
    flat = lambda y: y.reshape(R, y.shape[-1])
    return a4, flat(uv), flat(q), flat(k), flat(v), flat(gates)


def _s5_disc_kernel(lr_ref, li_ref, ls_ref, br_ref, bi_ref, cr_ref, ci_ref,
                    l2r_ref, l2i_ref, bbr_ref, bbi_ref, lbbr_ref, lbbi_ref,
                    c1r_ref, c1i_ref, c2r_ref, c2i_ref, cb_ref, clb_ref):
    lr = lr_ref[...]
    li = li_ref[...]
    dt = jnp.exp(ls_ref[...])
    ar = lr * dt
    ai = li * dt
    mag = jnp.exp(ar)
    lb_re = (mag * jnp.cos(ai))[:, None, :]
    lb_im = (mag * jnp.sin(ai))[:, None, :]
    lr3 = lr[:, None, :]
    li3 = li[:, None, :]
    den = lr3 * lr3 + li3 * li3
    f_re = ((lb_re - 1.0) * lr3 + lb_im * li3) / den
    f_im = (lb_im * lr3 - (lb_re - 1.0) * li3) / den
    l2_re = lb_re * lb_re - lb_im * lb_im
    l2_im = 2.0 * lb_re * lb_im
    l2r_ref[...] = l2_re
    l2i_ref[...] = l2_im
    br = br_ref[...]
    bi = bi_ref[...]
    bb_re = f_re * br - f_im * bi
    bb_im = f_re * bi + f_im * br
    lbb_re = lb_re * bb_re - lb_im * bb_im
    lbb_im = lb_re * bb_im + lb_im * bb_re
    bbr_ref[...] = bb_re
    bbi_ref[...] = bb_im
    lbbr_ref[...] = lbb_re
    lbbi_ref[...] = lbb_im
    cr = cr_ref[...]
    ci = ci_ref[...]
    c1r_ref[...] = cr * lb_re - ci * lb_im
    c1i_ref[...] = cr * lb_im + ci * lb_re
    c2r_ref[...] = cr * l2_re - ci * l2_im
    c2i_ref[...] = cr * l2_im + ci * l2_re
    for q in range(br.shape[1]):
        cb_ref[:, q, :] = jnp.sum(cr * bb_re[:, q:q + 1, :] - ci * bb_im[:, q:q + 1, :], axis=-1)
        clb_ref[:, q, :] = jnp.sum(cr * lbb_re[:, q:q + 1, :] - ci * lbb_im[:, q:q + 1, :], axis=-1)


def _s5_discretise(lam_re, lam_im, log_step, b_re, b_im, c_re, c_im):
    N = lam_re.shape[-1]
    P = b_re.shape[-1]
    rows = lam_re.size // N
    tr = lambda b: jnp.swapaxes(b, -1, -2).reshape(rows, P, N)
    vec = jax.ShapeDtypeStruct((rows, 1, N), F32)
    mat = jax.ShapeDtypeStruct((rows, P, N), F32)
    sq = jax.ShapeDtypeStruct((rows, P, P), F32)
    return pl.pallas_call(_s5_disc_kernel, out_shape=[vec, vec] + [mat] * 8 + [sq, sq], name="s5_discretise")(
        lam_re.reshape(rows, N), lam_im.reshape(rows, N), log_step.reshape(rows, 1), tr(b_re), tr(b_im),
        c_re.reshape(rows, P, N), c_im.reshape(rows, P, N))


def _s5_tables(disc, G, N, P):
    l2_re, l2_im, bb_re, bb_im, lbb_re, lbb_im, c1_re, c1_im, c2_re, c2_im, cb, clb = disc
    GB = LANES // P
    NB = G // GB
    X = l2_re.shape[0] // G
    eye = jnp.eye(GB, dtype=F32)

    def w_in(x):
        return jnp.einsum('xkgpn,gh->xkgphn', x.reshape(X, NB, GB, P, N), eye).reshape(X, NB, GB * P, GB * N)

    def w_out(x):
        return jnp.einsum('xkgpn,gh->xkgnhp', x.reshape(X, NB, GB, P, N), eye).reshape(X, NB, GB * N, GB * P)

    def w_dir(x):
        return jnp.einsum('xkgqp,gh->xkgqhp', x.reshape(X, NB, GB, P, P), eye).reshape(X, NB, GB * P, GB * P)

    w_x = jnp.concatenate([jnp.concatenate([w_in(lbb_re), w_in(lbb_im)], axis=-1),
                           jnp.concatenate([w_in(bb_re), w_in(bb_im)], axis=-1)], axis=2).astype(BF16)
    w_y = jnp.concatenate([jnp.concatenate([w_out(c1_re), -w_out(c1_im)], axis=2),
                           jnp.concatenate([w_out(c2_re), -w_out(c2_im)], axis=2)], axis=-1).astype(BF16)
    d0, d1 = w_dir(cb), w_dir(clb)
    w_u = jnp.concatenate([jnp.concatenate([d0, d1], axis=-1),
                           jnp.concatenate([jnp.zeros_like(d0), d0], axis=-1)], axis=2).astype(BF16)
    lam2 = jnp.concatenate([l2_re.reshape(X, NB, 1, GB * N), l2_im.reshape(X, NB, 1, GB * N)], axis=-1)
    return w_x, w_y, w_u, lam2


def _s5_kernel(*refs, reverse, has_h0, fuse, nb, sw):
    it = iter(refs)
    u_ref, wx_ref, wy_ref, wu_ref, lam_ref = next(it), next(it), next(it), next(it), next(it)
    if has_h0:
        h0r_ref, h0i_ref = next(it), next(it)
    if fuse:
        yb_ref, dsk_ref = next(it), next(it)
    y_ref, fr_ref, fi_ref = next(it), next(it), next(it)
    xs_ref, st_ref = next(it), next(it)
    if fuse:
        yacc_ref = next(it)

    tc, _, bs, _ = u_ref.shape
    rows = tc * bs
    c = pl.program_id(1)
    first, second = (1, 0) if reverse else (0, 1)

    @pl.when(c == 0)
    def _():
        for j in range(nb):
            if has_h0:
                st_ref[j, :, 0:sw] = h0r_ref[:, j * sw:(j + 1) * sw]
                st_ref[j, :, sw:2 * sw] = h0i_ref[:, j * sw:(j + 1) * sw]
            else:
                st_ref[j] = jnp.zeros(st_ref.shape[1:], F32)

    for j in range(nb):
        cs = slice(j * LANES, (j + 1) * LANES)
        u2 = jnp.concatenate([u_ref[:, first, :, cs].reshape(rows, LANES),
                              u_ref[:, second, :, cs].reshape(rows, LANES)], axis=1).astype(BF16)
        xs_ref[j] = _dot(u2, wx_ref[j]).reshape(tc, bs, 2 * sw)
        lr = jnp.broadcast_to(lam_ref[j, :, 0:sw], (bs, sw))
        li = jnp.broadcast_to(lam_ref[j, :, sw:2 * sw], (bs, sw))
        s_re, s_im = st_ref[j, :, 0:sw], st_ref[j, :, sw:2 * sw]
        for i in range(tc):
            t = (tc - 1 - i) if reverse else i
            x_re = xs_ref[j, t, :, 0:sw]
            x_im = xs_ref[j, t, :, sw:2 * sw]
            xs_ref[j, t, :, 0:sw] = s_re
            xs_ref[j, t, :, sw:2 * sw] = s_im
            s_re, s_im = lr * s_re - li * s_im + x_re, lr * s_im + li * s_re + x_im
        st_ref[j, :, 0:sw] = s_re
        st_ref[j, :, sw:2 * sw] = s_im
        fr_ref[:, j * sw:(j + 1) * sw] = s_re
        fi_ref[:, j * sw:(j + 1) * sw] = s_im
        yj = _dot(xs_ref[j].reshape(rows, 2 * sw).astype(BF16), wy_ref[j]) + _dot(u2, wu_ref[j])
        for half, parity in ((0, first), (1, second)):
            yh = yj[:, half * LANES:(half + 1) * LANES]
            if fuse:
                yacc_ref[parity, :, cs] = yh
            else:
                y_ref[:, parity, :, cs] = yh.reshape(tc, bs, LANES)

    if fuse:
        da = nb * LANES
        for parity in range(2):
            y = (yacc_ref[parity] + yb_ref[:, parity].reshape(rows, da)
                 + dsk_ref[...] * u_ref[:, parity].reshape(rows, da))
            y_ref[:, parity] = jax.nn.gelu(y).reshape(tc, bs, da)


def _s5_pass(u4, tabs, B, T, reverse, h0=None, fuse=None):
    (w_x, w_y, w_u, lam2), which = tabs
    _, nb, _, sw2 = w_x.shape
    sw = sw2 // 2
    da = nb * LANES
    bs = SUBLANES
    tc = 64
    nct = T // (2 * tc)
    tmap = (lambda g, c: (nct - 1 - c, 0, g, 0)) if reverse else (lambda g, c: (c, 0, g, 0))
    pick = lambda g, c: (which, 0, 0, 0)
    blk = pl.BlockSpec((tc, 2, bs, da), tmap)
    in_specs = [blk] + [pl.BlockSpec((None,) + a.shape[1:], pick) for a in (w_x, w_y, w_u, lam2)]
    args = [u4, w_x, w_y, w_u, lam2]
    if h0 is not None:
        in_specs += [pl.BlockSpec((bs, nb * sw), lambda g, c: (g, 0))] * 2
        args += list(h0)
    scratch = [pltpu.VMEM((nb, tc, bs, sw2), F32), pltpu.VMEM((nb, bs, sw2), F32)]
    if fuse is not None:
        y_other, d_skip = fuse
        in_specs += [blk, pl.BlockSpec((1, da), lambda g, c: (0, 0))]
        args += [y_other, d_skip]
        scratch.append(pltpu.VMEM((2, tc * bs, da), F32))
    fin = jax.ShapeDtypeStruct((B, nb * sw), F32)
    y, f_re, f_im = pl.pallas_call(
        functools.partial(_s5_kernel, reverse=reverse, has_h0=h0 is not None, fuse=fuse is not None, nb=nb, sw=sw),
        grid=(B // bs, nct), in_specs=in_specs,
        out_specs=[blk,
                   pl.BlockSpec((bs, nb * sw), lambda g, c: (g, 0)),
                   pl.BlockSpec((bs, nb * sw), lambda g, c: (g, 0))],
        out_shape=[jax.ShapeDtypeStruct((T // 2, 2, B, da), F32), fin, fin],
        scratch_shapes=scratch,
        compiler_params=_params("parallel", "arbitrary"),
        name="s5_scan_gelu" if fuse is not None else "s5_scan",
    )(*args)
    return y, f_re, f_im


def _sgu_kernel(uv_ref, g_ref, ws_ref, bias_ref, o_ref, *, nch):
    db = g_ref.shape[1]
    x = jax.nn.gelu(uv_ref[...].astype(F32))
    u = x[:, :db]
    v = x[:, db:]
    mu = jnp.mean(v, axis=-1, keepdims=True)
    vc = v - mu
    var = jnp.mean(vc * vc, axis=-1, keepdims=True)
    vn = (vc * lax.rsqrt(var + EPS) * g_ref[...]).astype(BF16)
    lane = lax.broadcasted_iota(jnp.int32, (CHUNK, LANES), 1)
    lo = lane < (LANES // 2)
    bias = bias_ref[...]
    for c in range(nch):
        r0 = c * CHUNK
        parts = []
        for hp in range(db // LANES):
            vb = vn[r0:r0 + CHUNK, hp * LANES:(hp + 1) * LANES]
            parts.append(jnp.where(lo, _dot(ws_ref[2 * hp], vb), _dot(ws_ref[2 * hp + 1], vb)))
        mixed = jnp.concatenate(parts, axis=1) + bias
        o_ref[r0:r0 + CHUNK, :] = (u[r0:r0 + CHUNK] * mixed).astype(BF16)


def _sgu(uv, g_sgu, w_spatial, bias_full):
    R, db2 = uv.shape
    db = db2 // 2
    nch = 4
    tm = nch * CHUNK
    return pl.pallas_call(
        functools.partial(_sgu_kernel, nch=nch),
        grid=(R // tm,),
        in_specs=[pl.BlockSpec((tm, db2), lambda i: (i, 0)),
                  pl.BlockSpec((1, db), lambda i: (0, 0)),
                  _wspec(w_spatial),
                  pl.BlockSpec(bias_full.shape, lambda i: (0, 0))],
        out_specs=pl.BlockSpec((tm, db), lambda i: (i, 0)),
        out_shape=jax.ShapeDtypeStruct((R, db), BF16),
        compiler_params=_params("parallel"), name="chunk_sgu",
    )(uv, g_sgu, w_spatial[0], bias_full)


def _attend(q, kcat, vcat, sink_ref, bias, o_ref):
    assert N_KV * HEAD_DIM == LANES and kcat.shape[1] == LANES
    M = q.shape[0]
    scale = HEAD_DIM ** -0.5
    lane = lax.broadcasted_iota(jnp.int32, (M, LANES), 1)
    lo = lane < HEAD_DIM
    heads_per_kv = N_HEADS // N_KV
    rows = []
    for blk in range(N_HEADS * HEAD_DIM // LANES):
        kv = (2 * blk) // heads_per_kv
        qb = q[:, blk * LANES:(blk + 1) * LANES] * scale
        qr = pltpu.roll(qb, HEAD_DIM, 1)
        if kv == 0:
            rows += [jnp.where(lo, qb, 0.0), jnp.where(lo, qr, 0.0)]
        else:
            rows += [jnp.where(lo, 0.0, qr), jnp.where(lo, 0.0, qb)]
    qst = jnp.concatenate(rows, axis=0).astype(BF16)
    s = lax.dot_general(qst, kcat.astype(BF16), (((1,), (1,)), ((), ())), preferred_element_type=F32)
    if bias is not None:
        cols, at = [], 0
        for off, blk_bias in bias:
            if off > at:
                cols.append(s[:, at:off])
            cols.append(s[:, off:off + LANES] + jnp.concatenate([blk_bias] * N_HEADS, axis=0))
            at = off + LANES
        cols.append(s[:, at:])
        s = jnp.concatenate(cols, axis=1)
    ps, es = [], []
    for h in range(N_HEADS):
        sh = s[h * M:(h + 1) * M]
        mh = jnp.maximum(jnp.max(sh, axis=-1, keepdims=True), sink_ref[h])
        ps.append(jnp.exp(sh - mh).astype(BF16))
        es.append(jnp.exp(sink_ref[h] - mh))
    vaug = jnp.concatenate([vcat.astype(BF16), jnp.ones(vcat.shape, BF16)], axis=1)
    oa = _dot(jnp.concatenate(ps, axis=0), vaug)
    o = [oa[h * M:(h + 1) * M, :LANES] * (1.0 / (oa[h * M:(h + 1) * M, LANES:] + es[h])) for h in range(N_HEADS)]
    for blk in range(N_HEADS * HEAD_DIM // LANES):
        kv = (2 * blk) // heads_per_kv
        oa_, ob_ = o[2 * blk], o[2 * blk + 1]
        if kv == 0:
            out = jnp.where(lo, oa_, pltpu.roll(ob_, HEAD_DIM, 1))
        else:
            out = jnp.where(lo, pltpu.roll(oa_, HEAD_DIM, 1), ob_)
        o_ref[:, blk * LANES:(blk + 1) * LANES] = out.astype(o_ref.dtype)


def _ctx_attn_kernel(sink_ref, q_ref, k_ref, v_ref, o_ref):
    _attend(q_ref[...], k_ref[...], v_ref[...], sink_ref, None, o_ref)


def _context_attention(q, k, v, sink, B, T):
    R, dc = q.shape
    kvw = k.shape[1]
    nq = T // BLOCK
    return pl.pallas_call(
        _ctx_attn_kernel,
        grid=(B, nq),
        in_specs=[pl.BlockSpec(memory_space=pltpu.SMEM),
                  pl.BlockSpec((BLOCK, dc), lambda b, i: (b * nq + i, 0)),
                  pl.BlockSpec((T, kvw), lambda b, i: (b, 0)),
                  pl.BlockSpec((T, kvw), lambda b, i: (b, 0))],
        out_specs=pl.BlockSpec((BLOCK, dc), lambda b, i: (b * nq + i, 0)),
        out_shape=jax.ShapeDtypeStruct((R, dc), BF16),
        compiler_params=_params("parallel", "parallel"), name="context_attention",
    )(sink, q, k, v)


def _win_attn_kernel(sink_ref, q_ref, kp_ref, kc_ref, kn_ref, vp_ref, vc_ref, vn_ref, ck_ref, cv_ref, o_ref, *,
                     nblk, nq):
    i = pl.program_id(1)
    k_blocks = [kp_ref[...]] + [kc_ref[n * BLOCK:(n + 1) * BLOCK] for n in range(nq)] + [kn_ref[...]]
    v_blocks = [vp_ref[...]] + [vc_ref[n * BLOCK:(n + 1) * BLOCK] for n in range(nq)] + [vn_ref[...]]
    qi = lax.broadcasted_iota(jnp.int32, (BLOCK, BLOCK), 0)
    kj = lax.broadcasted_iota(jnp.int32, (BLOCK, BLOCK), 1)
    edge = [jnp.where(i > 0, 0, BLOCK)] + [0] * nq + [jnp.where(nq * (i + 1) < nblk, 0, BLOCK)]
    for n in range(nq):
        kcat = jnp.concatenate(k_blocks[n:n + 3] + [ck_ref[...]], axis=0)
        vcat = jnp.concatenate(v_blocks[n:n + 3] + [cv_ref[...]], axis=0)
        prev_bias = jnp.where(qi + edge[n] <= kj, 0.0, -jnp.inf)
        next_bias = jnp.where(kj + edge[n + 2] <= qi, 0.0, -jnp.inf)
        _attend(q_ref[n * BLOCK:(n + 1) * BLOCK, :], kcat, vcat, sink_ref,
                [(0, prev_bias), (2 * BLOCK, next_bias)], o_ref.at[n * BLOCK:(n + 1) * BLOCK, :])


def _window_attention(q, k, v, cache_k4, cache_v4, layer, sink, B, T):
    R, dc = q.shape
    kvw = k.shape[1]
    nblk = T // BLOCK
    past = cache_k4.shape[2]
    nq = 4 if nblk % 4 == 0 else 2
    ngrp = nblk // nq
    assert nblk == nq * ngrp
    cur = lambda b, i: (b * ngrp + i, 0)
    prv = lambda b, i: (b * nblk + jnp.maximum(nq * i - 1, 0), 0)
    nxt = lambda b, i: (b * nblk + jnp.minimum(nq * (i + 1), nblk - 1), 0)
    kvs = lambda f: pl.BlockSpec((BLOCK, kvw), f)
    kv2 = pl.BlockSpec((nq * BLOCK, kvw), cur)
    ctx = pl.BlockSpec((None, None, past, kvw), lambda b, i: (b, layer, 0, 0))
    return pl.pallas_call(
        functools.partial(_win_attn_kernel, nblk=nblk, nq=nq),
        grid=(B, ngrp),
        in_specs=[pl.BlockSpec(memory_space=pltpu.SMEM),
                  pl.BlockSpec((nq * BLOCK, dc), cur),
                  kvs(prv), kv2, kvs(nxt), kvs(prv), kv2, kvs(nxt), ctx, ctx],
        out_specs=pl.BlockSpec((nq * BLOCK, dc), cur),
        out_shape=jax.ShapeDtypeStruct((R, dc), BF16),
        compiler_params=_params("parallel", "parallel"), name="window_attention",
    )(sink, q, k, k, k, v, v, v, cache_k4, cache_v4)


def _merge_kernel(x_ref, mod_ref, gt_ref, ga_ref, pb_ref, ao_ref, wglu_ref, wb_ref, wc_ref, wo_ref, g_ref, o_ref):
    dm = x_ref.shape[1]
    z = _dot(ga_ref[...].astype(BF16), wglu_ref[...])
    ya = z[:, :dm] * jax.nn.sigmoid(z[:, dm:])
    yb = _dot(pb_ref[...], wb_ref[...])
    yc = _dot(ao_ref[...], wc_ref[...])
    gts = jax.nn.sigmoid(gt_ref[...].astype(F32))
    merged = gts[:, 0:dm] * ya + gts[:, dm:2 * dm] * yb + gts[:, 2 * dm:3 * dm] * yc
    mo = _dot(merged.astype(BF16), wo_ref[...])
    m = mod_ref[...]
    o_ref[...] = x_ref[...] + m[2:3] * _rms(mo, g_ref[...])


def _merge(x2, mod, gates, ga_t, pb, ao, w_glu, w_b_out, w_c_out, w_o, g_post, B, T):
    R, D = x2.shape
    tm = min(512, T)
    nt = T // tm
    bm = mod.shape[0]
    da = w_glu[0].shape[1]
    mod_map = (lambda i: (i // nt, 0, 0)) if bm > 1 else (lambda i: (0, 0, 0))
    row = lambda w: pl.BlockSpec((tm, w), lambda i: (i, 0))
    return pl.pallas_call(
        _merge_kernel,
        grid=(R // tm,),
        in_specs=[row(D), pl.BlockSpec((None, 8, D), mod_map), row(gates.shape[1]),
                  pl.BlockSpec((tm, da), lambda i: (i % nt, i // nt)),
                  row(pb.shape[1]), row(ao.shape[1]),
                  _wspec(w_glu), _wspec(w_b_out), _wspec(w_c_out), _wspec(w_o),
                  pl.BlockSpec((1, D), lambda i: (0, 0))],
        out_specs=row(D),
        out_shape=jax.ShapeDtypeStruct((R, D), F32),
        compiler_params=_params("parallel"), name="merge_out_proj",
    )(x2, mod, gates, ga_t, pb, ao, w_glu[0], w_b_out[0], w_c_out[0], w_o[0], g_post)


def _ffn_kernel(*refs, halo, nt, tf):
    if halo:
        (x_ref, xp_ref, xn_ref, mod_ref, gpre_ref, wu_ref, cw_ref, cb_ref, wd_ref, gpost_ref,
         o_ref, h_ref, z_ref, a_ref, y_ref) = refs
    else:
        (x_ref, mod_ref, gpre_ref, wu_ref, cw_ref, cb_ref, wd_ref, gpost_ref,
         o_ref, h_ref, z_ref, a_ref, y_ref) = refs
    i = pl.program_id(0)
    tm = x_ref.shape[0]
    dff = wd_ref.shape[0]
    m = mod_ref[...]
    norm = lambda x: _rms(x, gpre_ref[...]) * (1.0 + m[4:5]) + m[3:4]
    h_ref[HALO:HALO + tm, :] = norm(x_ref[...]).astype(BF16)
    if halo:
        keep_prev = jnp.where(i % nt == 0, 0.0, 1.0)
        keep_next = jnp.where(i % nt == nt - 1, 0.0, 1.0)
        h_ref[0:HALO, :] = (norm(xp_ref[...]) * keep_prev).astype(BF16)
        h_ref[HALO + tm:, :] = (norm(xn_ref[...]) * keep_next).astype(BF16)
    else:
        h_ref[0:HALO, :] = jnp.zeros((HALO, h_ref.shape[1]), BF16)
        h_ref[HALO + tm:, :] = jnp.zeros((HALO, h_ref.shape[1]), BF16)
    hx = h_ref[...]
    half = tm // 2
    nl = tf // LANES

    def conv(slot, col):
        z = _dot(hx, wu_ref[:, col:col + tf])
        ev, od = [], []
        for k in range(nl):
            sl = slot * nl + k
            z_ref[sl] = z[:, k * LANES:(k + 1) * LANES]
            taps = [z_ref[sl, pl.ds(HALO - 1 + d, half, stride=2), :] for d in range(4)]
            cw = cw_ref[:, col + k * LANES:col + (k + 1) * LANES]
            cb = cb_ref[:, col + k * LANES:col + (k + 1) * LANES]
            ev.append(taps[0] * cw[0:1] + taps[1] * cw[1:2] + taps[2] * cw[2:3] + cb)
            od.append(taps[1] * cw[0:1] + taps[2] * cw[1:2] + taps[3] * cw[2:3] + cb)
        return jnp.concatenate(ev, axis=1), jnp.concatenate(od, axis=1)

    for c in range(dff // tf):
        g_e, g_o = conv(2 * (c % 2), c * tf)
        v_e, v_o = conv(2 * (c % 2) + 1, dff + c * tf)
        a_ref[0:half, c * tf:(c + 1) * tf] = (jax.nn.silu(g_e) * v_e).astype(BF16)
        a_ref[half:tm, c * tf:(c + 1) * tf] = (jax.nn.silu(g_o) * v_o).astype(BF16)
    y = _dot(a_ref[...], wd_ref[...])
    for k in range(y.shape[1] // LANES):
        y_ref[k, pl.ds(0, half, stride=2), :] = y[0:half, k * LANES:(k + 1) * LANES]
        y_ref[k, pl.ds(1, half, stride=2), :] = y[half:tm, k * LANES:(k + 1) * LANES]
    y = jnp.concatenate([y_ref[k] for k in range(y.shape[1] // LANES)], axis=1)
    o_ref[...] = x_ref[...] + m[5:6] * _rms(y, gpost_ref[...])


def _conv_ffn(x2, mod, g_pre, w_up, conv_w, conv_b, w_down, g_post, B, T):
    R, D = x2.shape
    dff = w_down[0].shape[1]
    tf = 256
    tm = min(512, T)
    nt = T // tm
    halo = nt > 1
    bm = mod.shape[0]
    mod_map = (lambda i: (i // nt, 0, 0)) if bm > 1 else (lambda i: (0, 0, 0))
    hb = tm // HALO
    nhb = R // HALO
    const = lambda i: (0, 0)
    whole = lambda a: pl.BlockSpec(a.shape, const, pipeline_mode=pl.Buffered(1))
    in_specs = [pl.BlockSpec((tm, D), lambda i: (i, 0))]
    args = [x2]
    if halo:
        in_specs += [pl.BlockSpec((HALO, D), lambda i: (jnp.maximum(i * hb - 1, 0), 0)),
                     pl.BlockSpec((HALO, D), lambda i: (jnp.minimum((i + 1) * hb, nhb - 1), 0))]
        args += [x2, x2]
    in_specs += [pl.BlockSpec((None, 8, D), mod_map), pl.BlockSpec((1, D), const),
                 _wspec(w_up), whole(conv_w), whole(conv_b), _wspec(w_down), pl.BlockSpec((1, D), const)]
    args += [mod, g_pre, w_up[0], conv_w, conv_b, w_down[0], g_post]
    return pl.pallas_call(
        functools.partial(_ffn_kernel, halo=halo, nt=nt, tf=tf),
        grid=(R // tm,),
        in_specs=in_specs,
        out_specs=pl.BlockSpec((tm, D), lambda i: (i, 0)),
        out_shape=jax.ShapeDtypeStruct((R, D), F32),
        scratch_shapes=[pltpu.VMEM((tm + 2 * HALO, D), BF16),
                        pltpu.VMEM((4 * tf // LANES, tm + 2 * HALO, LANES), F32),
                        pltpu.VMEM((tm, dff), BF16),
                        pltpu.VMEM((D // LANES, tm, LANES), F32)],
        compiler_params=_params("parallel"), name="conv_ffn",
    )(*args)


def _rope_tables(T):
    t = jnp.arange(T)
    row = (t // GRID_W).astype(F32)[:, None]
    col = (t % GRID_W).astype(F32)[:, None]
    half = HEAD_DIM // 2
    inv = ROPE_BASE ** (-jnp.arange(0, half, 2, dtype=F32) / half)
    lane = jnp.arange(LANES)
    freq = inv[lane % (half // 2)][None, :]
    ang = jnp.where(((lane % HEAD_DIM) < half)[None, :], row * freq, col * freq)
    sign = jnp.where((lane % half) < half // 2, -1.0, 1.0)[None, :]
    return jnp.cos(ang), jnp.sin(ang) * sign


def _layer(x2, mod, lp, B, T, ctx):
    D = x2.shape[1]
    rope_tabs = None if ctx is None else lp['rope']
    a_t, uv, q, k, v, gates = _in_proj(x2, mod, lp['g_pre_mix'], lp['w_in'], B, T, lp['dims'], rope_tabs)
    h0 = (None, None) if ctx is None else ctx[3]
    tabs, row0 = lp['s5_tabs']
    y_bwd, br_re, br_im = _s5_pass(a_t, (tabs, row0 + 1), B, T, True, h0=h0[1])
    ga_t, fw_re, fw_im = _s5_pass(a_t, (tabs, row0), B, T, False, h0=h0[0], fuse=(y_bwd, lp['d_skip']))
    pb = _sgu(uv, lp['g_sgu'], lp['w_spatial'], lp['sgu_bias'])
    if ctx is None:
        ao = _context_attention(q, k, v, lp['sink'], B, T)
    else:
        ao = _window_attention(q, k, v, ctx[0], ctx[1], ctx[2], lp['sink'], B, T)
    x2 = _merge(x2, mod, gates, ga_t.reshape(T, -1), pb, ao, lp['w_glu'], lp['w_b_out'], lp['w_c_out'],
                lp['w_o'], lp['g_post_mix'], B, T)
    x2 = _conv_ffn(x2, mod, lp['g_pre_ffn'], lp['w_up'], lp['conv_w'], lp['conv_b'], lp['w_down'],
                   lp['g_post_ffn'], B, T)
    return x2, (k, v, (fw_re, br_re), (fw_im, br_im))


def kernel(x_prompt, x_sample, cache_k, cache_v, state_ssm_re, state_ssm_im, c, c_ctx,
           w_mod, b_mod, g_pre_mix, g_post_mix, g_pre_ffn, g_post_ffn, w_in,
           lam_re, lam_im, log_step, b_re, b_im, c_re, c_im, d_skip, w_glu,
           g_sgu, w_spatial, b_spatial, w_b_out, sink, w_c_out, w_o,
           w_up, conv_w, conv_b, w_down):
    BP, TP, D = x_prompt.shape
    BS, TS, _ = x_sample.shape
    L = w_in.shape[0]
    G, N = lam_re.shape[2], lam_re.shape[3]
    P = b_re.shape[-1]
    d_a = G * P
    d_b2 = 2 * g_sgu.shape[1]
    d_c = w_c_out.shape[1]
    kv_w = N_KV * HEAD_DIM
    d_g = w_in.shape[2] - (d_a + d_b2 + d_c + 2 * kv_w)
    past = cache_k.shape[2]

    nc = 16
    cvecs = jnp.zeros((nc, D), F32).at[0].set(c_ctx).at[1:1 + BS].set(c)
    mod_all = _modulation(cvecs, w_mod, b_mod).reshape(L, nc, 6, D)
    mod_all = jnp.pad(mod_all, ((0, 0), (0, 0), (0, 2), (0, 0)))

    rope = _rope_tables(TS)
    cache_k4 = cache_k.reshape(BS, L, past, kv_w)
    cache_v4 = cache_v.reshape(BS, L, past, kv_w)

    xp = x_prompt.reshape(BP * TP, D)
    xs = x_sample.reshape(BS * TS, D)
    new_k, new_v, new_re, new_im = [], [], [], []
    s5_tabs = _s5_tables(_s5_discretise(lam_re, lam_im, log_step, b_re, b_im, c_re, c_im), G, N, P)
    wb16 = {name: w.astype(BF16) for name, w in (
        ('w_in', w_in), ('w_glu', w_glu), ('w_spatial', w_spatial), ('w_b_out', w_b_out),
        ('w_c_out', w_c_out), ('w_o', w_o), ('w_up', w_up), ('w_down', w_down))}
    for l in range(L):
        tabs = (s5_tabs, 2 * l)
        lp = {
            'dims': (d_a, d_b2, d_c, kv_w, d_g),
            'g_pre_mix': g_pre_mix[l][None], 'g_post_mix': g_post_mix[l][None],
            'g_pre_ffn': g_pre_ffn[l][None], 'g_post_ffn': g_post_ffn[l][None],
            's5_tabs': tabs, 'd_skip': d_skip[l][None], 'g_sgu': g_sgu[l][None],
            'sgu_bias': jnp.repeat(b_spatial[l].T, g_sgu.shape[1] // b_spatial.shape[1], axis=1),
            'sink': sink[l], 'conv_w': conv_w[l], 'conv_b': conv_b[l][None], 'rope': rope,
        }
        lp.update({name: (w, l) for name, w in wb16.items()})
        xp, (k_l, v_l, f_re, f_im) = _layer(xp, mod_all[l, 0:1], lp, BP, TP, None)
        new_k.append(k_l.reshape(BP, TP, N_KV, HEAD_DIM))
        new_v.append(v_l.reshape(BP, TP, N_KV, HEAD_DIM))
        new_re.append(jnp.stack([f.reshape(BP, G, N) for f in f_re], axis=1))
        new_im.append(jnp.stack([f.reshape(BP, G, N) for f in f_im], axis=1))
        h0 = [(state_ssm_re[:, l, d].reshape(BS, G * N), state_ssm_im[:, l, d].reshape(BS, G * N)) for d in range(2)]
        xs, _ = _layer(xs, mod_all[l, 1:1 + BS], lp, BS, TS, (cache_k4, cache_v4, l, h0))
    return (xp.reshape(BP, TP, D), xs.reshape(BS, TS, D),
            jnp.stack(new_k, axis=1), jnp.stack(new_v, axis=1),
            jnp.stack(new_re, axis=1), jnp.stack(new_im, axis=1))
```

```python
import functools
import math

import jax
import jax.numpy as jnp
from jax import lax
from jax.experimental import pallas as pl
from jax.experimental.pallas import tpu as pltpu

F32 = jnp.float32
BF16 = jnp.bfloat16

GRID_W = 64
SSM_P = 16
SSM_N = 64
CHUNK = 128
N_HEADS = 8
N_KV = 2
HEAD_DIM = 64
BLOCK = 128
ROPE_BASE = 10000.0
EPS = 1e-6

LANES = 128
SUBLANES = 8
HALO = 16
VMEM_LIMIT = 56 * 1024 * 1024


def _params(*sem):
    return pltpu.CompilerParams(dimension_semantics=sem, vmem_limit_bytes=VMEM_LIMIT)


def _dot(a, b):
    return jnp.dot(a, b, preferred_element_type=F32)


def _wspec(w):
    arr, layer = w
    idx = (layer,) + (0,) * (arr.ndim - 1)
    return pl.BlockSpec((None,) + arr.shape[1:], lambda *_: idx, pipeline_mode=pl.Buffered(1))


def _sigmoid(x):
    return 0.5 * jnp.tanh(0.5 * x) + 0.5


def _rms(x, g):
    return x * lax.rsqrt(jnp.mean(x * x, axis=-1, keepdims=True) + EPS) * g


def _mod_kernel(c_ref, w_ref, b_ref, o_ref):
    a = jax.nn.silu(c_ref[...]).astype(BF16)
    o_ref[...] = _dot(a, w_ref[...].astype(BF16)) + b_ref[...]


def _modulation(cvecs, w_mod, b_mod):
    L, D, D6 = w_mod.shape
    NC = cvecs.shape[0]
    tn = D6 // 4
    return pl.pallas_call(
        _mod_kernel,
        grid=(L, D6 // tn),
        in_specs=[pl.BlockSpec((NC, D), lambda l, j: (0, 0)),
                  pl.BlockSpec((None, D, tn), lambda l, j: (l, 0, j)),
                  pl.BlockSpec((None, 1, tn), lambda l, j: (l, 0, j))],
        out_specs=pl.BlockSpec((None, NC, tn), lambda l, j: (l, 0, j)),
        out_shape=jax.ShapeDtypeStruct((L, NC, D6), F32),
        compiler_params=_params("parallel", "parallel"),
        name="modulation",
    )(cvecs, w_mod, b_mod.reshape(L, 1, D6))


def _rope(x, cos, sin):
    lane = lax.broadcasted_iota(jnp.int32, cos.shape, 1)
    first = (lane % 32) < 16
    outs = []
    for j in range(x.shape[1] // LANES):
        xb = x[:, j * LANES:(j + 1) * LANES]
        partner = jnp.where(first, pltpu.roll(xb, LANES - 16, 1), pltpu.roll(xb, 16, 1))
        outs.append(xb * cos + partner * sin)
    return outs[0] if len(outs) == 1 else jnp.concatenate(outs, axis=1)


INPROJ_TT = 64
INPROJ_PITCH = INPROJ_TT + 8


def _inproj_kernel(*refs, rope, offs):
    if rope:
        (x_ref, mod_ref, g_ref, w_ref, cos_ref, sin_ref,
         a_ref, uv_ref, q_ref, k_ref, v_ref, gt_ref, a_scr) = refs
    else:
        x_ref, mod_ref, g_ref, w_ref, a_ref, uv_ref, q_ref, k_ref, v_ref, gt_ref, a_scr = refs
    ob, oq, ok, ov, og, dn = offs
    bs, tt, D = x_ref.shape
    rows = bs * tt
    m = mod_ref[...]
    x = x_ref[...]
    h = x * lax.rsqrt(jnp.mean(x * x, axis=-1, keepdims=True) + EPS) * g_ref[...]
    h = h * (1.0 + m[:, 1:2, :]) + m[:, 0:1, :]
    hb = h.reshape(rows, D).astype(BF16)

    a = _dot(hb, w_ref[:, 0:ob])
    for b in range(bs):
        for j in range(ob // LANES):
            a_scr[j, b * INPROJ_PITCH:b * INPROJ_PITCH + tt, :] = a[b * tt:(b + 1) * tt, j * LANES:(j + 1) * LANES]
    for t in range(tt):
        for j in range(ob // LANES):
            a_ref[t // 2, t % 2, :, j * LANES:(j + 1) * LANES] = a_scr[j, pl.ds(t, bs, stride=INPROJ_PITCH), :]

    uv_ref[...] = _dot(hb, w_ref[:, ob:oq]).astype(BF16).reshape(bs, tt, oq - ob)
    q = _dot(hb, w_ref[:, oq:ok])
    k = _dot(hb, w_ref[:, ok:ov])
    if rope:
        cos = jnp.concatenate([cos_ref[...]] * bs, axis=0)
        sin = jnp.concatenate([sin_ref[...]] * bs, axis=0)
        q = _rope(q, cos, sin)
        k = _rope(k, cos, sin)
    q_ref[...] = q.reshape(bs, tt, ok - oq)
    k_ref[...] = k.reshape(bs, tt, ov - ok)
    v_ref[...] = _dot(hb, w_ref[:, ov:og]).reshape(bs, tt, og - ov)
    gt_ref[...] = _dot(hb, w_ref[:, og:dn]).astype(BF16).reshape(bs, tt, dn - og)


def _in_proj(x2, mod, g_pre, w_in, B, T, dims, rope_tabs=None):
    R, D = x2.shape
    d_a, d_b2, d_c, kv_w, d_g = dims
    ob = d_a
    oq = ob + d_b2
    ok = oq + d_c
    ov = ok + kv_w
    og = ov + kv_w
    dn = og + d_g
    bs, tt = SUBLANES, INPROJ_TT
    bm = mod.shape[0]
    mod_spec = (pl.BlockSpec((bs, 8, D), lambda g, i: (g, 0, 0)) if bm > 1
                else pl.BlockSpec((1, 8, D), lambda g, i: (0, 0, 0)))
    in_specs = [pl.BlockSpec((bs, tt, D), lambda g, i: (g, i, 0)),
                mod_spec,
                pl.BlockSpec((1, D), lambda g, i: (0, 0)),
                _wspec(w_in)]
    args = [x2.reshape(B, T, D), mod, g_pre, w_in[0]]
    if rope_tabs is not None:
        in_specs += [pl.BlockSpec((tt, LANES), lambda g, i: (i, 0))] * 2
        args += list(rope_tabs)
    seq = lambda w: pl.BlockSpec((bs, tt, w), lambda g, i: (g, i, 0))
    out_specs = [pl.BlockSpec((tt // 2, 2, bs, d_a), lambda g, i: (i, 0, g, 0)),
                 seq(d_b2), seq(d_c), seq(kv_w), seq(kv_w), seq(d_g)]
    out_shape = [jax.ShapeDtypeStruct((T // 2, 2, B, d_a), F32),
                 jax.ShapeDtypeStruct((B, T, d_b2), BF16),
                 jax.ShapeDtypeStruct((B, T, d_c), F32),
                 jax.ShapeDtypeStruct((B, T, kv_w), F32),
                 jax.ShapeDtypeStruct((B, T, kv_w), F32),
                 jax.ShapeDtypeStruct((B, T, d_g), BF16)]
    a4, uv, q, k, v, gates = pl.pallas_call(
        functools.partial(_inproj_kernel, rope=rope_tabs is not None, offs=(ob, oq, ok, ov, og, dn)),
        grid=(B // bs, T // tt), in_specs=in_specs, out_specs=out_specs, out_shape=out_shape,
        scratch_shapes=[pltpu.VMEM((d_a // LANES, bs * INPROJ_PITCH, LANES), F32)],
        compiler_params=_params("parallel", "parallel"), name="in_proj",
    )(*args)
    flat = lambda y: y.reshape(R, y.shape[-1])
    return a4, flat(uv), flat(q), flat(k), flat(v), flat(gates)


def _s5_disc_kernel(lr_ref, li_ref, ls_ref, br_ref, bi_ref, cr_ref, ci_ref,
                    l2r_ref, l2i_ref, bbr_ref, bbi_ref, lbbr_ref, lbbi_ref,
                    c1r_ref, c1i_ref, c2r_ref, c2i_ref, cb_ref, clb_ref):
    lr = lr_ref[...]
    li = li_ref[...]
    dt = jnp.exp(ls_ref[...])
    ar = lr * dt
    ai = li * dt
    mag = jnp.exp(ar)
    lb_re = (mag * jnp.cos(ai))[:, None, :]
    lb_im = (mag * jnp.sin(ai))[:, None, :]
    lr3 = lr[:, None, :]
    li3 = li[:, None, :]
    den = lr3 * lr3 + li3 * li3
    f_re = ((lb_re - 1.0) * lr3 + lb_im * li3) / den
    f_im = (lb_im * lr3 - (lb_re - 1.0) * li3) / den
    l2_re = lb_re * lb_re - lb_im * lb_im
    l2_im = 2.0 * lb_re * lb_im
    l2r_ref[...] = l2_re
    l2i_ref[...] = l2_im
    br = br_ref[...]
    bi = bi_ref[...]
    bb_re = f_re * br - f_im * bi
    bb_im = f_re * bi + f_im * br
    lbb_re = lb_re * bb_re - lb_im * bb_im
    lbb_im = lb_re * bb_im + lb_im * bb_re
    bbr_ref[...] = bb_re
    bbi_ref[...] = bb_im
    lbbr_ref[...] = lbb_re
    lbbi_ref[...] = lbb_im
    cr = cr_ref[...]
    ci = ci_ref[...]
    c1r_ref[...] = cr * lb_re - ci * lb_im
    c1i_ref[...] = cr * lb_im + ci * lb_re
    c2r_ref[...] = cr * l2_re - ci * l2_im
    c2i_ref[...] = cr * l2_im + ci * l2_re
    for q in range(br.shape[1]):
        cb_ref[:, q, :] = jnp.sum(cr * bb_re[:, q:q + 1, :] - ci * bb_im[:, q:q + 1, :], axis=-1)
        clb_ref[:, q, :] = jnp.sum(cr * lbb_re[:, q:q + 1, :] - ci * lbb_im[:, q:q + 1, :], axis=-1)


def _s5_discretise(lam_re, lam_im, log_step, b_re, b_im, c_re, c_im):
    N = lam_re.shape[-1]
    P = b_re.shape[-1]
    rows = lam_re.size // N
    tr = lambda b: jnp.swapaxes(b, -1, -2).reshape(rows, P, N)
    vec = jax.ShapeDtypeStruct((rows, 1, N), F32)
    mat = jax.ShapeDtypeStruct((rows, P, N), F32)
    sq = jax.ShapeDtypeStruct((rows, P, P), F32)
    return pl.pallas_call(_s5_disc_kernel, out_shape=[vec, vec] + [mat] * 8 + [sq, sq], name="s5_discretise")(
        lam_re.reshape(rows, N), lam_im.reshape(rows, N), log_step.reshape(rows, 1), tr(b_re), tr(b_im),
        c_re.reshape(rows, P, N), c_im.reshape(rows, P, N))


def _s5_tables(disc, G, N, P):
    l2_re, l2_im, bb_re, bb_im, lbb_re, lbb_im, c1_re, c1_im, c2_re, c2_im, cb, clb = disc
    GB = LANES // P
    NB = G // GB
    X = l2_re.shape[0] // G
    eye = jnp.eye(GB, dtype=F32)

    def w_in(x):
        return jnp.einsum('xkgpn,gh->xkgphn', x.reshape(X, NB, GB, P, N), eye).reshape(X, NB, GB * P, GB * N)

    def w_out(x):
        return jnp.einsum('xkgpn,gh->xkgnhp', x.reshape(X, NB, GB, P, N), eye).reshape(X, NB, GB * N, GB * P)

    def w_dir(x):
        return jnp.einsum('xkgqp,gh->xkgqhp', x.reshape(X, NB, GB, P, P), eye).reshape(X, NB, GB * P, GB * P)

    w_x = jnp.concatenate([jnp.concatenate([w_in(lbb_re), w_in(lbb_im)], axis=-1),
                           jnp.concatenate([w_in(bb_re), w_in(bb_im)], axis=-1)], axis=2).astype(BF16)
    w_y = jnp.concatenate([jnp.concatenate([w_out(c1_re), -w_out(c1_im)], axis=2),
                           jnp.concatenate([w_out(c2_re), -w_out(c2_im)], axis=2)], axis=-1).astype(BF16)
    d0, d1 = w_dir(cb), w_dir(clb)
    w_u = jnp.concatenate([jnp.concatenate([d0, d1], axis=-1),
                           jnp.concatenate([jnp.zeros_like(d0), d0], axis=-1)], axis=2).astype(BF16)
    lam2 = jnp.concatenate([l2_re.reshape(X, NB, 1, GB * N), l2_im.reshape(X, NB, 1, GB * N)], axis=-1)
    return w_x, w_y, w_u, lam2


def _s5_kernel(*refs, reverse, has_h0, fuse, nb, sw):
    it = iter(refs)
    u_ref, wx_ref, wy_ref, wu_ref, lam_ref = next(it), next(it), next(it), next(it), next(it)
    if has_h0:
        h0r_ref, h0i_ref = next(it), next(it)
    if fuse:
        yb_ref, dsk_ref = next(it), next(it)
    y_ref, fr_ref, fi_ref = next(it), next(it), next(it)
    xs_ref, st_ref = next(it), next(it)
    if fuse:
        yacc_ref = next(it)

    tc, _, bs, _ = u_ref.shape
    rows = tc * bs
    c = pl.program_id(1)
    first, second = (1, 0) if reverse else (0, 1)

    @pl.when(c == 0)
    def _():
        for j in range(nb):
            if has_h0:
                st_ref[j, :, 0:sw] = h0r_ref[:, j * sw:(j + 1) * sw]
                st_ref[j, :, sw:2 * sw] = h0i_ref[:, j * sw:(j + 1) * sw]
            else:
                st_ref[j] = jnp.zeros(st_ref.shape[1:], F32)

    for j in range(nb):
        cs = slice(j * LANES, (j + 1) * LANES)
        u2 = jnp.concatenate([u_ref[:, first, :, cs].reshape(rows, LANES),
                              u_ref[:, second, :, cs].reshape(rows, LANES)], axis=1).astype(BF16)
        xs_ref[j] = _dot(u2, wx_ref[j]).reshape(tc, bs, 2 * sw)
        lr = jnp.broadcast_to(lam_ref[j, :, 0:sw], (bs, sw))
        li = jnp.broadcast_to(lam_ref[j, :, sw:2 * sw], (bs, sw))
        s_re, s_im = st_ref[j, :, 0:sw], st_ref[j, :, sw:2 * sw]
        for i in range(tc):
            t = (tc - 1 - i) if reverse else i
            x_re = xs_ref[j, t, :, 0:sw]
            x_im = xs_ref[j, t, :, sw:2 * sw]
            xs_ref[j, t, :, 0:sw] = s_re
            xs_ref[j, t, :, sw:2 * sw] = s_im
            s_re, s_im = lr * s_re - li * s_im + x_re, lr * s_im + li * s_re + x_im
        st_ref[j, :, 0:sw] = s_re
        st_ref[j, :, sw:2 * sw] = s_im
        fr_ref[:, j * sw:(j + 1) * sw] = s_re
        fi_ref[:, j * sw:(j + 1) * sw] = s_im
        yj = _dot(xs_ref[j].reshape(rows, 2 * sw).astype(BF16), wy_ref[j]) + _dot(u2, wu_ref[j])
        for half, parity in ((0, first), (1, second)):
            yh = yj[:, half * LANES:(half + 1) * LANES]
            if fuse:
                yacc_ref[parity, :, cs] = yh
            else:
                y_ref[:, parity, :, cs] = yh.reshape(tc, bs, LANES)

    if fuse:
        da = nb * LANES
        for parity in range(2):
            y = (yacc_ref[parity] + yb_ref[:, parity].reshape(rows, da)
                 + dsk_ref[...] * u_ref[:, parity].reshape(rows, da))
            y_ref[:, parity] = jax.nn.gelu(y).reshape(tc, bs, da)


def _s5_pass(u4, tabs, B, T, reverse, h0=None, fuse=None):
    (w_x, w_y, w_u, lam2), which = tabs
    _, nb, _, sw2 = w_x.shape
    sw = sw2 // 2
    da = nb * LANES
    bs = SUBLANES
    tc = 64
    nct = T // (2 * tc)
    tmap = (lambda g, c: (nct - 1 - c, 0, g, 0)) if reverse else (lambda g, c: (c, 0, g, 0))
    pick = lambda g, c: (which, 0, 0, 0)
    blk = pl.BlockSpec((tc, 2, bs, da), tmap)
    in_specs = [blk] + [pl.BlockSpec((None,) + a.shape[1:], pick) for a in (w_x, w_y, w_u, lam2)]
    args = [u4, w_x, w_y, w_u, lam2]
    if h0 is not None:
        in_specs += [pl.BlockSpec((bs, nb * sw), lambda g, c: (g, 0))] * 2
        args += list(h0)
    scratch = [pltpu.VMEM((nb, tc, bs, sw2), F32), pltpu.VMEM((nb, bs, sw2), F32)]
    if fuse is not None:
        y_other, d_skip = fuse
        in_specs += [blk, pl.BlockSpec((1, da), lambda g, c: (0, 0))]
        args += [y_other, d_skip]
        scratch.append(pltpu.VMEM((2, tc * bs, da), F32))
    fin = jax.ShapeDtypeStruct((B, nb * sw), F32)
    y, f_re, f_im = pl.pallas_call(
        functools.partial(_s5_kernel, reverse=reverse, has_h0=h0 is not None, fuse=fuse is not None, nb=nb, sw=sw),
        grid=(B // bs, nct), in_specs=in_specs,
        out_specs=[blk,
                   pl.BlockSpec((bs, nb * sw), lambda g, c: (g, 0)),
                   pl.BlockSpec((bs, nb * sw), lambda g, c: (g, 0))],
        out_shape=[jax.ShapeDtypeStruct((T // 2, 2, B, da), F32), fin, fin],
        scratch_shapes=scratch,
        compiler_params=_params("parallel", "arbitrary"),
        name="s5_scan_gelu" if fuse is not None else "s5_scan",
    )(*args)
    return y, f_re, f_im


def _sgu_kernel(uv_ref, g_ref, ws_ref, bias_ref, o_ref, *, nch):
    db = g_ref.shape[1]
    x = jax.nn.gelu(uv_ref[...].astype(F32))
    u = x[:, :db]
    v = x[:, db:]
    mu = jnp.mean(v, axis=-1, keepdims=True)
    vc = v - mu
    var = jnp.mean(vc * vc, axis=-1, keepdims=True)
    vn = (vc * lax.rsqrt(var + EPS) * g_ref[...]).astype(BF16)
    lane = lax.broadcasted_iota(jnp.int32, (CHUNK, LANES), 1)
    lo = lane < (LANES // 2)
    bias = bias_ref[...]
    for c in range(nch):
        r0 = c * CHUNK
        parts = []
        for hp in range(db // LANES):
            vb = vn[r0:r0 + CHUNK, hp * LANES:(hp + 1) * LANES]
            parts.append(jnp.where(lo, _dot(ws_ref[2 * hp], vb), _dot(ws_ref[2 * hp + 1], vb)))
        mixed = jnp.concatenate(parts, axis=1) + bias
        o_ref[r0:r0 + CHUNK, :] = (u[r0:r0 + CHUNK] * mixed).astype(BF16)


def _sgu(uv, g_sgu, w_spatial, bias_full):
    R, db2 = uv.shape
    db = db2 // 2
    nch = 4
    tm = nch * CHUNK
    return pl.pallas_call(
        functools.partial(_sgu_kernel, nch=nch),
        grid=(R // tm,),
        in_specs=[pl.BlockSpec((tm, db2), lambda i: (i, 0)),
                  pl.BlockSpec((1, db), lambda i: (0, 0)),
                  _wspec(w_spatial),
                  pl.BlockSpec(bias_full.shape, lambda i: (0, 0))],
        out_specs=pl.BlockSpec((tm, db), lambda i: (i, 0)),
        out_shape=jax.ShapeDtypeStruct((R, db), BF16),
        compiler_params=_params("parallel"), name="chunk_sgu",
    )(uv, g_sgu, w_spatial[0], bias_full)


def _attend(q, kcat, vcat, sink_ref, bias, o_ref):
    assert N_KV * HEAD_DIM == LANES and kcat.shape[1] == LANES
    M = q.shape[0]
    scale = HEAD_DIM ** -0.5
    lane = lax.broadcasted_iota(jnp.int32, (M, LANES), 1)
    lo = lane < HEAD_DIM
    heads_per_kv = N_HEADS // N_KV
    rows = []
    for blk in range(N_HEADS * HEAD_DIM // LANES):
        kv = (2 * blk) // heads_per_kv
        qb = q[:, blk * LANES:(blk + 1) * LANES] * scale
        qr = pltpu.roll(qb, HEAD_DIM, 1)
        if kv == 0:
            rows += [jnp.where(lo, qb, 0.0), jnp.where(lo, qr, 0.0)]
        else:
            rows += [jnp.where(lo, 0.0, qr), jnp.where(lo, 0.0, qb)]
    qst = jnp.concatenate(rows, axis=0).astype(BF16)
    s = lax.dot_general(qst, kcat.astype(BF16), (((1,), (1,)), ((), ())), preferred_element_type=F32)
    if bias is not None:
        cols, at = [], 0
        for off, blk_bias in bias:
            if off > at:
                cols.append(s[:, at:off])
            cols.append(s[:, off:off + LANES] + jnp.concatenate([blk_bias] * N_HEADS, axis=0))
            at = off + LANES
        cols.append(s[:, at:])
        s = jnp.concatenate(cols, axis=1)
    ps, es = [], []
    for h in range(N_HEADS):
        sh = s[h * M:(h + 1) * M]
        mh = jnp.maximum(jnp.max(sh, axis=-1, keepdims=True), sink_ref[h])
        ps.append(jnp.exp(sh - mh).astype(BF16))
        es.append(jnp.exp(sink_ref[h] - mh))
    vaug = jnp.concatenate([vcat.astype(BF16), jnp.ones(vcat.shape, BF16)], axis=1)
    oa = _dot(jnp.concatenate(ps, axis=0), vaug)
    o = [oa[h * M:(h + 1) * M, :LANES] * (1.0 / (oa[h * M:(h + 1) * M, LANES:] + es[h])) for h in range(N_HEADS)]
    for blk in range(N_HEADS * HEAD_DIM // LANES):
        kv = (2 * blk) // heads_per_kv
        oa_, ob_ = o[2 * blk], o[2 * blk + 1]
        if kv == 0:
            out = jnp.where(lo, oa_, pltpu.roll(ob_, HEAD_DIM, 1))
        else:
            out = jnp.where(lo, pltpu.roll(oa_, HEAD_DIM, 1), ob_)
        o_ref[:, blk * LANES:(blk + 1) * LANES] = out.astype(o_ref.dtype)


def _ctx_attn_kernel(sink_ref, q_ref, k_ref, v_ref, o_ref):
    _attend(q_ref[...], k_ref[...], v_ref[...], sink_ref, None, o_ref)


def _context_attention(q, k, v, sink, B, T):
    R, dc = q.shape
    kvw = k.shape[1]
    nq = T // BLOCK
    return pl.pallas_call(
        _ctx_attn_kernel,
        grid=(B, nq),
        in_specs=[pl.BlockSpec(memory_space=pltpu.SMEM),
                  pl.BlockSpec((BLOCK, dc), lambda b, i: (b * nq + i, 0)),
                  pl.BlockSpec((T, kvw), lambda b, i: (b, 0)),
                  pl.BlockSpec((T, kvw), lambda b, i: (b, 0))],
        out_specs=pl.BlockSpec((BLOCK, dc), lambda b, i: (b * nq + i, 0)),
        out_shape=jax.ShapeDtypeStruct((R, dc), BF16),
        compiler_params=_params("parallel", "parallel"), name="context_attention",
    )(sink, q, k, v)


def _win_attn_kernel(sink_ref, q_ref, kp_ref, kc_ref, kn_ref, vp_ref, vc_ref, vn_ref, ck_ref, cv_ref, o_ref, *,
                     nblk, nq):
    i = pl.program_id(1)
    k_blocks = [kp_ref[...]] + [kc_ref[n * BLOCK:(n + 1) * BLOCK] for n in range(nq)] + [kn_ref[...]]
    v_blocks = [vp_ref[...]] + [vc_ref[n * BLOCK:(n + 1) * BLOCK] for n in range(nq)] + [vn_ref[...]]
    qi = lax.broadcasted_iota(jnp.int32, (BLOCK, BLOCK), 0)
    kj = lax.broadcasted_iota(jnp.int32, (BLOCK, BLOCK), 1)
    edge = [jnp.where(i > 0, 0, BLOCK)] + [0] * nq + [jnp.where(nq * (i + 1) < nblk, 0, BLOCK)]
    for n in range(nq):
        kcat = jnp.concatenate(k_blocks[n:n + 3] + [ck_ref[...]], axis=0)
        vcat = jnp.concatenate(v_blocks[n:n + 3] + [cv_ref[...]], axis=0)
        prev_bias = jnp.where(qi + edge[n] <= kj, 0.0, -jnp.inf)
        next_bias = jnp.where(kj + edge[n + 2] <= qi, 0.0, -jnp.inf)
        _attend(q_ref[n * BLOCK:(n + 1) * BLOCK, :], kcat, vcat, sink_ref,
                [(0, prev_bias), (2 * BLOCK, next_bias)], o_ref.at[n * BLOCK:(n + 1) * BLOCK, :])


def _window_attention(q, k, v, cache_k4, cache_v4, layer, sink, B, T):
    R, dc = q.shape
    kvw = k.shape[1]
    nblk = T // BLOCK
    past = cache_k4.shape[2]
    nq = 4 if nblk % 4 == 0 else 2
    ngrp = nblk // nq
    assert nblk == nq * ngrp
    cur = lambda b, i: (b * ngrp + i, 0)
    prv = lambda b, i: (b * nblk + jnp.maximum(nq * i - 1, 0), 0)
    nxt = lambda b, i: (b * nblk + jnp.minimum(nq * (i + 1), nblk - 1), 0)
    kvs = lambda f: pl.BlockSpec((BLOCK, kvw), f)
    kv2 = pl.BlockSpec((nq * BLOCK, kvw), cur)
    ctx = pl.BlockSpec((None, None, past, kvw), lambda b, i: (b, layer, 0, 0))
    return pl.pallas_call(
        functools.partial(_win_attn_kernel, nblk=nblk, nq=nq),
        grid=(B, ngrp),
        in_specs=[pl.BlockSpec(memory_space=pltpu.SMEM),
                  pl.BlockSpec((nq * BLOCK, dc), cur),
                  kvs(prv), kv2, kvs(nxt), kvs(prv), kv2, kvs(nxt), ctx, ctx],
        out_specs=pl.BlockSpec((nq * BLOCK, dc), cur),
        out_shape=jax.ShapeDtypeStruct((R, dc), BF16),
        compiler_params=_params("parallel", "parallel"), name="window_attention",
    )(sink, q, k, k, k, v, v, v, cache_k4, cache_v4)


def _merge_kernel(x_ref, mod_ref, gt_ref, ga_ref, pb_ref, ao_ref, wglu_ref, wb_ref, wc_ref, wo_ref, g_ref, o_ref):
    dm = x_ref.shape[1]
    z = _dot(ga_ref[...].astype(BF16), wglu_ref[...])
    ya = z[:, :dm] * _sigmoid(z[:, dm:])
    yb = _dot(pb_ref[...], wb_ref[...])
    yc = _dot(ao_ref[...], wc_ref[...])
    gts = _sigmoid(gt_ref[...].astype(F32))
    merged = gts[:, 0:dm] * ya + gts[:, dm:2 * dm] * yb + gts[:, 2 * dm:3 * dm] * yc
    mo = _dot(merged.astype(BF16), wo_ref[...])
    m = mod_ref[...]
    o_ref[...] = x_ref[...] + m[2:3] * _rms(mo, g_ref[...])


def _merge(x2, mod, gates, ga_t, pb, ao, w_glu, w_b_out, w_c_out, w_o, g_post, B, T):
    R, D = x2.shape
    tm = min(512, T)
    nt = T // tm
    bm = mod.shape[0]
    da = w_glu[0].shape[1]
    mod_map = (lambda i: (i // nt, 0, 0)) if bm > 1 else (lambda i: (0, 0, 0))
    row = lambda w: pl.BlockSpec((tm, w), lambda i: (i, 0))
    return pl.pallas_call(
        _merge_kernel,
        grid=(R // tm,),
        in_specs=[row(D), pl.BlockSpec((None, 8, D), mod_map), row(gates.shape[1]),
                  pl.BlockSpec((tm, da), lambda i: (i % nt, i // nt)),
                  row(pb.shape[1]), row(ao.shape[1]),
                  _wspec(w_glu), _wspec(w_b_out), _wspec(w_c_out), _wspec(w_o),
                  pl.BlockSpec((1, D), lambda i: (0, 0))],
        out_specs=row(D),
        out_shape=jax.ShapeDtypeStruct((R, D), F32),
        compiler_params=_params("parallel"), name="merge_out_proj",
    )(x2, mod, gates, ga_t, pb, ao, w_glu[0], w_b_out[0], w_c_out[0], w_o[0], g_post)


def _ffn_kernel(*refs, halo, nt, tf):
    if halo:
        (x_ref, xp_ref, xn_ref, mod_ref, gpre_ref, wu_ref, cw_ref, cb_ref, wd_ref, gpost_ref,
         o_ref, h_ref, z_ref, a_ref, y_ref) = refs
    else:
        (x_ref, mod_ref, gpre_ref, wu_ref, cw_ref, cb_ref, wd_ref, gpost_ref,
         o_ref, h_ref, z_ref, a_ref, y_ref) = refs
    i = pl.program_id(0)
    tm = x_ref.shape[0]
    dff = wd_ref.shape[0]
    m = mod_ref[...]
    norm = lambda x: _rms(x, gpre_ref[...]) * (1.0 + m[4:5]) + m[3:4]
    h_ref[HALO:HALO + tm, :] = norm(x_ref[...]).astype(BF16)
    if halo:
        keep_prev = jnp.where(i % nt == 0, 0.0, 1.0)
        keep_next = jnp.where(i % nt == nt - 1, 0.0, 1.0)
        h_ref[0:HALO, :] = (norm(xp_ref[...]) * keep_prev).astype(BF16)
        h_ref[HALO + tm:, :] = (norm(xn_ref[...]) * keep_next).astype(BF16)
    else:
        h_ref[0:HALO, :] = jnp.zeros((HALO, h_ref.shape[1]), BF16)
        h_ref[HALO + tm:, :] = jnp.zeros((HALO, h_ref.shape[1]), BF16)
    hx = h_ref[...]
    half = tm // 2
    nl = tf // LANES

    def conv(slot, col):
        z = _dot(hx, wu_ref[:, col:col + tf])
        ev, od = [], []
        for k in range(nl):
            sl = slot * nl + k
            z_ref[sl] = z[:, k * LANES:(k + 1) * LANES]
            taps = [z_ref[sl, pl.ds(HALO - 1 + d, half, stride=2), :] for d in range(4)]
            cw = cw_ref[:, col + k * LANES:col + (k + 1) * LANES]
            cb = cb_ref[:, col + k * LANES:col + (k + 1) * LANES]
            ev.append(taps[0] * cw[0:1] + taps[1] * cw[1:2] + taps[2] * cw[2:3] + cb)
            od.append(taps[1] * cw[0:1] + taps[2] * cw[1:2] + taps[3] * cw[2:3] + cb)
        return jnp.concatenate(ev, axis=1), jnp.concatenate(od, axis=1)

    for c in range(dff // tf):
        g_e, g_o = conv(2 * (c % 2), c * tf)
        v_e, v_o = conv(2 * (c % 2) + 1, dff + c * tf)
        a_ref[0:half, c * tf:(c + 1) * tf] = (jax.nn.silu(g_e) * v_e).astype(BF16)
        a_ref[half:tm, c * tf:(c + 1) * tf] = (jax.nn.silu(g_o) * v_o).astype(BF16)
    y = _dot(a_ref[...], wd_ref[...])
    for k in range(y.shape[1] // LANES):
        y_ref[k, pl.ds(0, half, stride=2), :] = y[0:half, k * LANES:(k + 1) * LANES]
        y_ref[k, pl.ds(1, half, stride=2), :] = y[half:tm, k * LANES:(k + 1) * LANES]
    y = jnp.concatenate([y_ref[k] for k in range(y.shape[1] // LANES)], axis=1)
    o_ref[...] = x_ref[...] + m[5:6] * _rms(y, gpost_ref[...])


def _conv_ffn(x2, mod, g_pre, w_up, conv_w, conv_b, w_down, g_post, B, T):
    R, D = x2.shape
    dff = w_down[0].shape[1]
    tf = 256
    tm = min(512, T)
    nt = T // tm
    halo = nt > 1
    bm = mod.shape[0]
    mod_map = (lambda i: (i // nt, 0, 0)) if bm > 1 else (lambda i: (0, 0, 0))
    hb = tm // HALO
    nhb = R // HALO
    const = lambda i: (0, 0)
    whole = lambda a: pl.BlockSpec(a.shape, const, pipeline_mode=pl.Buffered(1))
    in_specs = [pl.BlockSpec((tm, D), lambda i: (i, 0))]
    args = [x2]
    if halo:
        in_specs += [pl.BlockSpec((HALO, D), lambda i: (jnp.maximum(i * hb - 1, 0), 0)),
                     pl.BlockSpec((HALO, D), lambda i: (jnp.minimum((i + 1) * hb, nhb - 1), 0))]
        args += [x2, x2]
    in_specs += [pl.BlockSpec((None, 8, D), mod_map), pl.BlockSpec((1, D), const),
                 _wspec(w_up), whole(conv_w), whole(conv_b), _wspec(w_down), pl.BlockSpec((1, D), const)]
    args += [mod, g_pre, w_up[0], conv_w, conv_b, w_down[0], g_post]
    return pl.pallas_call(
        functools.partial(_ffn_kernel, halo=halo, nt=nt, tf=tf),
        grid=(R // tm,),
        in_specs=in_specs,
        out_specs=pl.BlockSpec((tm, D), lambda i: (i, 0)),
        out_shape=jax.ShapeDtypeStruct((R, D), F32),
        scratch_shapes=[pltpu.VMEM((tm + 2 * HALO, D), BF16),
                        pltpu.VMEM((4 * tf // LANES, tm + 2 * HALO, LANES), F32),
                        pltpu.VMEM((tm, dff), BF16),
                        pltpu.VMEM((D // LANES, tm, LANES), F32)],
        compiler_params=_params("parallel"), name="conv_ffn",
    )(*args)


def _rope_tables(T):
    t = jnp.arange(T)
    row = (t // GRID_W).astype(F32)[:, None]
    col = (t % GRID_W).astype(F32)[:, None]
    half = HEAD_DIM // 2
    inv = ROPE_BASE ** (-jnp.arange(0, half, 2, dtype=F32) / half)
    lane = jnp.arange(LANES)
    freq = inv[lane % (half // 2)][None, :]
    ang = jnp.where(((lane % HEAD_DIM) < half)[None, :], row * freq, col * freq)
    sign = jnp.where((lane % half) < half // 2, -1.0, 1.0)[None, :]
    return jnp.cos(ang), jnp.sin(ang) * sign


def _layer(x2, mod, lp, B, T, ctx):
    D = x2.shape[1]
    rope_tabs = None if ctx is None else lp['rope']
    a_t, uv, q, k, v, gates = _in_proj(x2, mod, lp['g_pre_mix'], lp['w_in'], B, T, lp['dims'], rope_tabs)
    h0 = (None, None) if ctx is None else ctx[3]
    tabs, row0 = lp['s5_tabs']
    y_bwd, br_re, br_im = _s5_pass(a_t, (tabs, row0 + 1), B, T, True, h0=h0[1])
    ga_t, fw_re, fw_im = _s5_pass(a_t, (tabs, row0), B, T, False, h0=h0[0], fuse=(y_bwd, lp['d_skip']))
    pb = _sgu(uv, lp['g_sgu'], lp['w_spatial'], lp['sgu_bias'])
    if ctx is None:
        ao = _context_attention(q, k, v, lp['sink'], B, T)
    else:
        ao = _window_attention(q, k, v, ctx[0], ctx[1], ctx[2], lp['sink'], B, T)
    x2 = _merge(x2, mod, gates, ga_t.reshape(T, -1), pb, ao, lp['w_glu'], lp['w_b_out'], lp['w_c_out'],
                lp['w_o'], lp['g_post_mix'], B, T)
    x2 = _conv_ffn(x2, mod, lp['g_pre_ffn'], lp['w_up'], lp['conv_w'], lp['conv_b'], lp['w_down'],
                   lp['g_post_ffn'], B, T)
    return x2, (k, v, (fw_re, br_re), (fw_im, br_im))


def kernel(x_prompt, x_sample, cache_k, cache_v, state_ssm_re, state_ssm_im, c, c_ctx,
           w_mod, b_mod, g_pre_mix, g_post_mix, g_pre_ffn, g_post_ffn, w_in,
           lam_re, lam_im, log_step, b_re, b_im, c_re, c_im, d_skip, w_glu,
           g_sgu, w_spatial, b_spatial, w_b_out, sink, w_c_out, w_o,
           w_up, conv_w, conv_b, w_down):
    BP, TP, D = x_prompt.shape
    BS, TS, _ = x_sample.shape
    L = w_in.shape[0]
    G, N = lam_re.shape[2], lam_re.shape[3]
    P = b_re.shape[-1]
    d_a = G * P
    d_b2 = 2 * g_sgu.shape[1]
    d_c = w_c_out.shape[1]
    kv_w = N_KV * HEAD_DIM
    d_g = w_in.shape[2] - (d_a + d_b2 + d_c + 2 * kv_w)
    past = cache_k.shape[2]

    nc = 16
    cvecs = jnp.zeros((nc, D), F32).at[0].set(c_ctx).at[1:1 + BS].set(c)
    mod_all = _modulation(cvecs, w_mod, b_mod).reshape(L, nc, 6, D)
    mod_all = jnp.pad(mod_all, ((0, 0), (0, 0), (0, 2), (0, 0)))

    rope = _rope_tables(TS)
    cache_k4 = cache_k.reshape(BS, L, past, kv_w)
    cache_v4 = cache_v.reshape(BS, L, past, kv_w)

    xp = x_prompt.reshape(BP * TP, D)
    xs = x_sample.reshape(BS * TS, D)
    new_k, new_v, new_re, new_im = [], [], [], []
    s5_tabs = _s5_tables(_s5_discretise(lam_re, lam_im, log_step, b_re, b_im, c_re, c_im), G, N, P)
    wb16 = {name: w.astype(BF16) for name, w in (
        ('w_in', w_in), ('w_glu', w_glu), ('w_spatial', w_spatial), ('w_b_out', w_b_out),
        ('w_c_out', w_c_out), ('w_o', w_o), ('w_up', w_up), ('w_down', w_down))}
    for l in range(L):
        tabs = (s5_tabs, 2 * l)
        lp = {
            'dims': (d_a, d_b2, d_c, kv_w, d_g),
            'g_pre_mix': g_pre_mix[l][None], 'g_post_mix': g_post_mix[l][None],
            'g_pre_ffn': g_pre_ffn[l][None], 'g_post_ffn': g_post_ffn[l][None],
            's5_tabs': tabs, 'd_skip': d_skip[l][None], 'g_sgu': g_sgu[l][None],
            'sgu_bias': jnp.repeat(b_spatial[l].T, g_sgu.shape[1] // b_spatial.shape[1], axis=1),
            'sink': sink[l], 'conv_w': conv_w[l], 'conv_b': conv_b[l][None], 'rope': rope,
        }
        lp.update({name: (w, l) for name, w in wb16.items()})
        xp, (k_l, v_l, f_re, f_im) = _layer(xp, mod_all[l, 0:1], lp, BP, TP, None)
        new_k.append(k_l.reshape(BP, TP, N_KV, HEAD_DIM))
        new_v.append(v_l.reshape(BP, TP, N_KV, HEAD_DIM))
        new_re.append(jnp.stack([f.reshape(BP, G, N) for f in f_re], axis=1))
        new_im.append(jnp.stack([f.reshape(BP, G, N) for f in f_im], axis=1))
        h0 = [(state_ssm_re[:, l, d].reshape(BS, G * N), state_ssm_im[:, l, d].reshape(BS, G * N)) for d in range(2)]
        xs, _ = _layer(xs, mod_all[l, 1:1 + BS], lp, BS, TS, (cache_k4, cache_v4, l, h0))
    return (xp.reshape(BP, TP, D), xs.reshape(BS, TS, D),
            jnp.stack(new_k, axis=1), jnp.stack(new_v, axis=1),
            jnp.stack(new_re, axis=1), jnp.stack(new_im, axis=1))
```

```python
import functools
import math

import jax
import jax.numpy as jnp
from jax import lax
from jax.experimental import pallas as pl
from jax.experimental.pallas import tpu as pltpu

F32 = jnp.float32
BF16 = jnp.bfloat16

GRID_W = 64
SSM_P = 16
SSM_N = 64
CHUNK = 128
N_HEADS = 8
N_KV = 2
HEAD_DIM = 64
BLOCK = 128
ROPE_BASE = 10000.0
EPS = 1e-6

LANES = 128
SUBLANES = 8
HALO = 16
VMEM_LIMIT = 56 * 1024 * 1024


def _params(*sem):
    return pltpu.CompilerParams(dimension_semantics=sem, vmem_limit_bytes=VMEM_LIMIT)


def _dot(a, b):
    return jnp.dot(a, b, preferred_element_type=F32)


def _wspec(w):
    arr, layer = w
    idx = (layer,) + (0,) * (arr.ndim - 1)
    return pl.BlockSpec((None,) + arr.shape[1:], lambda *_: idx, pipeline_mode=pl.Buffered(1))


def _sigmoid(x):
    return 0.5 * jnp.tanh(0.5 * x) + 0.5


def _rms(x, g):
    return x * lax.rsqrt(jnp.mean(x * x, axis=-1, keepdims=True) + EPS) * g


def _mod_kernel(c_ref, w_ref, b_ref, o_ref):
    a = jax.nn.silu(c_ref[...]).astype(BF16)
    o_ref[...] = _dot(a, w_ref[...].astype(BF16)) + b_ref[...]


def _modulation(cvecs, w_mod, b_mod):
    L, D, D6 = w_mod.shape
    NC = cvecs.shape[0]
    tn = D6 // 4
    return pl.pallas_call(
        _mod_kernel,
        grid=(L, D6 // tn),
        in_specs=[pl.BlockSpec((NC, D), lambda l, j: (0, 0)),
                  pl.BlockSpec((None, D, tn), lambda l, j: (l, 0, j)),
                  pl.BlockSpec((None, 1, tn), lambda l, j: (l, 0, j))],
        out_specs=pl.BlockSpec((None, NC, tn), lambda l, j: (l, 0, j)),
        out_shape=jax.ShapeDtypeStruct((L, NC, D6), F32),
        compiler_params=_params("parallel", "parallel"),
        name="modulation",
    )(cvecs, w_mod, b_mod.reshape(L, 1, D6))


def _rope(x, cos, sin):
    lane = lax.broadcasted_iota(jnp.int32, cos.shape, 1)
    first = (lane % 32) < 16
    outs = []
    for j in range(x.shape[1] // LANES):
        xb = x[:, j * LANES:(j + 1) * LANES]
        partner = jnp.where(first, pltpu.roll(xb, LANES - 16, 1), pltpu.roll(xb, 16, 1))
        outs.append(xb * cos + partner * sin)
    return outs[0] if len(outs) == 1 else jnp.concatenate(outs, axis=1)


INPROJ_TT = 64
INPROJ_PITCH = INPROJ_TT + 8


def _inproj_kernel(*refs, rope, offs):
    if rope:
        (x_ref, mod_ref, g_ref, w_ref, cos_ref, sin_ref,
         a_ref, uv_ref, q_ref, k_ref, v_ref, gt_ref, a_scr) = refs
    else:
        x_ref, mod_ref, g_ref, w_ref, a_ref, uv_ref, q_ref, k_ref, v_ref, gt_ref, a_scr = refs
    ob, oq, ok, ov, og, dn = offs
    bs, tt, D = x_ref.shape
    rows = bs * tt
    m = mod_ref[...]
    x = x_ref[...]
    h = x * lax.rsqrt(jnp.mean(x * x, axis=-1, keepdims=True) + EPS) * g_ref[...]
    h = h * (1.0 + m[:, 1:2, :]) + m[:, 0:1, :]
    hb = h.reshape(rows, D).astype(BF16)

    a = _dot(hb, w_ref[:, 0:ob])
    for b in range(bs):
        for j in range(ob // LANES):
            a_scr[j, b * INPROJ_PITCH:b * INPROJ_PITCH + tt, :] = a[b * tt:(b + 1) * tt, j * LANES:(j + 1) * LANES]
    for t in range(tt):
        for j in range(ob // LANES):
            a_ref[t // 2, t % 2, :, j * LANES:(j + 1) * LANES] = a_scr[j, pl.ds(t, bs, stride=INPROJ_PITCH), :]

    uv_ref[...] = _dot(hb, w_ref[:, ob:oq]).astype(BF16).reshape(bs, tt, oq - ob)
    q = _dot(hb, w_ref[:, oq:ok])
    k = _dot(hb, w_ref[:, ok:ov])
    if rope:
        cos = jnp.concatenate([cos_ref[...]] * bs, axis=0)
        sin = jnp.concatenate([sin_ref[...]] * bs, axis=0)
        q = _rope(q, cos, sin)
        k = _rope(k, cos, sin)
    q_ref[...] = q.reshape(bs, tt, ok - oq)
    k_ref[...] = k.reshape(bs, tt, ov - ok)
    v_ref[...] = _dot(hb, w_ref[:, ov:og]).reshape(bs, tt, og - ov)
    gt_ref[...] = _dot(hb, w_ref[:, og:dn]).astype(BF16).reshape(bs, tt, dn - og)


def _in_proj(x2, mod, g_pre, w_in, B, T, dims, rope_tabs=None):
    R, D = x2.shape
    d_a, d_b2, d_c, kv_w, d_g = dims
    ob = d_a
    oq = ob + d_b2
    ok = oq + d_c
    ov = ok + kv_w
    og = ov + kv_w
    dn = og + d_g
    bs, tt = SUBLANES, INPROJ_TT
    bm = mod.shape[0]
    mod_spec = (pl.BlockSpec((bs, 8, D), lambda g, i: (g, 0, 0)) if bm > 1
                else pl.BlockSpec((1, 8, D), lambda g, i: (0, 0, 0)))
    in_specs = [pl.BlockSpec((bs, tt, D), lambda g, i: (g, i, 0)),
                mod_spec,
                pl.BlockSpec((1, D), lambda g, i: (0, 0)),
                _wspec(w_in)]
    args = [x2.reshape(B, T, D), mod, g_pre, w_in[0]]
    if rope_tabs is not None:
        in_specs += [pl.BlockSpec((tt, LANES), lambda g, i: (i, 0))] * 2
        args += list(rope_tabs)
    seq = lambda w: pl.BlockSpec((bs, tt, w), lambda g, i: (g, i, 0))
    out_specs = [pl.BlockSpec((tt // 2, 2, bs, d_a), lambda g, i: (i, 0, g, 0)),
                 seq(d_b2), seq(d_c), seq(kv_w), seq(kv_w), seq(d_g)]
    out_shape = [jax.ShapeDtypeStruct((T // 2, 2, B, d_a), F32),
                 jax.ShapeDtypeStruct((B, T, d_b2), BF16),
                 jax.ShapeDtypeStruct((B, T, d_c), F32),
                 jax.ShapeDtypeStruct((B, T, kv_w), F32),
                 jax.ShapeDtypeStruct((B, T, kv_w), F32),
                 jax.ShapeDtypeStruct((B, T, d_g), BF16)]
    a4, uv, q, k, v, gates = pl.pallas_call(
        functools.partial(_inproj_kernel, rope=rope_tabs is not None, offs=(ob, oq, ok, ov, og, dn)),
        grid=(B // bs, T // tt), in_specs=in_specs, out_specs=out_specs, out_shape=out_shape,
        scratch_shapes=[pltpu.VMEM((d_a // LANES, bs * INPROJ_PITCH, LANES), F32)],
        compiler_params=_params("parallel", "parallel"), name="in_proj",
    )(*args)
    flat = lambda y: y.reshape(R, y.shape[-1])
    return a4, flat(uv), flat(q), flat(k), flat(v), flat(gates)


def _s5_disc_kernel(lr_ref, li_ref, ls_ref, br_ref, bi_ref, cr_ref, ci_ref,
                    l2r_ref, l2i_ref, bbr_ref, bbi_ref, lbbr_ref, lbbi_ref,
                    c1r_ref, c1i_ref, c2r_ref, c2i_ref, cb_ref, clb_ref):
    lr = lr_ref[...]
    li = li_ref[...]
    dt = jnp.exp(ls_ref[...])
    ar = lr * dt
    ai = li * dt
    mag = jnp.exp(ar)
    lb_re = (mag * jnp.cos(ai))[:, None, :]
    lb_im = (mag * jnp.sin(ai))[:, None, :]
    lr3 = lr[:, None, :]
    li3 = li[:, None, :]
    den = lr3 * lr3 + li3 * li3
    f_re = ((lb_re - 1.0) * lr3 + lb_im * li3) / den
    f_im = (lb_im * lr3 - (lb_re - 1.0) * li3) / den
    l2_re = lb_re * lb_re - lb_im * lb_im
    l2_im = 2.0 * lb_re * lb_im
    l2r_ref[...] = l2_re
    l2i_ref[...] = l2_im
    br = br_ref[...]
    bi = bi_ref[...]
    bb_re = f_re * br - f_im * bi
    bb_im = f_re * bi + f_im * br
    lbb_re = lb_re * bb_re - lb_im * bb_im
    lbb_im = lb_re * bb_im + lb_im * bb_re
    bbr_ref[...] = bb_re
    bbi_ref[...] = bb_im
    lbbr_ref[...] = lbb_re
    lbbi_ref[...] = lbb_im
    cr = cr_ref[...]
    ci = ci_ref[...]
    c1r_ref[...] = cr * lb_re - ci * lb_im
    c1i_ref[...] = cr * lb_im + ci * lb_re
    c2r_ref[...] = cr * l2_re - ci * l2_im
    c2i_ref[...] = cr * l2_im + ci * l2_re
    for q in range(br.shape[1]):
        cb_ref[:, q, :] = jnp.sum(cr * bb_re[:, q:q + 1, :] - ci * bb_im[:, q:q + 1, :], axis=-1)
        clb_ref[:, q, :] = jnp.sum(cr * lbb_re[:, q:q + 1, :] - ci * lbb_im[:, q:q + 1, :], axis=-1)


def _s5_discretise(lam_re, lam_im, log_step, b_re, b_im, c_re, c_im):
    N = lam_re.shape[-1]
    P = b_re.shape[-1]
    rows = lam_re.size // N
    tr = lambda b: jnp.swapaxes(b, -1, -2).reshape(rows, P, N)
    vec = jax.ShapeDtypeStruct((rows, 1, N), F32)
    mat = jax.ShapeDtypeStruct((rows, P, N), F32)
    sq = jax.ShapeDtypeStruct((rows, P, P), F32)
    return pl.pallas_call(_s5_disc_kernel, out_shape=[vec, vec] + [mat] * 8 + [sq, sq], name="s5_discretise")(
        lam_re.reshape(rows, N), lam_im.reshape(rows, N), log_step.reshape(rows, 1), tr(b_re), tr(b_im),
        c_re.reshape(rows, P, N), c_im.reshape(rows, P, N))


def _s5_tables(disc, G, N, P):
    l2_re, l2_im, bb_re, bb_im, lbb_re, lbb_im, c1_re, c1_im, c2_re, c2_im, cb, clb = disc
    GB = LANES // P
    NB = G // GB
    X = l2_re.shape[0] // G
    eye = jnp.eye(GB, dtype=F32)[None, None, :, None, :, None]

    def place(x5):
        r, c = x5.shape[-2:]
        return (x5[:, :, :, :, None, :] * eye).reshape(X, NB, GB * r, GB * c)

    def w_in(x):
        return place(x.reshape(X, NB, GB, P, N))

    def w_out(x):
        return place(jnp.swapaxes(x.reshape(X, NB, GB, P, N), -1, -2))

    def w_dir(x):
        return place(x.reshape(X, NB, GB, P, P))

    w_x = jnp.concatenate([jnp.concatenate([w_in(lbb_re), w_in(lbb_im)], axis=-1),
                           jnp.concatenate([w_in(bb_re), w_in(bb_im)], axis=-1)], axis=2).astype(BF16)
    w_y = jnp.concatenate([jnp.concatenate([w_out(c1_re), -w_out(c1_im)], axis=2),
                           jnp.concatenate([w_out(c2_re), -w_out(c2_im)], axis=2)], axis=-1).astype(BF16)
    d0, d1 = w_dir(cb), w_dir(clb)
    w_u = jnp.concatenate([jnp.concatenate([d0, d1], axis=-1),
                           jnp.concatenate([jnp.zeros_like(d0), d0], axis=-1)], axis=2).astype(BF16)
    lam2 = jnp.concatenate([l2_re.reshape(X, NB, 1, GB * N), l2_im.reshape(X, NB, 1, GB * N)], axis=-1)
    return w_x, w_y, w_u, lam2


def _s5_kernel(*refs, reverse, has_h0, fuse, nb, sw):
    it = iter(refs)
    u_ref, wx_ref, wy_ref, wu_ref, lam_ref = next(it), next(it), next(it), next(it), next(it)
    if has_h0:
        h0r_ref, h0i_ref = next(it), next(it)
    if fuse:
        yb_ref, dsk_ref = next(it), next(it)
    y_ref, fr_ref, fi_ref = next(it), next(it), next(it)
    xs_ref, st_ref = next(it), next(it)
    if fuse:
        yacc_ref = next(it)

    tc, _, bs, _ = u_ref.shape
    rows = tc * bs
    c = pl.program_id(1)
    first, second = (1, 0) if reverse else (0, 1)

    @pl.when(c == 0)
    def _():
        for j in range(nb):
            if has_h0:
                st_ref[j, :, 0:sw] = h0r_ref[:, j * sw:(j + 1) * sw]
                st_ref[j, :, sw:2 * sw] = h0i_ref[:, j * sw:(j + 1) * sw]
            else:
                st_ref[j] = jnp.zeros(st_ref.shape[1:], F32)

    for j in range(nb):
        cs = slice(j * LANES, (j + 1) * LANES)
        u2 = jnp.concatenate([u_ref[:, first, :, cs].reshape(rows, LANES),
                              u_ref[:, second, :, cs].reshape(rows, LANES)], axis=1).astype(BF16)
        xs_ref[j] = _dot(u2, wx_ref[j]).reshape(tc, bs, 2 * sw)
        lr = jnp.broadcast_to(lam_ref[j, :, 0:sw], (bs, sw))
        li = jnp.broadcast_to(lam_ref[j, :, sw:2 * sw], (bs, sw))
        s_re, s_im = st_ref[j, :, 0:sw], st_ref[j, :, sw:2 * sw]
        for i in range(tc):
            t = (tc - 1 - i) if reverse else i
            x_re = xs_ref[j, t, :, 0:sw]
            x_im = xs_ref[j, t, :, sw:2 * sw]
            xs_ref[j, t, :, 0:sw] = s_re
            xs_ref[j, t, :, sw:2 * sw] = s_im
            s_re, s_im = lr * s_re - li * s_im + x_re, lr * s_im + li * s_re + x_im
        st_ref[j, :, 0:sw] = s_re
        st_ref[j, :, sw:2 * sw] = s_im
        fr_ref[:, j * sw:(j + 1) * sw] = s_re
        fi_ref[:, j * sw:(j + 1) * sw] = s_im
        yj = _dot(xs_ref[j].reshape(rows, 2 * sw).astype(BF16), wy_ref[j]) + _dot(u2, wu_ref[j])
        for half, parity in ((0, first), (1, second)):
            yh = yj[:, half * LANES:(half + 1) * LANES]
            if fuse:
                yacc_ref[parity, :, cs] = yh
            else:
                y_ref[:, parity, :, cs] = yh.reshape(tc, bs, LANES)

    if fuse:
        da = nb * LANES
        for parity in range(2):
            y = (yacc_ref[parity] + yb_ref[:, parity].reshape(rows, da)
                 + dsk_ref[...] * u_ref[:, parity].reshape(rows, da))
            y_ref[:, parity] = jax.nn.gelu(y).reshape(tc, bs, da)


def _s5_pass(u4, tabs, B, T, reverse, h0=None, fuse=None):
    (w_x, w_y, w_u, lam2), which = tabs
    _, nb, _, sw2 = w_x.shape
    sw = sw2 // 2
    da = nb * LANES
    bs = SUBLANES
    tc = 64
    nct = T // (2 * tc)
    tmap = (lambda g, c: (nct - 1 - c, 0, g, 0)) if reverse else (lambda g, c: (c, 0, g, 0))
    pick = lambda g, c: (which, 0, 0, 0)
    blk = pl.BlockSpec((tc, 2, bs, da), tmap)
    in_specs = [blk] + [pl.BlockSpec((None,) + a.shape[1:], pick) for a in (w_x, w_y, w_u, lam2)]
    args = [u4, w_x, w_y, w_u, lam2]
    if h0 is not None:
        in_specs += [pl.BlockSpec((bs, nb * sw), lambda g, c: (g, 0))] * 2
        args += list(h0)
    scratch = [pltpu.VMEM((nb, tc, bs, sw2), F32), pltpu.VMEM((nb, bs, sw2), F32)]
    if fuse is not None:
        y_other, d_skip = fuse
        in_specs += [blk, pl.BlockSpec((1, da), lambda g, c: (0, 0))]
        args += [y_other, d_skip]
        scratch.append(pltpu.VMEM((2, tc * bs, da), F32))
    fin = jax.ShapeDtypeStruct((B, nb * sw), F32)
    y, f_re, f_im = pl.pallas_call(
        functools.partial(_s5_kernel, reverse=reverse, has_h0=h0 is not None, fuse=fuse is not None, nb=nb, sw=sw),
        grid=(B // bs, nct), in_specs=in_specs,
        out_specs=[blk,
                   pl.BlockSpec((bs, nb * sw), lambda g, c: (g, 0)),
                   pl.BlockSpec((bs, nb * sw), lambda g, c: (g, 0))],
        out_shape=[jax.ShapeDtypeStruct((T // 2, 2, B, da), F32), fin, fin],
        scratch_shapes=scratch,
        compiler_params=_params("parallel", "arbitrary"),
        name="s5_scan_gelu" if fuse is not None else "s5_scan",
    )(*args)
    return y, f_re, f_im


def _sgu_kernel(uv_ref, g_ref, ws_ref, bias_ref, o_ref, *, nch):
    db = g_ref.shape[1]
    x = jax.nn.gelu(uv_ref[...].astype(F32))
    u = x[:, :db]
    v = x[:, db:]
    mu = jnp.mean(v, axis=-1, keepdims=True)
    vc = v - mu
    var = jnp.mean(vc * vc, axis=-1, keepdims=True)
    vn = (vc * lax.rsqrt(var + EPS) * g_ref[...]).astype(BF16)
    lane = lax.broadcasted_iota(jnp.int32, (CHUNK, LANES), 1)
    lo = lane < (LANES // 2)
    bias = bias_ref[...]
    for c in range(nch):
        r0 = c * CHUNK
        parts = []
        for hp in range(db // LANES):
            vb = vn[r0:r0 + CHUNK, hp * LANES:(hp + 1) * LANES]
            parts.append(jnp.where(lo, _dot(ws_ref[2 * hp], vb), _dot(ws_ref[2 * hp + 1], vb)))
        mixed = jnp.concatenate(parts, axis=1) + bias
        o_ref[r0:r0 + CHUNK, :] = (u[r0:r0 + CHUNK] * mixed).astype(BF16)


def _sgu(uv, g_sgu, w_spatial, bias_full):
    R, db2 = uv.shape
    db = db2 // 2
    nch = 4
    tm = nch * CHUNK
    return pl.pallas_call(
        functools.partial(_sgu_kernel, nch=nch),
        grid=(R // tm,),
        in_specs=[pl.BlockSpec((tm, db2), lambda i: (i, 0)),
                  pl.BlockSpec((1, db), lambda i: (0, 0)),
                  _wspec(w_spatial),
                  pl.BlockSpec(bias_full.shape, lambda i: (0, 0))],
        out_specs=pl.BlockSpec((tm, db), lambda i: (i, 0)),
        out_shape=jax.ShapeDtypeStruct((R, db), BF16),
        compiler_params=_params("parallel"), name="chunk_sgu",
    )(uv, g_sgu, w_spatial[0], bias_full)


def _attend(q, kcat, vcat, sink_ref, bias, o_ref):
    assert N_KV * HEAD_DIM == LANES and kcat.shape[1] == LANES
    M = q.shape[0]
    scale = HEAD_DIM ** -0.5
    lane = lax.broadcasted_iota(jnp.int32, (M, LANES), 1)
    lo = lane < HEAD_DIM
    heads_per_kv = N_HEADS // N_KV
    rows = []
    for blk in range(N_HEADS * HEAD_DIM // LANES):
        kv = (2 * blk) // heads_per_kv
        qb = q[:, blk * LANES:(blk + 1) * LANES] * scale
        qr = pltpu.roll(qb, HEAD_DIM, 1)
        if kv == 0:
            rows += [jnp.where(lo, qb, 0.0), jnp.where(lo, qr, 0.0)]
        else:
            rows += [jnp.where(lo, 0.0, qr), jnp.where(lo, 0.0, qb)]
    qst = jnp.concatenate(rows, axis=0).astype(BF16)
    s = lax.dot_general(qst, kcat.astype(BF16), (((1,), (1,)), ((), ())), preferred_element_type=F32)
    if bias is not None:
        cols, at = [], 0
        for off, blk_bias in bias:
            if off > at:
                cols.append(s[:, at:off])
            cols.append(s[:, off:off + LANES] + jnp.concatenate([blk_bias] * N_HEADS, axis=0))
            at = off + LANES
        cols.append(s[:, at:])
        s = jnp.concatenate(cols, axis=1)
    ps, es = [], []
    for h in range(N_HEADS):
        sh = s[h * M:(h + 1) * M]
        mh = jnp.maximum(jnp.max(sh, axis=-1, keepdims=True), sink_ref[h])
        ps.append(jnp.exp(sh - mh).astype(BF16))
        es.append(jnp.exp(sink_ref[h] - mh))
    vaug = jnp.concatenate([vcat.astype(BF16), jnp.ones(vcat.shape, BF16)], axis=1)
    oa = _dot(jnp.concatenate(ps, axis=0), vaug)
    o = [oa[h * M:(h + 1) * M, :LANES] * (1.0 / (oa[h * M:(h + 1) * M, LANES:] + es[h])) for h in range(N_HEADS)]
    for blk in range(N_HEADS * HEAD_DIM // LANES):
        kv = (2 * blk) // heads_per_kv
        oa_, ob_ = o[2 * blk], o[2 * blk + 1]
        if kv == 0:
            out = jnp.where(lo, oa_, pltpu.roll(ob_, HEAD_DIM, 1))
        else:
            out = jnp.where(lo, pltpu.roll(oa_, HEAD_DIM, 1), ob_)
        o_ref[:, blk * LANES:(blk + 1) * LANES] = out.astype(o_ref.dtype)


def _ctx_attn_kernel(sink_ref, q_ref, k_ref, v_ref, o_ref):
    _attend(q_ref[...], k_ref[...], v_ref[...], sink_ref, None, o_ref)


def _context_attention(q, k, v, sink, B, T):
    R, dc = q.shape
    kvw = k.shape[1]
    nq = T // BLOCK
    return pl.pallas_call(
        _ctx_attn_kernel,
        grid=(B, nq),
        in_specs=[pl.BlockSpec(memory_space=pltpu.SMEM),
                  pl.BlockSpec((BLOCK, dc), lambda b, i: (b * nq + i, 0)),
                  pl.BlockSpec((T, kvw), lambda b, i: (b, 0)),
                  pl.BlockSpec((T, kvw), lambda b, i: (b, 0))],
        out_specs=pl.BlockSpec((BLOCK, dc), lambda b, i: (b * nq + i, 0)),
        out_shape=jax.ShapeDtypeStruct((R, dc), BF16),
        compiler_params=_params("parallel", "parallel"), name="context_attention",
    )(sink, q, k, v)


def _win_attn_kernel(sink_ref, q_ref, kp_ref, kc_ref, kn_ref, vp_ref, vc_ref, vn_ref, ck_ref, cv_ref, o_ref, *,
                     nblk, nq):
    i = pl.program_id(1)
    k_blocks = [kp_ref[...]] + [kc_ref[n * BLOCK:(n + 1) * BLOCK] for n in range(nq)] + [kn_ref[...]]
    v_blocks = [vp_ref[...]] + [vc_ref[n * BLOCK:(n + 1) * BLOCK] for n in range(nq)] + [vn_ref[...]]
    qi = lax.broadcasted_iota(jnp.int32, (BLOCK, BLOCK), 0)
    kj = lax.broadcasted_iota(jnp.int32, (BLOCK, BLOCK), 1)
    edge = [jnp.where(i > 0, 0, BLOCK)] + [0] * nq + [jnp.where(nq * (i + 1) < nblk, 0, BLOCK)]
    for n in range(nq):
        kcat = jnp.concatenate(k_blocks[n:n + 3] + [ck_ref[...]], axis=0)
        vcat = jnp.concatenate(v_blocks[n:n + 3] + [cv_ref[...]], axis=0)
        prev_bias = jnp.where(qi + edge[n] <= kj, 0.0, -jnp.inf)
        next_bias = jnp.where(kj + edge[n + 2] <= qi, 0.0, -jnp.inf)
        _attend(q_ref[n * BLOCK:(n + 1) * BLOCK, :], kcat, vcat, sink_ref,
                [(0, prev_bias), (2 * BLOCK, next_bias)], o_ref.at[n * BLOCK:(n + 1) * BLOCK, :])


def _window_attention(q, k, v, cache_k4, cache_v4, layer, sink, B, T):
    R, dc = q.shape
    kvw = k.shape[1]
    nblk = T // BLOCK
    past = cache_k4.shape[2]
    nq = 4 if nblk % 4 == 0 else 2
    ngrp = nblk // nq
    assert nblk == nq * ngrp
    cur = lambda b, i: (b * ngrp + i, 0)
    prv = lambda b, i: (b * nblk + jnp.maximum(nq * i - 1, 0), 0)
    nxt = lambda b, i: (b * nblk + jnp.minimum(nq * (i + 1), nblk - 1), 0)
    kvs = lambda f: pl.BlockSpec((BLOCK, kvw), f)
    kv2 = pl.BlockSpec((nq * BLOCK, kvw), cur)
    ctx = pl.BlockSpec((None, None, past, kvw), lambda b, i: (b, layer, 0, 0))
    return pl.pallas_call(
        functools.partial(_win_attn_kernel, nblk=nblk, nq=nq),
        grid=(B, ngrp),
        in_specs=[pl.BlockSpec(memory_space=pltpu.SMEM),
                  pl.BlockSpec((nq * BLOCK, dc), cur),
                  kvs(prv), kv2, kvs(nxt), kvs(prv), kv2, kvs(nxt), ctx, ctx],
        out_specs=pl.BlockSpec((nq * BLOCK, dc), cur),
        out_shape=jax.ShapeDtypeStruct((R, dc), BF16),
        compiler_params=_params("parallel", "parallel"), name="window_attention",
    )(sink, q, k, k, k, v, v, v, cache_k4, cache_v4)


def _merge_kernel(x_ref, mod_ref, gt_ref, ga_ref, pb_ref, ao_ref, wglu_ref, wb_ref, wc_ref, wo_ref, g_ref, o_ref):
    dm = x_ref.shape[1]
    z = _dot(ga_ref[...].astype(BF16), wglu_ref[...])
    ya = z[:, :dm] * _sigmoid(z[:, dm:])
    yb = _dot(pb_ref[...], wb_ref[...])
    yc = _dot(ao_ref[...], wc_ref[...])
    gts = _sigmoid(gt_ref[...].astype(F32))
    merged = gts[:, 0:dm] * ya + gts[:, dm:2 * dm] * yb + gts[:, 2 * dm:3 * dm] * yc
    mo = _dot(merged.astype(BF16), wo_ref[...])
    m = mod_ref[...]
    o_ref[...] = x_ref[...] + m[2:3] * _rms(mo, g_ref[...])


def _merge(x2, mod, gates, ga_t, pb, ao, w_glu, w_b_out, w_c_out, w_o, g_post, B, T):
    R, D = x2.shape
    tm = min(512, T)
    nt = T // tm
    bm = mod.shape[0]
    da = w_glu[0].shape[1]
    mod_map = (lambda i: (i // nt, 0, 0)) if bm > 1 else (lambda i: (0, 0, 0))
    row = lambda w: pl.BlockSpec((tm, w), lambda i: (i, 0))
    return pl.pallas_call(
        _merge_kernel,
        grid=(R // tm,),
        in_specs=[row(D), pl.BlockSpec((None, 8, D), mod_map), row(gates.shape[1]),
                  pl.BlockSpec((tm, da), lambda i: (i % nt, i // nt)),
                  row(pb.shape[1]), row(ao.shape[1]),
                  _wspec(w_glu), _wspec(w_b_out), _wspec(w_c_out), _wspec(w_o),
                  pl.BlockSpec((1, D), lambda i: (0, 0))],
        out_specs=row(D),
        out_shape=jax.ShapeDtypeStruct((R, D), F32),
        compiler_params=_params("parallel"), name="merge_out_proj",
    )(x2, mod, gates, ga_t, pb, ao, w_glu[0], w_b_out[0], w_c_out[0], w_o[0], g_post)


def _ffn_kernel(*refs, halo, nt, tf):
    if halo:
        (x_ref, xp_ref, xn_ref, mod_ref, gpre_ref, wu_ref, cw_ref, cb_ref, wd_ref, gpost_ref,
         o_ref, h_ref, z_ref, a_ref, y_ref) = refs
    else:
        (x_ref, mod_ref, gpre_ref, wu_ref, cw_ref, cb_ref, wd_ref, gpost_ref,
         o_ref, h_ref, z_ref, a_ref, y_ref) = refs
    i = pl.program_id(0)
    tm = x_ref.shape[0]
    dff = wd_ref.shape[0]
    m = mod_ref[...]
    norm = lambda x: _rms(x, gpre_ref[...]) * (1.0 + m[4:5]) + m[3:4]
    h_ref[HALO:HALO + tm, :] = norm(x_ref[...]).astype(BF16)
    if halo:
        keep_prev = jnp.where(i % nt == 0, 0.0, 1.0)
        keep_next = jnp.where(i % nt == nt - 1, 0.0, 1.0)
        h_ref[0:HALO, :] = (norm(xp_ref[...]) * keep_prev).astype(BF16)
        h_ref[HALO + tm:, :] = (norm(xn_ref[...]) * keep_next).astype(BF16)
    else:
        h_ref[0:HALO, :] = jnp.zeros((HALO, h_ref.shape[1]), BF16)
        h_ref[HALO + tm:, :] = jnp.zeros((HALO, h_ref.shape[1]), BF16)
    hx = h_ref[...]
    half = tm // 2
    nl = tf // LANES

    def conv(slot, col):
        z = _dot(hx, wu_ref[:, col:col + tf])
        ev, od = [], []
        for k in range(nl):
            sl = slot * nl + k
            z_ref[sl] = z[:, k * LANES:(k + 1) * LANES]
            taps = [z_ref[sl, pl.ds(HALO - 1 + d, half, stride=2), :] for d in range(4)]
            cw = cw_ref[:, col + k * LANES:col + (k + 1) * LANES]
            cb = cb_ref[:, col + k * LANES:col + (k + 1) * LANES]
            ev.append(taps[0] * cw[0:1] + taps[1] * cw[1:2] + taps[2] * cw[2:3] + cb)
            od.append(taps[1] * cw[0:1] + taps[2] * cw[1:2] + taps[3] * cw[2:3] + cb)
        return jnp.concatenate(ev, axis=1), jnp.concatenate(od, axis=1)

    for c in range(dff // tf):
        g_e, g_o = conv(2 * (c % 2), c * tf)
        v_e, v_o = conv(2 * (c % 2) + 1, dff + c * tf)
        a_ref[0:half, c * tf:(c + 1) * tf] = (jax.nn.silu(g_e) * v_e).astype(BF16)
        a_ref[half:tm, c * tf:(c + 1) * tf] = (jax.nn.silu(g_o) * v_o).astype(BF16)
    y = _dot(a_ref[...], wd_ref[...])
    for k in range(y.shape[1] // LANES):
        y_ref[k, pl.ds(0, half, stride=2), :] = y[0:half, k * LANES:(k + 1) * LANES]
        y_ref[k, pl.ds(1, half, stride=2), :] = y[half:tm, k * LANES:(k + 1) * LANES]
    y = jnp.concatenate([y_ref[k] for k in range(y.shape[1] // LANES)], axis=1)
    o_ref[...] = x_ref[...] + m[5:6] * _rms(y, gpost_ref[...])


def _conv_ffn(x2, mod, g_pre, w_up, conv_w, conv_b, w_down, g_post, B, T):
    R, D = x2.shape
    dff = w_down[0].shape[1]
    tf = 256
    tm = min(512, T)
    nt = T // tm
    halo = nt > 1
    bm = mod.shape[0]
    mod_map = (lambda i: (i // nt, 0, 0)) if bm > 1 else (lambda i: (0, 0, 0))
    hb = tm // HALO
    nhb = R // HALO
    const = lambda i: (0, 0)
    whole = lambda a: pl.BlockSpec(a.shape, const, pipeline_mode=pl.Buffered(1))
    in_specs = [pl.BlockSpec((tm, D), lambda i: (i, 0))]
    args = [x2]
    if halo:
        in_specs += [pl.BlockSpec((HALO, D), lambda i: (jnp.maximum(i * hb - 1, 0), 0)),
                     pl.BlockSpec((HALO, D), lambda i: (jnp.minimum((i + 1) * hb, nhb - 1), 0))]
        args += [x2, x2]
    in_specs += [pl.BlockSpec((None, 8, D), mod_map), pl.BlockSpec((1, D), const),
                 _wspec(w_up), whole(conv_w), whole(conv_b), _wspec(w_down), pl.BlockSpec((1, D), const)]
    args += [mod, g_pre, w_up[0], conv_w, conv_b, w_down[0], g_post]
    return pl.pallas_call(
        functools.partial(_ffn_kernel, halo=halo, nt=nt, tf=tf),
        grid=(R // tm,),
        in_specs=in_specs,
        out_specs=pl.BlockSpec((tm, D), lambda i: (i, 0)),
        out_shape=jax.ShapeDtypeStruct((R, D), F32),
        scratch_shapes=[pltpu.VMEM((tm + 2 * HALO, D), BF16),
                        pltpu.VMEM((4 * tf // LANES, tm + 2 * HALO, LANES), F32),
                        pltpu.VMEM((tm, dff), BF16),
                        pltpu.VMEM((D // LANES, tm, LANES), F32)],
        compiler_params=_params("parallel"), name="conv_ffn",
    )(*args)


def _rope_tables(T):
    t = jnp.arange(T)
    row = (t // GRID_W).astype(F32)[:, None]
    col = (t % GRID_W).astype(F32)[:, None]
    half = HEAD_DIM // 2
    inv = ROPE_BASE ** (-jnp.arange(0, half, 2, dtype=F32) / half)
    lane = jnp.arange(LANES)
    freq = inv[lane % (half // 2)][None, :]
    ang = jnp.where(((lane % HEAD_DIM) < half)[None, :], row * freq, col * freq)
    sign = jnp.where((lane % half) < half // 2, -1.0, 1.0)[None, :]
    return jnp.cos(ang), jnp.sin(ang) * sign


def _layer(x2, mod, lp, B, T, ctx):
    D = x2.shape[1]
    rope_tabs = None if ctx is None else lp['rope']
    a_t, uv, q, k, v, gates = _in_proj(x2, mod, lp['g_pre_mix'], lp['w_in'], B, T, lp['dims'], rope_tabs)
    h0 = (None, None) if ctx is None else ctx[3]
    tabs, row0 = lp['s5_tabs']
    y_bwd, br_re, br_im = _s5_pass(a_t, (tabs, row0 + 1), B, T, True, h0=h0[1])
    ga_t, fw_re, fw_im = _s5_pass(a_t, (tabs, row0), B, T, False, h0=h0[0], fuse=(y_bwd, lp['d_skip']))
    pb = _sgu(uv, lp['g_sgu'], lp['w_spatial'], lp['sgu_bias'])
    if ctx is None:
        ao = _context_attention(q, k, v, lp['sink'], B, T)
    else:
        ao = _window_attention(q, k, v, ctx[0], ctx[1], ctx[2], lp['sink'], B, T)
    x2 = _merge(x2, mod, gates, ga_t.reshape(T, -1), pb, ao, lp['w_glu'], lp['w_b_out'], lp['w_c_out'],
                lp['w_o'], lp['g_post_mix'], B, T)
    x2 = _conv_ffn(x2, mod, lp['g_pre_ffn'], lp['w_up'], lp['conv_w'], lp['conv_b'], lp['w_down'],
                   lp['g_post_ffn'], B, T)
    return x2, (k, v, (fw_re, br_re), (fw_im, br_im))


def kernel(x_prompt, x_sample, cache_k, cache_v, state_ssm_re, state_ssm_im, c, c_ctx,
           w_mod, b_mod, g_pre_mix, g_post_mix, g_pre_ffn, g_post_ffn, w_in,
           lam_re, lam_im, log_step, b_re, b_im, c_re, c_im, d_skip, w_glu,
           g_sgu, w_spatial, b_spatial, w_b_out, sink, w_c_out, w_o,
           w_up, conv_w, conv_b, w_down):
    BP, TP, D = x_prompt.shape
    BS, TS, _ = x_sample.shape
    L = w_in.shape[0]
    G, N = lam_re.shape[2], lam_re.shape[3]
    P = b_re.shape[-1]
    d_a = G * P
    d_b2 = 2 * g_sgu.shape[1]
    d_c = w_c_out.shape[1]
    kv_w = N_KV * HEAD_DIM
    d_g = w_in.shape[2] - (d_a + d_b2 + d_c + 2 * kv_w)
    past = cache_k.shape[2]

    nc = 16
    cvecs = jnp.zeros((nc, D), F32).at[0].set(c_ctx).at[1:1 + BS].set(c)
    mod_all = _modulation(cvecs, w_mod, b_mod).reshape(L, nc, 6, D)
    mod_all = jnp.pad(mod_all, ((0, 0), (0, 0), (0, 2), (0, 0)))

    rope = _rope_tables(TS)
    cache_k4 = cache_k.reshape(BS, L, past, kv_w)
    cache_v4 = cache_v.reshape(BS, L, past, kv_w)

    xp = x_prompt.reshape(BP * TP, D)
    xs = x_sample.reshape(BS * TS, D)
    new_k, new_v, new_re, new_im = [], [], [], []
    s5_tabs = _s5_tables(_s5_discretise(lam_re, lam_im, log_step, b_re, b_im, c_re, c_im), G, N, P)
    wb16 = {name: w.astype(BF16) for name, w in (
        ('w_in', w_in), ('w_glu', w_glu), ('w_spatial', w_spatial), ('w_b_out', w_b_out),
        ('w_c_out', w_c_out), ('w_o', w_o), ('w_up', w_up), ('w_down', w_down))}
    for l in range(L):
        tabs = (s5_tabs, 2 * l)
        lp = {
            'dims': (d_a, d_b2, d_c, kv_w, d_g),
            'g_pre_mix': g_pre_mix[l][None], 'g_post_mix': g_post_mix[l][None],
            'g_pre_ffn': g_pre_ffn[l][None], 'g_post_ffn': g_post_ffn[l][None],
            's5_tabs': tabs, 'd_skip': d_skip[l][None], 'g_sgu': g_sgu[l][None],
            'sgu_bias': jnp.repeat(b_spatial[l].T, g_sgu.shape[1] // b_spatial.shape[1], axis=1),
            'sink': sink[l], 'conv_w': conv_w[l], 'conv_b': conv_b[l][None], 'rope': rope,
        }
        lp.update({name: (w, l) for name, w in wb16.items()})
        xp, (k_l, v_l, f_re, f_im) = _layer(xp, mod_all[l, 0:1], lp, BP, TP, None)
        new_k.append(k_l.reshape(BP, TP, N_KV, HEAD_DIM))
        new_v.append(v_l.reshape(BP, TP, N_KV, HEAD_DIM))
        new_re.append(jnp.stack([f.reshape(BP, G, N) for f in f_re], axis=1))
        new_im.append(jnp.stack([f.reshape(BP, G, N) for f in f_im], axis=1))
        h0 = [(state_ssm_re[:, l, d].reshape(BS, G * N), state_ssm_im[:, l, d].reshape(BS, G * N)) for d in range(2)]
        xs, _ = _layer(xs, mod_all[l, 1:1 + BS], lp, BS, TS, (cache_k4, cache_v4, l, h0))
    return (xp.reshape(BP, TP, D), xs.reshape(BS, TS, D),
            jnp.stack(new_k, axis=1), jnp.stack(new_v, axis=1),
            jnp.stack(new_re, axis=1), jnp.stack(new_im, axis=1))
```

```python
import functools
import math

import jax
import jax.numpy as jnp
from jax import lax
from jax.experimental import pallas as pl
from jax.experimental.pallas import tpu as pltpu

F32 = jnp.float32
BF16 = jnp.bfloat16

GRID_W = 64
SSM_P = 16
SSM_N = 64
CHUNK = 128
N_HEADS = 8
N_KV = 2
HEAD_DIM = 64
BLOCK = 128
ROPE_BASE = 10000.0
EPS = 1e-6

LANES = 128
SUBLANES = 8
HALO = 16
VMEM_LIMIT = 56 * 1024 * 1024


def _params(*sem):
    return pltpu.CompilerParams(dimension_semantics=sem, vmem_limit_bytes=VMEM_LIMIT)


def _dot(a, b):
    return jnp.dot(a, b, preferred_element_type=F32)


def _wspec(w):
    arr, layer = w
    idx = (layer,) + (0,) * (arr.ndim - 1)
    return pl.BlockSpec((None,) + arr.shape[1:], lambda *_: idx, pipeline_mode=pl.Buffered(1))


def _sigmoid(x):
    return 0.5 * jnp.tanh(0.5 * x) + 0.5


def _rms(x, g):
    return x * lax.rsqrt(jnp.mean(x * x, axis=-1, keepdims=True) + EPS) * g


def _mod_kernel(c_ref, w_ref, b_ref, o_ref):
    a = jax.nn.silu(c_ref[...]).astype(BF16)
    o_ref[...] = _dot(a, w_ref[...].astype(BF16)) + b_ref[...]


def _modulation(cvecs, w_mod, b_mod):
    L, D, D6 = w_mod.shape
    NC = cvecs.shape[0]
    tn = D6 // 4
    return pl.pallas_call(
        _mod_kernel,
        grid=(L, D6 // tn),
        in_specs=[pl.BlockSpec((NC, D), lambda l, j: (0, 0)),
                  pl.BlockSpec((None, D, tn), lambda l, j: (l, 0, j)),
                  pl.BlockSpec((None, 1, tn), lambda l, j: (l, 0, j))],
        out_specs=pl.BlockSpec((None, NC, tn), lambda l, j: (l, 0, j)),
        out_shape=jax.ShapeDtypeStruct((L, NC, D6), F32),
        compiler_params=_params("parallel", "parallel"),
        name="modulation",
    )(cvecs, w_mod, b_mod.reshape(L, 1, D6))


def _rope(x, cos, sin):
    lane = lax.broadcasted_iota(jnp.int32, cos.shape, 1)
    first = (lane % 32) < 16
    outs = []
    for j in range(x.shape[1] // LANES):
        xb = x[:, j * LANES:(j + 1) * LANES]
        partner = jnp.where(first, pltpu.roll(xb, LANES - 16, 1), pltpu.roll(xb, 16, 1))
        outs.append(xb * cos + partner * sin)
    return outs[0] if len(outs) == 1 else jnp.concatenate(outs, axis=1)


INPROJ_TT = 64
INPROJ_PITCH = INPROJ_TT + 8


def _inproj_kernel(*refs, rope, offs):
    if rope:
        (x_ref, mod_ref, g_ref, w_ref, cos_ref, sin_ref,
         a_ref, uv_ref, q_ref, k_ref, v_ref, gt_ref, a_scr) = refs
    else:
        x_ref, mod_ref, g_ref, w_ref, a_ref, uv_ref, q_ref, k_ref, v_ref, gt_ref, a_scr = refs
    ob, oq, ok, ov, og, dn = offs
    bs, tt, D = x_ref.shape
    rows = bs * tt
    m = mod_ref[...]
    x = x_ref[...]
    h = x * lax.rsqrt(jnp.mean(x * x, axis=-1, keepdims=True) + EPS) * g_ref[...]
    h = h * (1.0 + m[:, 1:2, :]) + m[:, 0:1, :]
    hb = h.reshape(rows, D).astype(BF16)

    a = _dot(hb, w_ref[:, 0:ob])
    for b in range(bs):
        for j in range(ob // LANES):
            a_scr[j, b * INPROJ_PITCH:b * INPROJ_PITCH + tt, :] = a[b * tt:(b + 1) * tt, j * LANES:(j + 1) * LANES]
    for t in range(tt):
        for j in range(ob // LANES):
            a_ref[t // 2, t % 2, :, j * LANES:(j + 1) * LANES] = a_scr[j, pl.ds(t, bs, stride=INPROJ_PITCH), :]

    uv_ref[...] = _dot(hb, w_ref[:, ob:oq]).astype(BF16).reshape(bs, tt, oq - ob)
    q = _dot(hb, w_ref[:, oq:ok])
    k = _dot(hb, w_ref[:, ok:ov])
    if rope:
        cos = jnp.concatenate([cos_ref[...]] * bs, axis=0)
        sin = jnp.concatenate([sin_ref[...]] * bs, axis=0)
        q = _rope(q, cos, sin)
        k = _rope(k, cos, sin)
    q_ref[...] = q.reshape(bs, tt, ok - oq)
    k_ref[...] = k.reshape(bs, tt, ov - ok)
    v_ref[...] = _dot(hb, w_ref[:, ov:og]).reshape(bs, tt, og - ov)
    gt_ref[...] = _dot(hb, w_ref[:, og:dn]).astype(BF16).reshape(bs, tt, dn - og)


def _in_proj(x2, mod, g_pre, w_in, B, T, dims, rope_tabs=None):
    R, D = x2.shape
    d_a, d_b2, d_c, kv_w, d_g = dims
    ob = d_a
    oq = ob + d_b2
    ok = oq + d_c
    ov = ok + kv_w
    og = ov + kv_w
    dn = og + d_g
    bs, tt = SUBLANES, INPROJ_TT
    bm = mod.shape[0]
    mod_spec = (pl.BlockSpec((bs, 8, D), lambda g, i: (g, 0, 0)) if bm > 1
                else pl.BlockSpec((1, 8, D), lambda g, i: (0, 0, 0)))
    in_specs = [pl.BlockSpec((bs, tt, D), lambda g, i: (g, i, 0)),
                mod_spec,
                pl.BlockSpec((1, D), lambda g, i: (0, 0)),
                _wspec(w_in)]
    args = [x2.reshape(B, T, D), mod, g_pre, w_in[0]]
    if rope_tabs is not None:
        in_specs += [pl.BlockSpec((tt, LANES), lambda g, i: (i, 0))] * 2
        args += list(rope_tabs)
    seq = lambda w: pl.BlockSpec((bs, tt, w), lambda g, i: (g, i, 0))
    out_specs = [pl.BlockSpec((tt // 2, 2, bs, d_a), lambda g, i: (i, 0, g, 0)),
                 seq(d_b2), seq(d_c), seq(kv_w), seq(kv_w), seq(d_g)]
    out_shape = [jax.ShapeDtypeStruct((T // 2, 2, B, d_a), F32),
                 jax.ShapeDtypeStruct((B, T, d_b2), BF16),
                 jax.ShapeDtypeStruct((B, T, d_c), F32),
                 jax.ShapeDtypeStruct((B, T, kv_w), F32),
                 jax.ShapeDtypeStruct((B, T, kv_w), F32),
                 jax.ShapeDtypeStruct((B, T, d_g), BF16)]
    a4, uv, q, k, v, gates = pl.pallas_call(
        functools.partial(_inproj_kernel, rope=rope_tabs is not None, offs=(ob, oq, ok, ov, og, dn)),
        grid=(B // bs, T // tt), in_specs=in_specs, out_specs=out_specs, out_shape=out_shape,
        scratch_shapes=[pltpu.VMEM((d_a // LANES, bs * INPROJ_PITCH, LANES), F32)],
        compiler_params=_params("parallel", "parallel"), name="in_proj",
    )(*args)
    flat = lambda y: y.reshape(R, y.shape[-1])
    return a4, flat(uv), flat(q), flat(k), flat(v), flat(gates)


def _s5_disc_kernel(lr_ref, li_ref, ls_ref, br_ref, bi_ref, cr_ref, ci_ref,
                    l2r_ref, l2i_ref, bbr_ref, bbi_ref, lbbr_ref, lbbi_ref,
                    c1r_ref, c1i_ref, c2r_ref, c2i_ref, cb_ref, clb_ref):
    lr = lr_ref[...]
    li = li_ref[...]
    dt = jnp.exp(ls_ref[...])
    ar = lr * dt
    ai = li * dt
    mag = jnp.exp(ar)
    lb_re = (mag * jnp.cos(ai))[:, None, :]
    lb_im = (mag * jnp.sin(ai))[:, None, :]
    lr3 = lr[:, None, :]
    li3 = li[:, None, :]
    den = lr3 * lr3 + li3 * li3
    f_re = ((lb_re - 1.0) * lr3 + lb_im * li3) / den
    f_im = (lb_im * lr3 - (lb_re - 1.0) * li3) / den
    l2_re = lb_re * lb_re - lb_im * lb_im
    l2_im = 2.0 * lb_re * lb_im
    l2r_ref[...] = l2_re
    l2i_ref[...] = l2_im
    br = br_ref[...]
    bi = bi_ref[...]
    bb_re = f_re * br - f_im * bi
    bb_im = f_re * bi + f_im * br
    lbb_re = lb_re * bb_re - lb_im * bb_im
    lbb_im = lb_re * bb_im + lb_im * bb_re
    bbr_ref[...] = bb_re
    bbi_ref[...] = bb_im
    lbbr_ref[...] = lbb_re
    lbbi_ref[...] = lbb_im
    cr = cr_ref[...]
    ci = ci_ref[...]
    c1r_ref[...] = cr * lb_re - ci * lb_im
    c1i_ref[...] = cr * lb_im + ci * lb_re
    c2r_ref[...] = cr * l2_re - ci * l2_im
    c2i_ref[...] = cr * l2_im + ci * l2_re
    for q in range(br.shape[1]):
        cb_ref[:, q, :] = jnp.sum(cr * bb_re[:, q:q + 1, :] - ci * bb_im[:, q:q + 1, :], axis=-1)
        clb_ref[:, q, :] = jnp.sum(cr * lbb_re[:, q:q + 1, :] - ci * lbb_im[:, q:q + 1, :], axis=-1)


def _s5_discretise(lam_re, lam_im, log_step, b_re, b_im, c_re, c_im):
    N = lam_re.shape[-1]
    P = b_re.shape[-1]
    rows = lam_re.size // N
    tr = lambda b: jnp.swapaxes(b, -1, -2).reshape(rows, P, N)
    vec = jax.ShapeDtypeStruct((rows, 1, N), F32)
    mat = jax.ShapeDtypeStruct((rows, P, N), F32)
    sq = jax.ShapeDtypeStruct((rows, P, P), F32)
    return pl.pallas_call(_s5_disc_kernel, out_shape=[vec, vec] + [mat] * 8 + [sq, sq], name="s5_discretise")(
        lam_re.reshape(rows, N), lam_im.reshape(rows, N), log_step.reshape(rows, 1), tr(b_re), tr(b_im),
        c_re.reshape(rows, P, N), c_im.reshape(rows, P, N))


def _s5_tables(disc, G, N, P):
    l2_re, l2_im, bb_re, bb_im, lbb_re, lbb_im, c1_re, c1_im, c2_re, c2_im, cb, clb = disc
    GB = LANES // P
    NB = G // GB
    X = l2_re.shape[0] // G
    eye = jnp.eye(GB, dtype=F32)

    def w_in(x):
        return jnp.einsum('xkgpn,gh->xkgphn', x.reshape(X, NB, GB, P, N), eye).reshape(X, NB, GB * P, GB * N)

    def w_out(x):
        return jnp.einsum('xkgpn,gh->xkgnhp', x.reshape(X, NB, GB, P, N), eye).reshape(X, NB, GB * N, GB * P)

    def w_dir(x):
        return jnp.einsum('xkgqp,gh->xkgqhp', x.reshape(X, NB, GB, P, P), eye).reshape(X, NB, GB * P, GB * P)

    w_x = jnp.concatenate([jnp.concatenate([w_in(lbb_re), w_in(lbb_im)], axis=-1),
                           jnp.concatenate([w_in(bb_re), w_in(bb_im)], axis=-1)], axis=2).astype(BF16)
    w_y = jnp.concatenate([jnp.concatenate([w_out(c1_re), -w_out(c1_im)], axis=2),
                           jnp.concatenate([w_out(c2_re), -w_out(c2_im)], axis=2)], axis=-1).astype(BF16)
    d0, d1 = w_dir(cb), w_dir(clb)
    w_u = jnp.concatenate([jnp.concatenate([d0, d1], axis=-1),
                           jnp.concatenate([jnp.zeros_like(d0), d0], axis=-1)], axis=2).astype(BF16)
    lam2 = jnp.concatenate([l2_re.reshape(X, NB, 1, GB * N), l2_im.reshape(X, NB, 1, GB * N)], axis=-1)
    return w_x, w_y, w_u, lam2


def _s5_kernel(*refs, reverse, has_h0, fuse, nb, sw):
    it = iter(refs)
    u_ref, wx_ref, wy_ref, wu_ref, lam_ref = next(it), next(it), next(it), next(it), next(it)
    if has_h0:
        h0r_ref, h0i_ref = next(it), next(it)
    if fuse:
        yb_ref, dsk_ref = next(it), next(it)
    y_ref, fr_ref, fi_ref = next(it), next(it), next(it)
    xs_ref, st_ref = next(it), next(it)
    if fuse:
        yacc_ref = next(it)

    tc, _, bs, _ = u_ref.shape
    rows = tc * bs
    c = pl.program_id(1)
    first, second = (1, 0) if reverse else (0, 1)

    @pl.when(c == 0)
    def _():
        for j in range(nb):
            if has_h0:
                st_ref[j, :, 0:sw] = h0r_ref[:, j * sw:(j + 1) * sw]
                st_ref[j, :, sw:2 * sw] = h0i_ref[:, j * sw:(j + 1) * sw]
            else:
                st_ref[j] = jnp.zeros(st_ref.shape[1:], F32)

    for j in range(nb):
        cs = slice(j * LANES, (j + 1) * LANES)
        u2 = jnp.concatenate([u_ref[:, first, :, cs].reshape(rows, LANES),
                              u_ref[:, second, :, cs].reshape(rows, LANES)], axis=1).astype(BF16)
        xs_ref[j] = _dot(u2, wx_ref[j]).reshape(tc, bs, 2 * sw)
        lr = jnp.broadcast_to(lam_ref[j, :, 0:sw], (bs, sw))
        li = jnp.broadcast_to(lam_ref[j, :, sw:2 * sw], (bs, sw))
        s_re, s_im = st_ref[j, :, 0:sw], st_ref[j, :, sw:2 * sw]
        for i in range(tc):
            t = (tc - 1 - i) if reverse else i
            x_re = xs_ref[j, t, :, 0:sw]
            x_im = xs_ref[j, t, :, sw:2 * sw]
            xs_ref[j, t, :, 0:sw] = s_re
            xs_ref[j, t, :, sw:2 * sw] = s_im
            s_re, s_im = lr * s_re - li * s_im + x_re, lr * s_im + li * s_re + x_im
        st_ref[j, :, 0:sw] = s_re
        st_ref[j, :, sw:2 * sw] = s_im
        fr_ref[:, j * sw:(j + 1) * sw] = s_re
        fi_ref[:, j * sw:(j + 1) * sw] = s_im
        yj = _dot(xs_ref[j].reshape(rows, 2 * sw).astype(BF16), wy_ref[j]) + _dot(u2, wu_ref[j])
        for half, parity in ((0, first), (1, second)):
            yh = yj[:, half * LANES:(half + 1) * LANES]
            if fuse:
                yacc_ref[parity, :, cs] = yh
            else:
                y_ref[:, parity, :, cs] = yh.reshape(tc, bs, LANES)

    if fuse:
        da = nb * LANES
        for parity in range(2):
            y = (yacc_ref[parity] + yb_ref[:, parity].reshape(rows, da)
                 + dsk_ref[...] * u_ref[:, parity].reshape(rows, da))
            y_ref[:, parity] = jax.nn.gelu(y).reshape(tc, bs, da)


def _s5_pass(u4, tabs, B, T, reverse, h0=None, fuse=None):
    (w_x, w_y, w_u, lam2), which = tabs
    _, nb, _, sw2 = w_x.shape
    sw = sw2 // 2
    da = nb * LANES
    bs = SUBLANES
    tc = min(128, T // 2)
    nct = T // (2 * tc)
    tmap = (lambda g, c: (nct - 1 - c, 0, g, 0)) if reverse else (lambda g, c: (c, 0, g, 0))
    pick = lambda g, c: (which, 0, 0, 0)
    blk = pl.BlockSpec((tc, 2, bs, da), tmap)
    in_specs = [blk] + [pl.BlockSpec((None,) + a.shape[1:], pick) for a in (w_x, w_y, w_u, lam2)]
    args = [u4, w_x, w_y, w_u, lam2]
    if h0 is not None:
        in_specs += [pl.BlockSpec((bs, nb * sw), lambda g, c: (g, 0))] * 2
        args += list(h0)
    scratch = [pltpu.VMEM((nb, tc, bs, sw2), F32), pltpu.VMEM((nb, bs, sw2), F32)]
    if fuse is not None:
        y_other, d_skip = fuse
        in_specs += [blk, pl.BlockSpec((1, da), lambda g, c: (0, 0))]
        args += [y_other, d_skip]
        scratch.append(pltpu.VMEM((2, tc * bs, da), F32))
    fin = jax.ShapeDtypeStruct((B, nb * sw), F32)
    y, f_re, f_im = pl.pallas_call(
        functools.partial(_s5_kernel, reverse=reverse, has_h0=h0 is not None, fuse=fuse is not None, nb=nb, sw=sw),
        grid=(B // bs, nct), in_specs=in_specs,
        out_specs=[blk,
                   pl.BlockSpec((bs, nb * sw), lambda g, c: (g, 0)),
                   pl.BlockSpec((bs, nb * sw), lambda g, c: (g, 0))],
        out_shape=[jax.ShapeDtypeStruct((T // 2, 2, B, da), F32), fin, fin],
        scratch_shapes=scratch,
        compiler_params=_params("parallel", "arbitrary"),
        name="s5_scan_gelu" if fuse is not None else "s5_scan",
    )(*args)
    return y, f_re, f_im


def _sgu_kernel(uv_ref, g_ref, ws_ref, bias_ref, o_ref, *, nch):
    db = g_ref.shape[1]
    x = jax.nn.gelu(uv_ref[...].astype(F32))
    u = x[:, :db]
    v = x[:, db:]
    mu = jnp.mean(v, axis=-1, keepdims=True)
    vc = v - mu
    var = jnp.mean(vc * vc, axis=-1, keepdims=True)
    vn = (vc * lax.rsqrt(var + EPS) * g_ref[...]).astype(BF16)
    lane = lax.broadcasted_iota(jnp.int32, (CHUNK, LANES), 1)
    lo = lane < (LANES // 2)
    bias = bias_ref[...]
    for c in range(nch):
        r0 = c * CHUNK
        parts = []
        for hp in range(db // LANES):
            vb = vn[r0:r0 + CHUNK, hp * LANES:(hp + 1) * LANES]
            parts.append(jnp.where(lo, _dot(ws_ref[2 * hp], vb), _dot(ws_ref[2 * hp + 1], vb)))
        mixed = jnp.concatenate(parts, axis=1) + bias
        o_ref[r0:r0 + CHUNK, :] = (u[r0:r0 + CHUNK] * mixed).astype(BF16)


def _sgu(uv, g_sgu, w_spatial, bias_full):
    R, db2 = uv.shape
    db = db2 // 2
    nch = 4
    tm = nch * CHUNK
    return pl.pallas_call(
        functools.partial(_sgu_kernel, nch=nch),
        grid=(R // tm,),
        in_specs=[pl.BlockSpec((tm, db2), lambda i: (i, 0)),
                  pl.BlockSpec((1, db), lambda i: (0, 0)),
                  _wspec(w_spatial),
                  pl.BlockSpec(bias_full.shape, lambda i: (0, 0))],
        out_specs=pl.BlockSpec((tm, db), lambda i: (i, 0)),
        out_shape=jax.ShapeDtypeStruct((R, db), BF16),
        compiler_params=_params("parallel"), name="chunk_sgu",
    )(uv, g_sgu, w_spatial[0], bias_full)


def _attend(q, kcat, vcat, sink_ref, bias, o_ref):
    assert N_KV * HEAD_DIM == LANES and kcat.shape[1] == LANES
    M = q.shape[0]
    scale = HEAD_DIM ** -0.5
    lane = lax.broadcasted_iota(jnp.int32, (M, LANES), 1)
    lo = lane < HEAD_DIM
    heads_per_kv = N_HEADS // N_KV
    rows = []
    for blk in range(N_HEADS * HEAD_DIM // LANES):
        kv = (2 * blk) // heads_per_kv
        qb = q[:, blk * LANES:(blk + 1) * LANES] * scale
        qr = pltpu.roll(qb, HEAD_DIM, 1)
        if kv == 0:
            rows += [jnp.where(lo, qb, 0.0), jnp.where(lo, qr, 0.0)]
        else:
            rows += [jnp.where(lo, 0.0, qr), jnp.where(lo, 0.0, qb)]
    qst = jnp.concatenate(rows, axis=0).astype(BF16)
    s = lax.dot_general(qst, kcat.astype(BF16), (((1,), (1,)), ((), ())), preferred_element_type=F32)
    if bias is not None:
        cols, at = [], 0
        for off, blk_bias in bias:
            if off > at:
                cols.append(s[:, at:off])
            cols.append(s[:, off:off + LANES] + jnp.concatenate([blk_bias] * N_HEADS, axis=0))
            at = off + LANES
        cols.append(s[:, at:])
        s = jnp.concatenate(cols, axis=1)
    ps, es = [], []
    for h in range(N_HEADS):
        sh = s[h * M:(h + 1) * M]
        mh = jnp.maximum(jnp.max(sh, axis=-1, keepdims=True), sink_ref[h])
        ps.append(jnp.exp(sh - mh).astype(BF16))
        es.append(jnp.exp(sink_ref[h] - mh))
    vaug = jnp.concatenate([vcat.astype(BF16), jnp.ones(vcat.shape, BF16)], axis=1)
    oa = _dot(jnp.concatenate(ps, axis=0), vaug)
    o = [oa[h * M:(h + 1) * M, :LANES] * (1.0 / (oa[h * M:(h + 1) * M, LANES:] + es[h])) for h in range(N_HEADS)]
    for blk in range(N_HEADS * HEAD_DIM // LANES):
        kv = (2 * blk) // heads_per_kv
        oa_, ob_ = o[2 * blk], o[2 * blk + 1]
        if kv == 0:
            out = jnp.where(lo, oa_, pltpu.roll(ob_, HEAD_DIM, 1))
        else:
            out = jnp.where(lo, pltpu.roll(oa_, HEAD_DIM, 1), ob_)
        o_ref[:, blk * LANES:(blk + 1) * LANES] = out.astype(o_ref.dtype)


def _ctx_attn_kernel(sink_ref, q_ref, k_ref, v_ref, o_ref):
    _attend(q_ref[...], k_ref[...], v_ref[...], sink_ref, None, o_ref)


def _context_attention(q, k, v, sink, B, T):
    R, dc = q.shape
    kvw = k.shape[1]
    nq = T // BLOCK
    return pl.pallas_call(
        _ctx_attn_kernel,
        grid=(B, nq),
        in_specs=[pl.BlockSpec(memory_space=pltpu.SMEM),
                  pl.BlockSpec((BLOCK, dc), lambda b, i: (b * nq + i, 0)),
                  pl.BlockSpec((T, kvw), lambda b, i: (b, 0)),
                  pl.BlockSpec((T, kvw), lambda b, i: (b, 0))],
        out_specs=pl.BlockSpec((BLOCK, dc), lambda b, i: (b * nq + i, 0)),
        out_shape=jax.ShapeDtypeStruct((R, dc), BF16),
        compiler_params=_params("parallel", "parallel"), name="context_attention",
    )(sink, q, k, v)


def _win_attn_kernel(sink_ref, q_ref, kp_ref, kc_ref, kn_ref, vp_ref, vc_ref, vn_ref, ck_ref, cv_ref, o_ref, *,
                     nblk, nq):
    i = pl.program_id(1)
    k_blocks = [kp_ref[...]] + [kc_ref[n * BLOCK:(n + 1) * BLOCK] for n in range(nq)] + [kn_ref[...]]
    v_blocks = [vp_ref[...]] + [vc_ref[n * BLOCK:(n + 1) * BLOCK] for n in range(nq)] + [vn_ref[...]]
    qi = lax.broadcasted_iota(jnp.int32, (BLOCK, BLOCK), 0)
    kj = lax.broadcasted_iota(jnp.int32, (BLOCK, BLOCK), 1)
    edge = [jnp.where(i > 0, 0, BLOCK)] + [0] * nq + [jnp.where(nq * (i + 1) < nblk, 0, BLOCK)]
    for n in range(nq):
        kcat = jnp.concatenate(k_blocks[n:n + 3] + [ck_ref[...]], axis=0)
        vcat = jnp.concatenate(v_blocks[n:n + 3] + [cv_ref[...]], axis=0)
        prev_bias = jnp.where(qi + edge[n] <= kj, 0.0, -jnp.inf)
        next_bias = jnp.where(kj + edge[n + 2] <= qi, 0.0, -jnp.inf)
        _attend(q_ref[n * BLOCK:(n + 1) * BLOCK, :], kcat, vcat, sink_ref,
                [(0, prev_bias), (2 * BLOCK, next_bias)], o_ref.at[n * BLOCK:(n + 1) * BLOCK, :])


def _window_attention(q, k, v, cache_k4, cache_v4, layer, sink, B, T):
    R, dc = q.shape
    kvw = k.shape[1]
    nblk = T // BLOCK
    past = cache_k4.shape[2]
    nq = 8 if nblk % 8 == 0 else (4 if nblk % 4 == 0 else 2)
    ngrp = nblk // nq
    assert nblk == nq * ngrp
    cur = lambda b, i: (b * ngrp + i, 0)
    prv = lambda b, i: (b * nblk + jnp.maximum(nq * i - 1, 0), 0)
    nxt = lambda b, i: (b * nblk + jnp.minimum(nq * (i + 1), nblk - 1), 0)
    kvs = lambda f: pl.BlockSpec((BLOCK, kvw), f)
    kv2 = pl.BlockSpec((nq * BLOCK, kvw), cur)
    ctx = pl.BlockSpec((None, None, past, kvw), lambda b, i: (b, layer, 0, 0))
    return pl.pallas_call(
        functools.partial(_win_attn_kernel, nblk=nblk, nq=nq),
        grid=(B, ngrp),
        in_specs=[pl.BlockSpec(memory_space=pltpu.SMEM),
                  pl.BlockSpec((nq * BLOCK, dc), cur),
                  kvs(prv), kv2, kvs(nxt), kvs(prv), kv2, kvs(nxt), ctx, ctx],
        out_specs=pl.BlockSpec((nq * BLOCK, dc), cur),
        out_shape=jax.ShapeDtypeStruct((R, dc), BF16),
        compiler_params=_params("parallel", "parallel"), name="window_attention",
    )(sink, q, k, k, k, v, v, v, cache_k4, cache_v4)


def _merge_kernel(x_ref, mod_ref, gt_ref, ga_ref, pb_ref, ao_ref, wglu_ref, wb_ref, wc_ref, wo_ref, g_ref, o_ref):
    dm = x_ref.shape[1]
    z = _dot(ga_ref[...].astype(BF16), wglu_ref[...])
    ya = z[:, :dm] * _sigmoid(z[:, dm:])
    yb = _dot(pb_ref[...], wb_ref[...])
    yc = _dot(ao_ref[...], wc_ref[...])
    gts = _sigmoid(gt_ref[...].astype(F32))
    merged = gts[:, 0:dm] * ya + gts[:, dm:2 * dm] * yb + gts[:, 2 * dm:3 * dm] * yc
    mo = _dot(merged.astype(BF16), wo_ref[...])
    m = mod_ref[...]
    o_ref[...] = x_ref[...] + m[2:3] * _rms(mo, g_ref[...])


def _merge(x2, mod, gates, ga_t, pb, ao, w_glu, w_b_out, w_c_out, w_o, g_post, B, T):
    R, D = x2.shape
    tm = min(512, T)
    nt = T // tm
    bm = mod.shape[0]
    da = w_glu[0].shape[1]
    mod_map = (lambda i: (i // nt, 0, 0)) if bm > 1 else (lambda i: (0, 0, 0))
    row = lambda w: pl.BlockSpec((tm, w), lambda i: (i, 0))
    return pl.pallas_call(
        _merge_kernel,
        grid=(R // tm,),
        in_specs=[row(D), pl.BlockSpec((None, 8, D), mod_map), row(gates.shape[1]),
                  pl.BlockSpec((tm, da), lambda i: (i % nt, i // nt)),
                  row(pb.shape[1]), row(ao.shape[1]),
                  _wspec(w_glu), _wspec(w_b_out), _wspec(w_c_out), _wspec(w_o),
                  pl.BlockSpec((1, D), lambda i: (0, 0))],
        out_specs=row(D),
        out_shape=jax.ShapeDtypeStruct((R, D), F32),
        compiler_params=_params("parallel"), name="merge_out_proj",
    )(x2, mod, gates, ga_t, pb, ao, w_glu[0], w_b_out[0], w_c_out[0], w_o[0], g_post)


def _ffn_kernel(*refs, halo, nt, tf):
    if halo:
        (x_ref, xp_ref, xn_ref, mod_ref, gpre_ref, wu_ref, cw_ref, cb_ref, wd_ref, gpost_ref,
         o_ref, h_ref, z_ref, a_ref, y_ref) = refs
    else:
        (x_ref, mod_ref, gpre_ref, wu_ref, cw_ref, cb_ref, wd_ref, gpost_ref,
         o_ref, h_ref, z_ref, a_ref, y_ref) = refs
    i = pl.program_id(0)
    tm = x_ref.shape[0]
    dff = wd_ref.shape[0]
    m = mod_ref[...]
    norm = lambda x: _rms(x, gpre_ref[...]) * (1.0 + m[4:5]) + m[3:4]
    h_ref[HALO:HALO + tm, :] = norm(x_ref[...]).astype(BF16)
    if halo:
        keep_prev = jnp.where(i % nt == 0, 0.0, 1.0)
        keep_next = jnp.where(i % nt == nt - 1, 0.0, 1.0)
        h_ref[0:HALO, :] = (norm(xp_ref[...]) * keep_prev).astype(BF16)
        h_ref[HALO + tm:, :] = (norm(xn_ref[...]) * keep_next).astype(BF16)
    else:
        h_ref[0:HALO, :] = jnp.zeros((HALO, h_ref.shape[1]), BF16)
        h_ref[HALO + tm:, :] = jnp.zeros((HALO, h_ref.shape[1]), BF16)
    hx = h_ref[...]
    half = tm // 2
    nl = tf // LANES

    def conv(slot, col):
        z = _dot(hx, wu_ref[:, col:col + tf])
        ev, od = [], []
        for k in range(nl):
            sl = slot * nl + k
            z_ref[sl] = z[:, k * LANES:(k + 1) * LANES]
            taps = [z_ref[sl, pl.ds(HALO - 1 + d, half, stride=2), :] for d in range(4)]
            cw = cw_ref[:, col + k * LANES:col + (k + 1) * LANES]
            cb = cb_ref[:, col + k * LANES:col + (k + 1) * LANES]
            ev.append(taps[0] * cw[0:1] + taps[1] * cw[1:2] + taps[2] * cw[2:3] + cb)
            od.append(taps[1] * cw[0:1] + taps[2] * cw[1:2] + taps[3] * cw[2:3] + cb)
        return jnp.concatenate(ev, axis=1), jnp.concatenate(od, axis=1)

    for c in range(dff // tf):
        g_e, g_o = conv(2 * (c % 2), c * tf)
        v_e, v_o = conv(2 * (c % 2) + 1, dff + c * tf)
        a_ref[0:half, c * tf:(c + 1) * tf] = (jax.nn.silu(g_e) * v_e).astype(BF16)
        a_ref[half:tm, c * tf:(c + 1) * tf] = (jax.nn.silu(g_o) * v_o).astype(BF16)
    y = _dot(a_ref[...], wd_ref[...])
    for k in range(y.shape[1] // LANES):
        y_ref[k, pl.ds(0, half, stride=2), :] = y[0:half, k * LANES:(k + 1) * LANES]
        y_ref[k, pl.ds(1, half, stride=2), :] = y[half:tm, k * LANES:(k + 1) * LANES]
    y = jnp.concatenate([y_ref[k] for k in range(y.shape[1] // LANES)], axis=1)
    o_ref[...] = x_ref[...] + m[5:6] * _rms(y, gpost_ref[...])


def _conv_ffn(x2, mod, g_pre, w_up, conv_w, conv_b, w_down, g_post, B, T):
    R, D = x2.shape
    dff = w_down[0].shape[1]
    tf = 256
    tm = min(512, T)
    nt = T // tm
    halo = nt > 1
    bm = mod.shape[0]
    mod_map = (lambda i: (i // nt, 0, 0)) if bm > 1 else (lambda i: (0, 0, 0))
    hb = tm // HALO
    nhb = R // HALO
    const = lambda i: (0, 0)
    whole = lambda a: pl.BlockSpec(a.shape, const, pipeline_mode=pl.Buffered(1))
    in_specs = [pl.BlockSpec((tm, D), lambda i: (i, 0))]
    args = [x2]
    if halo:
        in_specs += [pl.BlockSpec((HALO, D), lambda i: (jnp.maximum(i * hb - 1, 0), 0)),
                     pl.BlockSpec((HALO, D), lambda i: (jnp.minimum((i + 1) * hb, nhb - 1), 0))]
        args += [x2, x2]
    in_specs += [pl.BlockSpec((None, 8, D), mod_map), pl.BlockSpec((1, D), const),
                 _wspec(w_up), whole(conv_w), whole(conv_b), _wspec(w_down), pl.BlockSpec((1, D), const)]
    args += [mod, g_pre, w_up[0], conv_w, conv_b, w_down[0], g_post]
    return pl.pallas_call(
        functools.partial(_ffn_kernel, halo=halo, nt=nt, tf=tf),
        grid=(R // tm,),
        in_specs=in_specs,
        out_specs=pl.BlockSpec((tm, D), lambda i: (i, 0)),
        out_shape=jax.ShapeDtypeStruct((R, D), F32),
        scratch_shapes=[pltpu.VMEM((tm + 2 * HALO, D), BF16),
                        pltpu.VMEM((4 * tf // LANES, tm + 2 * HALO, LANES), F32),
                        pltpu.VMEM((tm, dff), BF16),
                        pltpu.VMEM((D // LANES, tm, LANES), F32)],
        compiler_params=_params("parallel"), name="conv_ffn",
    )(*args)


def _rope_tables(T):
    t = jnp.arange(T)
    row = (t // GRID_W).astype(F32)[:, None]
    col = (t % GRID_W).astype(F32)[:, None]
    half = HEAD_DIM // 2
    inv = ROPE_BASE ** (-jnp.arange(0, half, 2, dtype=F32) / half)
    lane = jnp.arange(LANES)
    freq = inv[lane % (half // 2)][None, :]
    ang = jnp.where(((lane % HEAD_DIM) < half)[None, :], row * freq, col * freq)
    sign = jnp.where((lane % half) < half // 2, -1.0, 1.0)[None, :]
    return jnp.cos(ang), jnp.sin(ang) * sign


def _layer(x2, mod, lp, B, T, ctx):
    D = x2.shape[1]
    rope_tabs = None if ctx is None else lp['rope']
    a_t, uv, q, k, v, gates = _in_proj(x2, mod, lp['g_pre_mix'], lp['w_in'], B, T, lp['dims'], rope_tabs)
    h0 = (None, None) if ctx is None else ctx[3]
    tabs, row0 = lp['s5_tabs']
    y_bwd, br_re, br_im = _s5_pass(a_t, (tabs, row0 + 1), B, T, True, h0=h0[1])
    ga_t, fw_re, fw_im = _s5_pass(a_t, (tabs, row0), B, T, False, h0=h0[0], fuse=(y_bwd, lp['d_skip']))
    pb = _sgu(uv, lp['g_sgu'], lp['w_spatial'], lp['sgu_bias'])
    if ctx is None:
        ao = _context_attention(q, k, v, lp['sink'], B, T)
    else:
        ao = _window_attention(q, k, v, ctx[0], ctx[1], ctx[2], lp['sink'], B, T)
    x2 = _merge(x2, mod, gates, ga_t.reshape(T, -1), pb, ao, lp['w_glu'], lp['w_b_out'], lp['w_c_out'],
                lp['w_o'], lp['g_post_mix'], B, T)
    x2 = _conv_ffn(x2, mod, lp['g_pre_ffn'], lp['w_up'], lp['conv_w'], lp['conv_b'], lp['w_down'],
                   lp['g_post_ffn'], B, T)
    return x2, (k, v, (fw_re, br_re), (fw_im, br_im))


def kernel(x_prompt, x_sample, cache_k, cache_v, state_ssm_re, state_ssm_im, c, c_ctx,
           w_mod, b_mod, g_pre_mix, g_post_mix, g_pre_ffn, g_post_ffn, w_in,
           lam_re, lam_im, log_step, b_re, b_im, c_re, c_im, d_skip, w_glu,
           g_sgu, w_spatial, b_spatial, w_b_out, sink, w_c_out, w_o,
           w_up, conv_w, conv_b, w_down):
    BP, TP, D = x_prompt.shape
    BS, TS, _ = x_sample.shape
    L = w_in.shape[0]
    G, N = lam_re.shape[2], lam_re.shape[3]
    P = b_re.shape[-1]
    d_a = G * P
    d_b2 = 2 * g_sgu.shape[1]
    d_c = w_c_out.shape[1]
    kv_w = N_KV * HEAD_DIM
    d_g = w_in.shape[2] - (d_a + d_b2 + d_c + 2 * kv_w)
    past = cache_k.shape[2]

    nc = 16
    cvecs = jnp.zeros((nc, D), F32).at[0].set(c_ctx).at[1:1 + BS].set(c)
    mod_all = _modulation(cvecs, w_mod, b_mod).reshape(L, nc, 6, D)
    mod_all = jnp.pad(mod_all, ((0, 0), (0, 0), (0, 2), (0, 0)))

    rope = _rope_tables(TS)
    cache_k4 = cache_k.reshape(BS, L, past, kv_w)
    cache_v4 = cache_v.reshape(BS, L, past, kv_w)

    xp = x_prompt.reshape(BP * TP, D)
    xs = x_sample.reshape(BS * TS, D)
    new_k, new_v, new_re, new_im = [], [], [], []
    s5_tabs = _s5_tables(_s5_discretise(lam_re, lam_im, log_step, b_re, b_im, c_re, c_im), G, N, P)
    wb16 = {name: w.astype(BF16) for name, w in (
        ('w_in', w_in), ('w_glu', w_glu), ('w_spatial', w_spatial), ('w_b_out', w_b_out),
        ('w_c_out', w_c_out), ('w_o', w_o), ('w_up', w_up), ('w_down', w_down))}
    for l in range(L):
        tabs = (s5_tabs, 2 * l)
        lp = {
            'dims': (d_a, d_b2, d_c, kv_w, d_g),
            'g_pre_mix': g_pre_mix[l][None], 'g_post_mix': g_post_mix[l][None],
            'g_pre_ffn': g_pre_ffn[l][None], 'g_post_ffn': g_post_ffn[l][None],
            's5_tabs': tabs, 'd_skip': d_skip[l][None], 'g_sgu': g_sgu[l][None],
            'sgu_bias': jnp.repeat(b_spatial[l].T, g_sgu.shape[1] // b_spatial.shape[1], axis=1),
            'sink': sink[l], 'conv_w': conv_w[l], 'conv_b': conv_b[l][None], 'rope': rope,
        }
        lp.update({name: (w, l) for name, w in wb16.items()})
        xp, (k_l, v_l, f_re, f_im) = _layer(xp, mod_all[l, 0:1], lp, BP, TP, None)
        new_k.append(k_l.reshape(BP, TP, N_KV, HEAD_DIM))
        new_v.append(v_l.reshape(BP, TP, N_KV, HEAD_DIM))
        new_re.append(jnp.stack([f.reshape(BP, G, N) for f in f_re], axis=1))
        new_im.append(jnp.stack([f.reshape(BP, G, N) for f in f_im], axis=1))
        h0 = [(state_ssm_re[:, l, d].reshape(BS, G * N), state_ssm_im[:, l, d].reshape(BS, G * N)) for d in range(2)]
        xs, _ = _layer(xs, mod_all[l, 1:1 + BS], lp, BS, TS, (cache_k4, cache_v4, l, h0))
    return (xp.reshape(BP, TP, D), xs.reshape(BS, TS, D),
            jnp.stack(new_k, axis=1), jnp.stack(new_v, axis=1),
            jnp.stack(new_re, axis=1), jnp.stack(new_im, axis=1))
```

```python
import functools
import math

import jax
import jax.numpy as jnp
from jax import lax
from jax.experimental import pallas as pl
from jax.experimental.pallas import tpu as pltpu

F32 = jnp.float32
BF16 = jnp.bfloat16

GRID_W = 64
SSM_P = 16
SSM_N = 64
CHUNK = 128
N_HEADS = 8
N_KV = 2
HEAD_DIM = 64
BLOCK = 128
ROPE_BASE = 10000.0
EPS = 1e-6

LANES = 128
SUBLANES = 8
HALO = 16
VMEM_LIMIT = 56 * 1024 * 1024


def _params(*sem):
    return pltpu.CompilerParams(dimension_semantics=sem, vmem_limit_bytes=VMEM_LIMIT)


def _dot(a, b):
    return jnp.dot(a, b, preferred_element_type=F32)


def _wspec(w):
    arr, layer = w
    idx = (layer,) + (0,) * (arr.ndim - 1)
    return pl.BlockSpec((None,) + arr.shape[1:], lambda *_: idx, pipeline_mode=pl.Buffered(1))


def _sigmoid(x):
    return 0.5 * jnp.tanh(0.5 * x) + 0.5


def _rms(x, g):
    return x * lax.rsqrt(jnp.mean(x * x, axis=-1, keepdims=True) + EPS) * g


def _mod_kernel(c_ref, w_ref, b_ref, o_ref):
    a = jax.nn.silu(c_ref[...]).astype(BF16)
    o_ref[...] = _dot(a, w_ref[...].astype(BF16)) + b_ref[...]


def _modulation(cvecs, w_mod, b_mod):
    L, D, D6 = w_mod.shape
    NC = cvecs.shape[0]
    tn = D6 // 4
    return pl.pallas_call(
        _mod_kernel,
        grid=(L, D6 // tn),
        in_specs=[pl.BlockSpec((NC, D), lambda l, j: (0, 0)),
                  pl.BlockSpec((None, D, tn), lambda l, j: (l, 0, j)),
                  pl.BlockSpec((None, 1, tn), lambda l, j: (l, 0, j))],
        out_specs=pl.BlockSpec((None, NC, tn), lambda l, j: (l, 0, j)),
        out_shape=jax.ShapeDtypeStruct((L, NC, D6), F32),
        compiler_params=_params("parallel", "parallel"),
        name="modulation",
    )(cvecs, w_mod, b_mod.reshape(L, 1, D6))


def _rope(x, cos, sin):
    lane = lax.broadcasted_iota(jnp.int32, cos.shape, 1)
    first = (lane % 32) < 16
    outs = []
    for j in range(x.shape[1] // LANES):
        xb = x[:, j * LANES:(j + 1) * LANES]
        partner = jnp.where(first, pltpu.roll(xb, LANES - 16, 1), pltpu.roll(xb, 16, 1))
        outs.append(xb * cos + partner * sin)
    return outs[0] if len(outs) == 1 else jnp.concatenate(outs, axis=1)


INPROJ_TT = 64
INPROJ_PITCH = INPROJ_TT + 8


def _inproj_kernel(*refs, rope, offs):
    if rope:
        (x_ref, mod_ref, g_ref, w_ref, cos_ref, sin_ref,
         a_ref, uv_ref, q_ref, k_ref, v_ref, gt_ref, a_scr) = refs
    else:
        x_ref, mod_ref, g_ref, w_ref, a_ref, uv_ref, q_ref, k_ref, v_ref, gt_ref, a_scr = refs
    ob, oq, ok, ov, og, dn = offs
    bs, tt, D = x_ref.shape
    rows = bs * tt
    m = mod_ref[...]
    x = x_ref[...]
    h = x * lax.rsqrt(jnp.mean(x * x, axis=-1, keepdims=True) + EPS) * g_ref[...]
    h = h * (1.0 + m[:, 1:2, :]) + m[:, 0:1, :]
    hb = h.reshape(rows, D).astype(BF16)

    a = _dot(hb, w_ref[:, 0:ob])
    for b in range(bs):
        for j in range(ob // LANES):
            a_scr[j, b * INPROJ_PITCH:b * INPROJ_PITCH + tt, :] = a[b * tt:(b + 1) * tt, j * LANES:(j + 1) * LANES]
    for t in range(tt):
        for j in range(ob // LANES):
            a_ref[t // 2, t % 2, :, j * LANES:(j + 1) * LANES] = a_scr[j, pl.ds(t, bs, stride=INPROJ_PITCH), :]

    uv_ref[...] = _dot(hb, w_ref[:, ob:oq]).astype(BF16).reshape(bs, tt, oq - ob)
    q = _dot(hb, w_ref[:, oq:ok])
    k = _dot(hb, w_ref[:, ok:ov])
    if rope:
        cos = jnp.concatenate([cos_ref[...]] * bs, axis=0)
        sin = jnp.concatenate([sin_ref[...]] * bs, axis=0)
        q = _rope(q, cos, sin)
        k = _rope(k, cos, sin)
    q_ref[...] = q.reshape(bs, tt, ok - oq)
    k_ref[...] = k.reshape(bs, tt, ov - ok)
    v_ref[...] = _dot(hb, w_ref[:, ov:og]).reshape(bs, tt, og - ov)
    gt_ref[...] = _dot(hb, w_ref[:, og:dn]).astype(BF16).reshape(bs, tt, dn - og)


def _in_proj(x2, mod, g_pre, w_in, B, T, dims, rope_tabs=None):
    R, D = x2.shape
    d_a, d_b2, d_c, kv_w, d_g = dims
    ob = d_a
    oq = ob + d_b2
    ok = oq + d_c
    ov = ok + kv_w
    og = ov + kv_w
    dn = og + d_g
    bs, tt = SUBLANES, INPROJ_TT
    bm = mod.shape[0]
    mod_spec = (pl.BlockSpec((bs, 8, D), lambda g, i: (g, 0, 0)) if bm > 1
                else pl.BlockSpec((1, 8, D), lambda g, i: (0, 0, 0)))
    in_specs = [pl.BlockSpec((bs, tt, D), lambda g, i: (g, i, 0)),
                mod_spec,
                pl.BlockSpec((1, D), lambda g, i: (0, 0)),
                _wspec(w_in)]
    args = [x2.reshape(B, T, D), mod, g_pre, w_in[0]]
    if rope_tabs is not None:
        in_specs += [pl.BlockSpec((tt, LANES), lambda g, i: (i, 0))] * 2
        args += list(rope_tabs)
    seq = lambda w: pl.BlockSpec((bs, tt, w), lambda g, i: (g, i, 0))
    out_specs = [pl.BlockSpec((tt // 2, 2, bs, d_a), lambda g, i: (i, 0, g, 0)),
                 seq(d_b2), seq(d_c), seq(kv_w), seq(kv_w), seq(d_g)]
    out_shape = [jax.ShapeDtypeStruct((T // 2, 2, B, d_a), F32),
                 jax.ShapeDtypeStruct((B, T, d_b2), BF16),
                 jax.ShapeDtypeStruct((B, T, d_c), F32),
                 jax.ShapeDtypeStruct((B, T, kv_w), F32),
                 jax.ShapeDtypeStruct((B, T, kv_w), F32),
                 jax.ShapeDtypeStruct((B, T, d_g), BF16)]
    a4, uv, q, k, v, gates = pl.pallas_call(
        functools.partial(_inproj_kernel, rope=rope_tabs is not None, offs=(ob, oq, ok, ov, og, dn)),
        grid=(B // bs, T // tt), in_specs=in_specs, out_specs=out_specs, out_shape=out_shape,
        scratch_shapes=[pltpu.VMEM((d_a // LANES, bs * INPROJ_PITCH, LANES), F32)],
        compiler_params=_params("parallel", "parallel"), name="in_proj",
    )(*args)
    flat = lambda y: y.reshape(R, y.shape[-1])
    return a4, flat(uv), flat(q), flat(k), flat(v), flat(gates)


def _s5_disc_kernel(lr_ref, li_ref, ls_ref, br_ref, bi_ref, cr_ref, ci_ref,
                    l2r_ref, l2i_ref, bbr_ref, bbi_ref, lbbr_ref, lbbi_ref,
                    c1r_ref, c1i_ref, c2r_ref, c2i_ref, cb_ref, clb_ref):
    lr = lr_ref[...]
    li = li_ref[...]
    dt = jnp.exp(ls_ref[...])
    ar = lr * dt
    ai = li * dt
    mag = jnp.exp(ar)
    lb_re = (mag * jnp.cos(ai))[:, None, :]
    lb_im = (mag * jnp.sin(ai))[:, None, :]
    lr3 = lr[:, None, :]
    li3 = li[:, None, :]
    den = lr3 * lr3 + li3 * li3
    f_re = ((lb_re - 1.0) * lr3 + lb_im * li3) / den
    f_im = (lb_im * lr3 - (lb_re - 1.0) * li3) / den
    l2_re = lb_re * lb_re - lb_im * lb_im
    l2_im = 2.0 * lb_re * lb_im
    l2r_ref[...] = l2_re
    l2i_ref[...] = l2_im
    br = br_ref[...]
    bi = bi_ref[...]
    bb_re = f_re * br - f_im * bi
    bb_im = f_re * bi + f_im * br
    lbb_re = lb_re * bb_re - lb_im * bb_im
    lbb_im = lb_re * bb_im + lb_im * bb_re
    bbr_ref[...] = bb_re
    bbi_ref[...] = bb_im
    lbbr_ref[...] = lbb_re
    lbbi_ref[...] = lbb_im
    cr = cr_ref[...]
    ci = ci_ref[...]
    c1r_ref[...] = cr * lb_re - ci * lb_im
    c1i_ref[...] = cr * lb_im + ci * lb_re
    c2r_ref[...] = cr * l2_re - ci * l2_im
    c2i_ref[...] = cr * l2_im + ci * l2_re
    for q in range(br.shape[1]):
        cb_ref[:, q, :] = jnp.sum(cr * bb_re[:, q:q + 1, :] - ci * bb_im[:, q:q + 1, :], axis=-1)
        clb_ref[:, q, :] = jnp.sum(cr * lbb_re[:, q:q + 1, :] - ci * lbb_im[:, q:q + 1, :], axis=-1)


def _s5_discretise(lam_re, lam_im, log_step, b_re, b_im, c_re, c_im):
    N = lam_re.shape[-1]
    P = b_re.shape[-1]
    rows = lam_re.size // N
    tr = lambda b: jnp.swapaxes(b, -1, -2).reshape(rows, P, N)
    vec = jax.ShapeDtypeStruct((rows, 1, N), F32)
    mat = jax.ShapeDtypeStruct((rows, P, N), F32)
    sq = jax.ShapeDtypeStruct((rows, P, P), F32)
    return pl.pallas_call(_s5_disc_kernel, out_shape=[vec, vec] + [mat] * 8 + [sq, sq], name="s5_discretise")(
        lam_re.reshape(rows, N), lam_im.reshape(rows, N), log_step.reshape(rows, 1), tr(b_re), tr(b_im),
        c_re.reshape(rows, P, N), c_im.reshape(rows, P, N))


def _s5_tables(disc, G, N, P):
    l2_re, l2_im, bb_re, bb_im, lbb_re, lbb_im, c1_re, c1_im, c2_re, c2_im, cb, clb = disc
    GB = LANES // P
    NB = G // GB
    X = l2_re.shape[0] // G
    eye = jnp.eye(GB, dtype=F32)

    def w_in(x):
        return jnp.einsum('xkgpn,gh->xkgphn', x.reshape(X, NB, GB, P, N), eye).reshape(X, NB, GB * P, GB * N)

    def w_out(x):
        return jnp.einsum('xkgpn,gh->xkgnhp', x.reshape(X, NB, GB, P, N), eye).reshape(X, NB, GB * N, GB * P)

    def w_dir(x):
        return jnp.einsum('xkgqp,gh->xkgqhp', x.reshape(X, NB, GB, P, P), eye).reshape(X, NB, GB * P, GB * P)

    w_x = jnp.concatenate([jnp.concatenate([w_in(lbb_re), w_in(lbb_im)], axis=-1),
                           jnp.concatenate([w_in(bb_re), w_in(bb_im)], axis=-1)], axis=2).astype(BF16)
    w_y = jnp.concatenate([jnp.concatenate([w_out(c1_re), -w_out(c1_im)], axis=2),
                           jnp.concatenate([w_out(c2_re), -w_out(c2_im)], axis=2)], axis=-1).astype(BF16)
    d0, d1 = w_dir(cb), w_dir(clb)
    w_u = jnp.concatenate([jnp.concatenate([d0, d1], axis=-1),
                           jnp.concatenate([jnp.zeros_like(d0), d0], axis=-1)], axis=2).astype(BF16)
    lam2 = jnp.concatenate([l2_re.reshape(X, NB, 1, GB * N), l2_im.reshape(X, NB, 1, GB * N)], axis=-1)
    return w_x, w_y, w_u, lam2


def _s5_kernel(*refs, reverse, has_h0, fuse, nb, sw):
    it = iter(refs)
    u_ref, wx_ref, wy_ref, wu_ref, lam_ref = next(it), next(it), next(it), next(it), next(it)
    if has_h0:
        h0r_ref, h0i_ref = next(it), next(it)
    if fuse:
        yb_ref, dsk_ref = next(it), next(it)
    y_ref, fr_ref, fi_ref = next(it), next(it), next(it)
    xs_ref, st_ref = next(it), next(it)
    if fuse:
        yacc_ref = next(it)

    tc, _, bs, _ = u_ref.shape
    rows = tc * bs
    c = pl.program_id(1)
    first, second = (1, 0) if reverse else (0, 1)

    @pl.when(c == 0)
    def _():
        for j in range(nb):
            if has_h0:
                st_ref[j, :, 0:sw] = h0r_ref[:, j * sw:(j + 1) * sw]
                st_ref[j, :, sw:2 * sw] = h0i_ref[:, j * sw:(j + 1) * sw]
            else:
                st_ref[j] = jnp.zeros(st_ref.shape[1:], F32)

    for j in range(nb):
        cs = slice(j * LANES, (j + 1) * LANES)
        u2 = jnp.concatenate([u_ref[:, first, :, cs].reshape(rows, LANES),
                              u_ref[:, second, :, cs].reshape(rows, LANES)], axis=1).astype(BF16)
        xs_ref[j] = _dot(u2, wx_ref[j]).reshape(tc, bs, 2 * sw)
        lr = jnp.broadcast_to(lam_ref[j, :, 0:sw], (bs, sw))
        li = jnp.broadcast_to(lam_ref[j, :, sw:2 * sw], (bs, sw))
        s_re, s_im = st_ref[j, :, 0:sw], st_ref[j, :, sw:2 * sw]
        for i in range(tc):
            t = (tc - 1 - i) if reverse else i
            x_re = xs_ref[j, t, :, 0:sw]
            x_im = xs_ref[j, t, :, sw:2 * sw]
            xs_ref[j, t, :, 0:sw] = s_re
            xs_ref[j, t, :, sw:2 * sw] = s_im
            s_re, s_im = lr * s_re - li * s_im + x_re, lr * s_im + li * s_re + x_im
        st_ref[j, :, 0:sw] = s_re
        st_ref[j, :, sw:2 * sw] = s_im
        fr_ref[:, j * sw:(j + 1) * sw] = s_re
        fi_ref[:, j * sw:(j + 1) * sw] = s_im
        yj = _dot(xs_ref[j].reshape(rows, 2 * sw).astype(BF16), wy_ref[j]) + _dot(u2, wu_ref[j])
        for half, parity in ((0, first), (1, second)):
            yh = yj[:, half * LANES:(half + 1) * LANES]
            if fuse:
                yacc_ref[parity, :, cs] = yh
            else:
                y_ref[:, parity, :, cs] = yh.reshape(tc, bs, LANES)

    if fuse:
        da = nb * LANES
        for parity in range(2):
            y = (yacc_ref[parity] + yb_ref[:, parity].reshape(rows, da)
                 + dsk_ref[...] * u_ref[:, parity].reshape(rows, da))
            y_ref[:, parity] = jax.nn.gelu(y).reshape(tc, bs, da)


def _s5_pass(u4, tabs, B, T, reverse, h0=None, fuse=None):
    (w_x, w_y, w_u, lam2), which = tabs
    _, nb, _, sw2 = w_x.shape
    sw = sw2 // 2
    da = nb * LANES
    bs = SUBLANES
    tc = min(128, T // 2)
    nct = T // (2 * tc)
    tmap = (lambda g, c: (nct - 1 - c, 0, g, 0)) if reverse else (lambda g, c: (c, 0, g, 0))
    pick = lambda g, c: (which, 0, 0, 0)
    blk = pl.BlockSpec((tc, 2, bs, da), tmap)
    in_specs = [blk] + [pl.BlockSpec((None,) + a.shape[1:], pick) for a in (w_x, w_y, w_u, lam2)]
    args = [u4, w_x, w_y, w_u, lam2]
    if h0 is not None:
        in_specs += [pl.BlockSpec((bs, nb * sw), lambda g, c: (g, 0))] * 2
        args += list(h0)
    scratch = [pltpu.VMEM((nb, tc, bs, sw2), F32), pltpu.VMEM((nb, bs, sw2), F32)]
    if fuse is not None:
        y_other, d_skip = fuse
        in_specs += [blk, pl.BlockSpec((1, da), lambda g, c: (0, 0))]
        args += [y_other, d_skip]
        scratch.append(pltpu.VMEM((2, tc * bs, da), F32))
    fin = jax.ShapeDtypeStruct((B, nb * sw), F32)
    y, f_re, f_im = pl.pallas_call(
        functools.partial(_s5_kernel, reverse=reverse, has_h0=h0 is not None, fuse=fuse is not None, nb=nb, sw=sw),
        grid=(B // bs, nct), in_specs=in_specs,
        out_specs=[blk,
                   pl.BlockSpec((bs, nb * sw), lambda g, c: (g, 0)),
                   pl.BlockSpec((bs, nb * sw), lambda g, c: (g, 0))],
        out_shape=[jax.ShapeDtypeStruct((T // 2, 2, B, da), F32), fin, fin],
        scratch_shapes=scratch,
        compiler_params=_params("parallel", "arbitrary"),
        name="s5_scan_gelu" if fuse is not None else "s5_scan",
    )(*args)
    return y, f_re, f_im


def _sgu_kernel(uv_ref, g_ref, ws_ref, bias_ref, o_ref, *, nch):
    db = g_ref.shape[1]
    x0 = uv_ref[...].astype(F32)
    c0 = math.sqrt(2.0 / math.pi)
    hx0 = 0.5 * x0
    x = hx0 + hx0 * jnp.tanh(x0 * (x0 * x0 * (0.044715 * c0) + c0))
    u = x[:, :db]
    v = x[:, db:]
    mu = jnp.mean(v, axis=-1, keepdims=True)
    vc = v - mu
    var = jnp.mean(vc * vc, axis=-1, keepdims=True)
    vn = (vc * lax.rsqrt(var + EPS) * g_ref[...]).astype(BF16)
    lane = lax.broadcasted_iota(jnp.int32, (CHUNK, LANES), 1)
    lo = lane < (LANES // 2)
    bias = bias_ref[...]
    for c in range(nch):
        r0 = c * CHUNK
        parts = []
        for hp in range(db // LANES):
            vb = vn[r0:r0 + CHUNK, hp * LANES:(hp + 1) * LANES]
            parts.append(jnp.where(lo, _dot(ws_ref[2 * hp], vb), _dot(ws_ref[2 * hp + 1], vb)))
        mixed = jnp.concatenate(parts, axis=1) + bias
        o_ref[r0:r0 + CHUNK, :] = (u[r0:r0 + CHUNK] * mixed).astype(BF16)


def _sgu(uv, g_sgu, w_spatial, bias_full):
    R, db2 = uv.shape
    db = db2 // 2
    nch = 4
    tm = nch * CHUNK
    return pl.pallas_call(
        functools.partial(_sgu_kernel, nch=nch),
        grid=(R // tm,),
        in_specs=[pl.BlockSpec((tm, db2), lambda i: (i, 0)),
                  pl.BlockSpec((1, db), lambda i: (0, 0)),
                  _wspec(w_spatial),
                  pl.BlockSpec(bias_full.shape, lambda i: (0, 0))],
        out_specs=pl.BlockSpec((tm, db), lambda i: (i, 0)),
        out_shape=jax.ShapeDtypeStruct((R, db), BF16),
        compiler_params=_params("parallel"), name="chunk_sgu",
    )(uv, g_sgu, w_spatial[0], bias_full)


def _attend(q, kcat, vcat, sink_ref, bias, o_ref):
    assert N_KV * HEAD_DIM == LANES and kcat.shape[1] == LANES
    M = q.shape[0]
    scale = HEAD_DIM ** -0.5
    lane = lax.broadcasted_iota(jnp.int32, (M, LANES), 1)
    lo = lane < HEAD_DIM
    heads_per_kv = N_HEADS // N_KV
    rows = []
    for blk in range(N_HEADS * HEAD_DIM // LANES):
        kv = (2 * blk) // heads_per_kv
        qb = q[:, blk * LANES:(blk + 1) * LANES] * scale
        qr = pltpu.roll(qb, HEAD_DIM, 1)
        if kv == 0:
            rows += [jnp.where(lo, qb, 0.0), jnp.where(lo, qr, 0.0)]
        else:
            rows += [jnp.where(lo, 0.0, qr), jnp.where(lo, 0.0, qb)]
    qst = jnp.concatenate(rows, axis=0).astype(BF16)
    s = lax.dot_general(qst, kcat.astype(BF16), (((1,), (1,)), ((), ())), preferred_element_type=F32)
    if bias is not None:
        cols, at = [], 0
        for off, blk_bias in bias:
            if off > at:
                cols.append(s[:, at:off])
            cols.append(s[:, off:off + LANES] + jnp.concatenate([blk_bias] * N_HEADS, axis=0))
            at = off + LANES
        cols.append(s[:, at:])
        s = jnp.concatenate(cols, axis=1)
    ps, es = [], []
    for h in range(N_HEADS):
        sh = s[h * M:(h + 1) * M]
        mh = jnp.maximum(jnp.max(sh, axis=-1, keepdims=True), sink_ref[h])
        ps.append(jnp.exp(sh - mh).astype(BF16))
        es.append(jnp.exp(sink_ref[h] - mh))
    vaug = jnp.concatenate([vcat.astype(BF16), jnp.ones(vcat.shape, BF16)], axis=1)
    oa = _dot(jnp.concatenate(ps, axis=0), vaug)
    o = [oa[h * M:(h + 1) * M, :LANES] * (1.0 / (oa[h * M:(h + 1) * M, LANES:] + es[h])) for h in range(N_HEADS)]
    for blk in range(N_HEADS * HEAD_DIM // LANES):
        kv = (2 * blk) // heads_per_kv
        oa_, ob_ = o[2 * blk], o[2 * blk + 1]
        if kv == 0:
            out = jnp.where(lo, oa_, pltpu.roll(ob_, HEAD_DIM, 1))
        else:
            out = jnp.where(lo, pltpu.roll(oa_, HEAD_DIM, 1), ob_)
        o_ref[:, blk * LANES:(blk + 1) * LANES] = out.astype(o_ref.dtype)


def _ctx_attn_kernel(sink_ref, q_ref, k_ref, v_ref, o_ref):
    for n in range(q_ref.shape[0] // BLOCK):
        _attend(q_ref[n * BLOCK:(n + 1) * BLOCK, :], k_ref[...], v_ref[...], sink_ref, None,
                o_ref.at[n * BLOCK:(n + 1) * BLOCK, :])


def _context_attention(q, k, v, sink, B, T):
    R, dc = q.shape
    kvw = k.shape[1]
    return pl.pallas_call(
        _ctx_attn_kernel,
        grid=(B,),
        in_specs=[pl.BlockSpec(memory_space=pltpu.SMEM),
                  pl.BlockSpec((T, dc), lambda b: (b, 0)),
                  pl.BlockSpec((T, kvw), lambda b: (b, 0)),
                  pl.BlockSpec((T, kvw), lambda b: (b, 0))],
        out_specs=pl.BlockSpec((T, dc), lambda b: (b, 0)),
        out_shape=jax.ShapeDtypeStruct((R, dc), BF16),
        compiler_params=_params("parallel"), name="context_attention",
    )(sink, q, k, v)


def _win_attn_kernel(sink_ref, q_ref, kp_ref, kc_ref, kn_ref, vp_ref, vc_ref, vn_ref, ck_ref, cv_ref, o_ref, *,
                     nblk, nq):
    i = pl.program_id(1)
    k_blocks = [kp_ref[...]] + [kc_ref[n * BLOCK:(n + 1) * BLOCK] for n in range(nq)] + [kn_ref[...]]
    v_blocks = [vp_ref[...]] + [vc_ref[n * BLOCK:(n + 1) * BLOCK] for n in range(nq)] + [vn_ref[...]]
    qi = lax.broadcasted_iota(jnp.int32, (BLOCK, BLOCK), 0)
    kj = lax.broadcasted_iota(jnp.int32, (BLOCK, BLOCK), 1)
    edge = [jnp.where(i > 0, 0, BLOCK)] + [0] * nq + [jnp.where(nq * (i + 1) < nblk, 0, BLOCK)]
    for n in range(nq):
        kcat = jnp.concatenate(k_blocks[n:n + 3] + [ck_ref[...]], axis=0)
        vcat = jnp.concatenate(v_blocks[n:n + 3] + [cv_ref[...]], axis=0)
        prev_bias = jnp.where(qi + edge[n] <= kj, 0.0, -jnp.inf)
        next_bias = jnp.where(kj + edge[n + 2] <= qi, 0.0, -jnp.inf)
        _attend(q_ref[n * BLOCK:(n + 1) * BLOCK, :], kcat, vcat, sink_ref,
                [(0, prev_bias), (2 * BLOCK, next_bias)], o_ref.at[n * BLOCK:(n + 1) * BLOCK, :])


def _window_attention(q, k, v, cache_k4, cache_v4, layer, sink, B, T):
    R, dc = q.shape
    kvw = k.shape[1]
    nblk = T // BLOCK
    past = cache_k4.shape[2]
    nq = 8 if nblk % 8 == 0 else (4 if nblk % 4 == 0 else 2)
    ngrp = nblk // nq
    assert nblk == nq * ngrp
    cur = lambda b, i: (b * ngrp + i, 0)
    prv = lambda b, i: (b * nblk + jnp.maximum(nq * i - 1, 0), 0)
    nxt = lambda b, i: (b * nblk + jnp.minimum(nq * (i + 1), nblk - 1), 0)
    kvs = lambda f: pl.BlockSpec((BLOCK, kvw), f)
    kv2 = pl.BlockSpec((nq * BLOCK, kvw), cur)
    ctx = pl.BlockSpec((None, None, past, kvw), lambda b, i: (b, layer, 0, 0))
    return pl.pallas_call(
        functools.partial(_win_attn_kernel, nblk=nblk, nq=nq),
        grid=(B, ngrp),
        in_specs=[pl.BlockSpec(memory_space=pltpu.SMEM),
                  pl.BlockSpec((nq * BLOCK, dc), cur),
                  kvs(prv), kv2, kvs(nxt), kvs(prv), kv2, kvs(nxt), ctx, ctx],
        out_specs=pl.BlockSpec((nq * BLOCK, dc), cur),
        out_shape=jax.ShapeDtypeStruct((R, dc), BF16),
        compiler_params=_params("parallel", "parallel"), name="window_attention",
    )(sink, q, k, k, k, v, v, v, cache_k4, cache_v4)


def _merge_kernel(x_ref, mod_ref, gt_ref, ga_ref, pb_ref, ao_ref, wglu_ref, wb_ref, wc_ref, wo_ref, g_ref, o_ref):
    dm = x_ref.shape[1]
    z = _dot(ga_ref[...].astype(BF16), wglu_ref[...])
    ya = z[:, :dm] * _sigmoid(z[:, dm:])
    yb = _dot(pb_ref[...], wb_ref[...])
    yc = _dot(ao_ref[...], wc_ref[...])
    gts = _sigmoid(gt_ref[...].astype(F32))
    merged = gts[:, 0:dm] * ya + gts[:, dm:2 * dm] * yb + gts[:, 2 * dm:3 * dm] * yc
    mo = _dot(merged.astype(BF16), wo_ref[...])
    m = mod_ref[...]
    o_ref[...] = x_ref[...] + m[2:3] * _rms(mo, g_ref[...])


def _merge(x2, mod, gates, ga_t, pb, ao, w_glu, w_b_out, w_c_out, w_o, g_post, B, T):
    R, D = x2.shape
    tm = min(512, T)
    nt = T // tm
    bm = mod.shape[0]
    da = w_glu[0].shape[1]
    mod_map = (lambda i: (i // nt, 0, 0)) if bm > 1 else (lambda i: (0, 0, 0))
    row = lambda w: pl.BlockSpec((tm, w), lambda i: (i, 0))
    return pl.pallas_call(
        _merge_kernel,
        grid=(R // tm,),
        in_specs=[row(D), pl.BlockSpec((None, 8, D), mod_map), row(gates.shape[1]),
                  pl.BlockSpec((tm, da), lambda i: (i % nt, i // nt)),
                  row(pb.shape[1]), row(ao.shape[1]),
                  _wspec(w_glu), _wspec(w_b_out), _wspec(w_c_out), _wspec(w_o),
                  pl.BlockSpec((1, D), lambda i: (0, 0))],
        out_specs=row(D),
        out_shape=jax.ShapeDtypeStruct((R, D), F32),
        compiler_params=_params("parallel"), name="merge_out_proj",
    )(x2, mod, gates, ga_t, pb, ao, w_glu[0], w_b_out[0], w_c_out[0], w_o[0], g_post)


def _ffn_kernel(*refs, halo, nt, tf):
    if halo:
        (x_ref, xp_ref, xn_ref, mod_ref, gpre_ref, wu_ref, cw_ref, cb_ref, wd_ref, gpost_ref,
         o_ref, h_ref, z_ref, a_ref, y_ref) = refs
    else:
        (x_ref, mod_ref, gpre_ref, wu_ref, cw_ref, cb_ref, wd_ref, gpost_ref,
         o_ref, h_ref, z_ref, a_ref, y_ref) = refs
    i = pl.program_id(0)
    tm = x_ref.shape[0]
    dff = wd_ref.shape[0]
    m = mod_ref[...]
    norm = lambda x: _rms(x, gpre_ref[...]) * (1.0 + m[4:5]) + m[3:4]
    h_ref[HALO:HALO + tm, :] = norm(x_ref[...]).astype(BF16)
    if halo:
        keep_prev = jnp.where(i % nt == 0, 0.0, 1.0)
        keep_next = jnp.where(i % nt == nt - 1, 0.0, 1.0)
        h_ref[0:HALO, :] = (norm(xp_ref[...]) * keep_prev).astype(BF16)
        h_ref[HALO + tm:, :] = (norm(xn_ref[...]) * keep_next).astype(BF16)
    else:
        h_ref[0:HALO, :] = jnp.zeros((HALO, h_ref.shape[1]), BF16)
        h_ref[HALO + tm:, :] = jnp.zeros((HALO, h_ref.shape[1]), BF16)
    hx = h_ref[...]
    half = tm // 2
    nl = tf // LANES

    def conv(slot, col):
        z = _dot(hx, wu_ref[:, col:col + tf])
        ev, od = [], []
        for k in range(nl):
            sl = slot * nl + k
            z_ref[sl] = z[:, k * LANES:(k + 1) * LANES]
            taps = [z_ref[sl, pl.ds(HALO - 1 + d, half, stride=2), :] for d in range(4)]
            cw = cw_ref[:, col + k * LANES:col + (k + 1) * LANES]
            cb = cb_ref[:, col + k * LANES:col + (k + 1) * LANES]
            ev.append(taps[0] * cw[0:1] + taps[1] * cw[1:2] + taps[2] * cw[2:3] + cb)
            od.append(taps[1] * cw[0:1] + taps[2] * cw[1:2] + taps[3] * cw[2:3] + cb)
        return jnp.concatenate(ev, axis=1), jnp.concatenate(od, axis=1)

    for c in range(dff // tf):
        g_e, g_o = conv(2 * (c % 2), c * tf)
        v_e, v_o = conv(2 * (c % 2) + 1, dff + c * tf)
        a_ref[0:half, c * tf:(c + 1) * tf] = (jax.nn.silu(g_e) * v_e).astype(BF16)
        a_ref[half:tm, c * tf:(c + 1) * tf] = (jax.nn.silu(g_o) * v_o).astype(BF16)
    y = _dot(a_ref[...], wd_ref[...])
    for k in range(y.shape[1] // LANES):
        y_ref[k, pl.ds(0, half, stride=2), :] = y[0:half, k * LANES:(k + 1) * LANES]
        y_ref[k, pl.ds(1, half, stride=2), :] = y[half:tm, k * LANES:(k + 1) * LANES]
    y = jnp.concatenate([y_ref[k] for k in range(y.shape[1] // LANES)], axis=1)
    o_ref[...] = x_ref[...] + m[5:6] * _rms(y, gpost_ref[...])


def _conv_ffn(x2, mod, g_pre, w_up, conv_w, conv_b, w_down, g_post, B, T):
    R, D = x2.shape
    dff = w_down[0].shape[1]
    tf = 256
    tm = min(512, T)
    nt = T // tm
    halo = nt > 1
    bm = mod.shape[0]
    mod_map = (lambda i: (i // nt, 0, 0)) if bm > 1 else (lambda i: (0, 0, 0))
    hb = tm // HALO
    nhb = R // HALO
    const = lambda i: (0, 0)
    whole = lambda a: pl.BlockSpec(a.shape, const, pipeline_mode=pl.Buffered(1))
    in_specs = [pl.BlockSpec((tm, D), lambda i: (i, 0))]
    args = [x2]
    if halo:
        in_specs += [pl.BlockSpec((HALO, D), lambda i: (jnp.maximum(i * hb - 1, 0), 0)),
                     pl.BlockSpec((HALO, D), lambda i: (jnp.minimum((i + 1) * hb, nhb - 1), 0))]
        args += [x2, x2]
    in_specs += [pl.BlockSpec((None, 8, D), mod_map), pl.BlockSpec((1, D), const),
                 _wspec(w_up), whole(conv_w), whole(conv_b), _wspec(w_down), pl.BlockSpec((1, D), const)]
    args += [mod, g_pre, w_up[0], conv_w, conv_b, w_down[0], g_post]
    return pl.pallas_call(
        functools.partial(_ffn_kernel, halo=halo, nt=nt, tf=tf),
        grid=(R // tm,),
        in_specs=in_specs,
        out_specs=pl.BlockSpec((tm, D), lambda i: (i, 0)),
        out_shape=jax.ShapeDtypeStruct((R, D), F32),
        scratch_shapes=[pltpu.VMEM((tm + 2 * HALO, D), BF16),
                        pltpu.VMEM((4 * tf // LANES, tm + 2 * HALO, LANES), F32),
                        pltpu.VMEM((tm, dff), BF16),
                        pltpu.VMEM((D // LANES, tm, LANES), F32)],
        compiler_params=_params("parallel"), name="conv_ffn",
    )(*args)


def _rope_tables(T):
    t = jnp.arange(T)
    row = (t // GRID_W).astype(F32)[:, None]
    col = (t % GRID_W).astype(F32)[:, None]
    half = HEAD_DIM // 2
    inv = ROPE_BASE ** (-jnp.arange(0, half, 2, dtype=F32) / half)
    lane = jnp.arange(LANES)
    freq = inv[lane % (half // 2)][None, :]
    ang = jnp.where(((lane % HEAD_DIM) < half)[None, :], row * freq, col * freq)
    sign = jnp.where((lane % half) < half // 2, -1.0, 1.0)[None, :]
    return jnp.cos(ang), jnp.sin(ang) * sign


def _layer(x2, mod, lp, B, T, ctx):
    D = x2.shape[1]
    rope_tabs = None if ctx is None else lp['rope']
    a_t, uv, q, k, v, gates = _in_proj(x2, mod, lp['g_pre_mix'], lp['w_in'], B, T, lp['dims'], rope_tabs)
    h0 = (None, None) if ctx is None else ctx[3]
    tabs, row0 = lp['s5_tabs']
    y_bwd, br_re, br_im = _s5_pass(a_t, (tabs, row0 + 1), B, T, True, h0=h0[1])
    ga_t, fw_re, fw_im = _s5_pass(a_t, (tabs, row0), B, T, False, h0=h0[0], fuse=(y_bwd, lp['d_skip']))
    pb = _sgu(uv, lp['g_sgu'], lp['w_spatial'], lp['sgu_bias'])
    if ctx is None:
        ao = _context_attention(q, k, v, lp['sink'], B, T)
    else:
        ao = _window_attention(q, k, v, ctx[0], ctx[1], ctx[2], lp['sink'], B, T)
    x2 = _merge(x2, mod, gates, ga_t.reshape(T, -1), pb, ao, lp['w_glu'], lp['w_b_out'], lp['w_c_out'],
                lp['w_o'], lp['g_post_mix'], B, T)
    x2 = _conv_ffn(x2, mod, lp['g_pre_ffn'], lp['w_up'], lp['conv_w'], lp['conv_b'], lp['w_down'],
                   lp['g_post_ffn'], B, T)
    return x2, (k, v, (fw_re, br_re), (fw_im, br_im))


def kernel(x_prompt, x_sample, cache_k, cache_v, state_ssm_re, state_ssm_im, c, c_ctx,
           w_mod, b_mod, g_pre_mix, g_post_mix, g_pre_ffn, g_post_ffn, w_in,
           lam_re, lam_im, log_step, b_re, b_im, c_re, c_im, d_skip, w_glu,
           g_sgu, w_spatial, b_spatial, w_b_out, sink, w_c_out, w_o,
           w_up, conv_w, conv_b, w_down):
    BP, TP, D = x_prompt.shape
    BS, TS, _ = x_sample.shape
    L = w_in.shape[0]
    G, N = lam_re.shape[2], lam_re.shape[3]
    P = b_re.shape[-1]
    d_a = G * P
    d_b2 = 2 * g_sgu.shape[1]
    d_c = w_c_out.shape[1]
    kv_w = N_KV * HEAD_DIM
    d_g = w_in.shape[2] - (d_a + d_b2 + d_c + 2 * kv_w)
    past = cache_k.shape[2]

    nc = 16
    cvecs = jnp.zeros((nc, D), F32).at[0].set(c_ctx).at[1:1 + BS].set(c)
    mod_all = _modulation(cvecs, w_mod, b_mod).reshape(L, nc, 6, D)
    mod_all = jnp.pad(mod_all, ((0, 0), (0, 0), (0, 2), (0, 0)))

    rope = _rope_tables(TS)
    cache_k4 = cache_k.reshape(BS, L, past, kv_w)
    cache_v4 = cache_v.reshape(BS, L, past, kv_w)

    xp = x_prompt.reshape(BP * TP, D)
    xs = x_sample.reshape(BS * TS, D)
    new_k, new_v, new_re, new_im = [], [], [], []
    s5_tabs = _s5_tables(_s5_discretise(lam_re, lam_im, log_step, b_re, b_im, c_re, c_im), G, N, P)
    wb16 = {name: w.astype(BF16) for name, w in (
        ('w_in', w_in), ('w_glu', w_glu), ('w_spatial', w_spatial), ('w_b_out', w_b_out),
        ('w_c_out', w_c_out), ('w_o', w_o), ('w_up', w_up), ('w_down', w_down))}
    for l in range(L):
        tabs = (s5_tabs, 2 * l)
        lp = {
            'dims': (d_a, d_b2, d_c, kv_w, d_g),
            'g_pre_mix': g_pre_mix[l][None], 'g_post_mix': g_post_mix[l][None],
            'g_pre_ffn': g_pre_ffn[l][None], 'g_post_ffn': g_post_ffn[l][None],
            's5_tabs': tabs, 'd_skip': d_skip[l][None], 'g_sgu': g_sgu[l][None],
            'sgu_bias': jnp.repeat(b_spatial[l].T, g_sgu.shape[1] // b_spatial.shape[1], axis=1),
            'sink': sink[l], 'conv_w': conv_w[l], 'conv_b': conv_b[l][None], 'rope': rope,
        }
        lp.update({name: (w, l) for name, w in wb16.items()})
        xp, (k_l, v_l, f_re, f_im) = _layer(xp, mod_all[l, 0:1], lp, BP, TP, None)
        new_k.append(k_l.reshape(BP, TP, N_KV, HEAD_DIM))
        new_v.append(v_l.reshape(BP, TP, N_KV, HEAD_DIM))
        new_re.append(jnp.stack([f.reshape(BP, G, N) for f in f_re], axis=1))
        new_im.append(jnp.stack([f.reshape(BP, G, N) for f in f_im], axis=1))
        h0 = [(state_ssm_re[:, l, d].reshape(BS, G * N), state_ssm_im[:, l, d].reshape(BS, G * N)) for d in range(2)]
        xs, _ = _layer(xs, mod_all[l, 1:1 + BS], lp, BS, TS, (cache_k4, cache_v4, l, h0))
    return (xp.reshape(BP, TP, D), xs.reshape(BS, TS, D),
            jnp.stack(new_k, axis=1), jnp.stack(new_v, axis=1),
            jnp.stack(new_re, axis=1), jnp.stack(new_im, axis=1))
```
